```python
import jax, jax.numpy as jnp
from jax import lax
import numpy as np

D_MODEL = 1024
BATCH = 8
SEQ = 4096
DEPTH = 4
DEC_BATCH = 32
DEC_SEQ = 16
PAST_LEN = 2048

CHUNK = 64
Q_BLOCK = 128
N_MEM = 256
EPS = 1e-6
FOX_HEADS = 16
FOX_HEAD_DIM = D_MODEL // FOX_HEADS
FOX_WIDTH = FOX_HEADS * FOX_HEAD_DIM
FOX_IN = 3 * FOX_WIDTH + FOX_HEADS + FOX_WIDTH
FOX_SCALE = FOX_HEAD_DIM ** -0.5
MLA_HEADS = 16
MLA_NOPE = 64
MLA_ROPE = 32
MLA_V = 64
MLA_Q_LORA = 384
MLA_KV_LORA = 256
MLA_DOWN = MLA_Q_LORA + MLA_KV_LORA + MLA_ROPE
MLA_SCALE = (MLA_NOPE + MLA_ROPE) ** -0.5
ROPE_THETA = 10000.0
X_HEADS = 4
X_HEAD_DIM = D_MODEL // X_HEADS
X_WIDTH = X_HEADS * X_HEAD_DIM
X_SCALE = X_HEAD_DIM ** -0.5
D_FF = ((8 * D_MODEL + 3 * 256 - 1) // (3 * 256)) * 256
N_FOX = (DEPTH + 1) // 2
N_MLA = DEPTH // 2

kernel_name = "fox_mla_streaming_encoder_step"


def rms_normalize(x):
    x32 = x.astype(jnp.float32)
    return (x32 * lax.rsqrt(jnp.mean(x32 * x32, axis=-1, keepdims=True) + EPS)).astype(x.dtype)


def rmsnorm(x, g):
    return rms_normalize(x) * g


def rope(x, pos):
    half = x.shape[-1] // 2
    inv = ROPE_THETA ** (-jnp.arange(half, dtype=jnp.float32) / half)
    ang = pos.astype(jnp.float32)[:, None] * inv[None, :]
    if x.ndim == 4:
        ang = ang[:, None, :]
    cos, sin = jnp.cos(ang), jnp.sin(ang)
    x1 = x[..., :half].astype(jnp.float32)
    x2 = x[..., half:].astype(jnp.float32)
    return jnp.concatenate([x1 * cos - x2 * sin, x1 * sin + x2 * cos], axis=-1).astype(x.dtype)


def masked_softmax(scores, mask):
    return jax.nn.softmax(jnp.where(mask, scores, -jnp.inf), axis=-1)


def sweep_query_blocks(block_fn, q_arrays, q_pos):
    t = q_pos.shape[0]
    if t <= Q_BLOCK:
        return block_fn(*q_arrays, q_pos)
    nb = t // Q_BLOCK
    blocks = tuple(jnp.moveaxis(a.reshape((a.shape[0], nb, Q_BLOCK) + a.shape[2:]), 1, 0) for a in q_arrays)
    out = lax.map(lambda args: block_fn(*args), blocks + (q_pos.reshape(nb, Q_BLOCK),))
    out = jnp.moveaxis(out, 0, 1)
    return out.reshape((out.shape[0], t) + out.shape[3:])


def fox_mixer(h, q_pos, past, w_in, b_f, w_out):
    b, t, _ = h.shape
    proj = h @ w_in
    q, k, v, f_pre, gate = jnp.split(proj, [FOX_WIDTH, 2 * FOX_WIDTH, 3 * FOX_WIDTH, 3 * FOX_WIDTH + FOX_HEADS], axis=-1)
    q = q.reshape(b, t, FOX_HEADS, FOX_HEAD_DIM)
    k = k.reshape(b, t, FOX_HEADS, FOX_HEAD_DIM)
    v = v.reshape(b, t, FOX_HEADS, FOX_HEAD_DIM)
    logf = jax.nn.log_sigmoid((f_pre + b_f).astype(jnp.float32))
    if past is None:
        k_all, v_all, lf_all, k_pos = k, v, logf, q_pos
    else:
        k_all = jnp.concatenate([past[0], k], axis=1)
        v_all = jnp.concatenate([past[1], v], axis=1)
        lf_all = jnp.concatenate([past[2].astype(jnp.float32), logf], axis=1)
        k_pos = jnp.arange(k_all.shape[1], dtype=jnp.int32)
    cum = jnp.cumsum(lf_all, axis=1)
    ck = jnp.moveaxis(cum, 1, 2)[:, :, None, :]
    cq_all = cum[:, -t:]

    def block(qb, cqb, pq):
        s = jnp.einsum('bqhd,bkhd->bhqk', qb, k_all).astype(jnp.float32) * FOX_SCALE
        s = s + jnp.moveaxis(cqb, 1, 2)[..., None] - ck
        mask = k_pos[None, :] <= pq[:, None]
        p = masked_softmax(s, mask).astype(v_all.dtype)
        return jnp.einsum('bhqk,bkhd->bqhd', p, v_all)

    o = sweep_query_blocks(block, (q, cq_all), q_pos)
    o = o.reshape(b, t, FOX_WIDTH) * jax.nn.sigmoid(gate)
    return o @ w_out, (k, v, logf.astype(h.dtype))


def mla_mixer(h, q_pos, past, w_a, g_q, g_kv, w_qb, w_kvb, w_out):
    b, t, _ = h.shape
    a = h @ w_a
    c_q, c_kv, k_r = jnp.split(a, [MLA_Q_LORA, MLA_Q_LORA + MLA_KV_LORA], axis=-1)
    q = (rmsnorm(c_q, g_q) @ w_qb).reshape(b, t, MLA_HEADS, MLA_NOPE + MLA_ROPE)
    q_nope, q_rope = jnp.split(q, [MLA_NOPE], axis=-1)
    q_rope = rope(q_rope, q_pos)
    c_kv = rmsnorm(c_kv, g_kv)
    k_r = rope(k_r, q_pos)
    if past is None:
        ckv_all, kr_all, k_pos = c_kv, k_r, q_pos
    else:
        ckv_all = jnp.concatenate([past[0], c_kv], axis=1)
        kr_all = jnp.concatenate([past[1], k_r], axis=1)
        k_pos = jnp.arange(ckv_all.shape[1], dtype=jnp.int32)
    kv = (ckv_all @ w_kvb).reshape(b, ckv_all.shape[1], MLA_HEADS, MLA_NOPE + MLA_V)
    k_nope, v = jnp.split(kv, [MLA_NOPE], axis=-1)
    k_chunk = k_pos // CHUNK

    def block(qn, qr, pq):
        s = (jnp.einsum('bqhn,bkhn->bhqk', qn, k_nope)
             + jnp.einsum('bqhr,bkr->bhqk', qr, kr_all)).astype(jnp.float32) * MLA_SCALE
        mask = k_chunk[None, :] <= (pq // CHUNK)[:, None]
        p = masked_softmax(s, mask).astype(v.dtype)
        return jnp.einsum('bhqk,bkhd->bqhd', p, v)

    o = sweep_query_blocks(block, (q_nope, q_rope), q_pos)
    return o.reshape(b, t, MLA_HEADS * MLA_V) @ w_out, (c_kv, k_r)


def memory_kv(mem, g_mem, w_x_kv):
    m_l = rms_normalize(mem)[None] * g_mem[:, None, None, :]
    kv = jnp.einsum('lbnd,lde->lbne', m_l, w_x_kv)
    k, v = jnp.split(kv, 2, axis=-1)
    shp = (DEPTH, mem.shape[0], mem.shape[1], X_HEADS, X_HEAD_DIM)
    return k.reshape(shp), v.reshape(shp)


def cross_attn(h, mk, mv, w_q, w_o):
    b, t, _ = h.shape
    q = (h @ w_q).reshape(b, t, X_HEADS, X_HEAD_DIM)
    s = jnp.einsum('bqhd,bkhd->bhqk', q, mk).astype(jnp.float32) * X_SCALE
    p = jax.nn.softmax(s, axis=-1).astype(mv.dtype)
    o = jnp.einsum('bhqk,bkhd->bqhd', p, mv).reshape(b, t, X_WIDTH)
    return o @ w_o


def swiglu(h, w_gu, w_down):
    g, u = jnp.split(h @ w_gu, 2, axis=-1)
    return (jax.nn.silu(g) * u) @ w_down


def trunk(x, q_pos, fox_past, mla_past, mem_k, mem_v,
          g_mix, g_cross, g_ffn, g_final,
          w_fox_in, b_fox_f, w_fox_out,
          w_mla_a, g_mla_q, g_mla_kv, w_mla_qb, w_mla_kvb, w_mla_out,
          w_x_q, w_x_o, w_ffn_gu, w_ffn_down):
    fox_k, fox_v, fox_lf, mla_c, mla_r = [], [], [], [], []
    for i in range(DEPTH):
        j = i // 2
        h = rmsnorm(x, g_mix[i])
        if i % 2 == 0:
            past = None if fox_past is None else (fox_past[0][j], fox_past[1][j], fox_past[2][j])
            o, st = fox_mixer(h, q_pos, past, w_fox_in[j], b_fox_f[j], w_fox_out[j])
            fox_k.append(st[0]); fox_v.append(st[1]); fox_lf.append(st[2])
        else:
            past = None if mla_past is None else (mla_past[0][j], mla_past[1][j])
            o, st = mla_mixer(h, q_pos, past, w_mla_a[j], g_mla_q[j], g_mla_kv[j],
                              w_mla_qb[j], w_mla_kvb[j], w_mla_out[j])
            mla_c.append(st[0]); mla_r.append(st[1])
        x = x + o
        x = x + cross_attn(rmsnorm(x, g_cross[i]), mem_k[i], mem_v[i], w_x_q[i], w_x_o[i])
        x = x + swiglu(rmsnorm(x, g_ffn[i]), w_ffn_gu[i], w_ffn_down[i])
    y = rmsnorm(x, g_final)
    return y, jnp.stack(fox_k), jnp.stack(fox_v), jnp.stack(fox_lf), jnp.stack(mla_c), jnp.stack(mla_r)


def setup_inputs(seed: int = 0) -> dict:
    key = jax.random.key(seed)
    ks = iter(jax.random.split(key, 40))
    f32 = jnp.float32

    def nrm(shape, scale=1.0):
        return jax.random.normal(next(ks), shape, f32) * scale

    def gain(shape):
        return 1.0 + 0.05 * nrm(shape)

    d = D_MODEL
    return {
        "x_prompt": nrm((BATCH, SEQ, d)),
        "x_sample": nrm((DEC_BATCH, DEC_SEQ, d)),
        "mem_prompt": nrm((BATCH, N_MEM, d)),
        "cache_fox_k": nrm((N_FOX, DEC_BATCH, PAST_LEN, FOX_HEADS, FOX_HEAD_DIM)),
        "cache_fox_v": nrm((N_FOX, DEC_BATCH, PAST_LEN, FOX_HEADS, FOX_HEAD_DIM)),
        "cache_fox_logf": jax.nn.log_sigmoid(3.0 + 0.5 * nrm((N_FOX, DEC_BATCH, PAST_LEN, FOX_HEADS))),
        "cache_mla_ckv": nrm((N_MLA, DEC_BATCH, PAST_LEN, MLA_KV_LORA)),
        "cache_mla_krope": nrm((N_MLA, DEC_BATCH, PAST_LEN, MLA_ROPE)),
        "cache_mem_k": nrm((DEPTH, DEC_BATCH, N_MEM, X_HEADS, X_HEAD_DIM)),
        "cache_mem_v": nrm((DEPTH, DEC_BATCH, N_MEM, X_HEADS, X_HEAD_DIM)),
        "g_mix": gain((DEPTH, d)),
        "g_cross": gain((DEPTH, d)),
        "g_mem": gain((DEPTH, d)),
        "g_ffn": gain((DEPTH, d)),
        "g_final": gain((d,)),
        "w_fox_in": nrm((N_FOX, d, FOX_IN), d ** -0.5),
        "b_fox_f": 3.0 + 0.5 * nrm((N_FOX, FOX_HEADS)),
        "w_fox_out": nrm((N_FOX, FOX_WIDTH, d), FOX_WIDTH ** -0.5),
        "w_mla_a": nrm((N_MLA, d, MLA_DOWN), d ** -0.5),
        "g_mla_q": gain((N_MLA, MLA_Q_LORA)),
        "g_mla_kv": gain((N_MLA, MLA_KV_LORA)),
        "w_mla_qb": nrm((N_MLA, MLA_Q_LORA, MLA_HEADS * (MLA_NOPE + MLA_ROPE)), MLA_Q_LORA ** -0.5),
        "w_mla_kvb": nrm((N_MLA, MLA_KV_LORA, MLA_HEADS * (MLA_NOPE + MLA_V)), MLA_KV_LORA ** -0.5),
        "w_mla_out": nrm((N_MLA, MLA_HEADS * MLA_V, d), (MLA_HEADS * MLA_V) ** -0.5),
        "w_x_q": nrm((DEPTH, d, X_WIDTH), d ** -0.5),
        "w_x_kv": nrm((DEPTH, d, 2 * X_WIDTH), d ** -0.5),
        "w_x_o": nrm((DEPTH, X_WIDTH, d), X_WIDTH ** -0.5),
        "w_ffn_gu": nrm((DEPTH, d, 2 * D_FF), d ** -0.5),
        "w_ffn_down": nrm((DEPTH, D_FF, d), D_FF ** -0.5),
    }


def reference(x_prompt, x_sample, mem_prompt, cache_fox_k, cache_fox_v, cache_fox_logf,
              cache_mla_ckv, cache_mla_krope, cache_mem_k, cache_mem_v,
              g_mix, g_cross, g_mem, g_ffn, g_final,
              w_fox_in, b_fox_f, w_fox_out,
              w_mla_a, g_mla_q, g_mla_kv, w_mla_qb, w_mla_kvb, w_mla_out,
              w_x_q, w_x_kv, w_x_o, w_ffn_gu, w_ffn_down):
    mem_k_p, mem_v_p = memory_kv(mem_prompt, g_mem, w_x_kv)
    pos_p = jnp.arange(x_prompt.shape[1], dtype=jnp.int32)
    y_prompt, fk_p, fv_p, fl_p, mc_p, mr_p = trunk(
        x_prompt, pos_p, None, None, mem_k_p, mem_v_p,
        g_mix, g_cross, g_ffn, g_final, w_fox_in, b_fox_f, w_fox_out,
        w_mla_a, g_mla_q, g_mla_kv, w_mla_qb, w_mla_kvb, w_mla_out,
        w_x_q, w_x_o, w_ffn_gu, w_ffn_down)
    past_len = cache_fox_k.shape[2]
    pos_s = past_len + jnp.arange(x_sample.shape[1], dtype=jnp.int32)
    y_sample, fk_s, fv_s, fl_s, mc_s, mr_s = trunk(
        x_sample, pos_s, (cache_fox_k, cache_fox_v, cache_fox_logf), (cache_mla_ckv, cache_mla_krope),
        cache_mem_k, cache_mem_v,
        g_mix, g_cross, g_ffn, g_final, w_fox_in, b_fox_f, w_fox_out,
        w_mla_a, g_mla_q, g_mla_kv, w_mla_qb, w_mla_kvb, w_mla_out,
        w_x_q, w_x_o, w_ffn_gu, w_ffn_down)
    return (y_prompt, y_sample,
            fk_p, fv_p, fl_p, mc_p, mr_p, mem_k_p, mem_v_p,
            fk_s, fv_s, fl_s, mc_s, mr_s)
```

```python
import functools

import jax
import jax.numpy as jnp
from jax import lax
from jax.experimental import pallas as pl
from jax.experimental.pallas import tpu as pltpu

EPS = 1e-6
CHUNK = 64
ROPE_THETA = 10000.0
LANE = 128
HEAD_DIM = 64
PAIR = 2
MLA_ROPE = 32
NEG = -1e30
VMEM_LIMIT = 56 * 1024 * 1024
BF16 = jnp.bfloat16
F32 = jnp.float32


def _dot(a, b):
    return jnp.dot(a, b, preferred_element_type=F32)


def _dot_nt(a, b):
    return lax.dot_general(a, b, (((1,), (1,)), ((), ())), preferred_element_type=F32)


def _rms(x):
    return x * lax.rsqrt(jnp.mean(x * x, axis=-1, keepdims=True) + EPS)


def _params(*sem):
    return pltpu.CompilerParams(dimension_semantics=sem, vmem_limit_bytes=VMEM_LIMIT)


def _row_tile(m, cap=512):
    t = min(m, cap)
    assert m % t == 0
    return t


def _full(shape):
    return pl.BlockSpec(shape, lambda *_: (0,) * len(shape))


def _fox_in_body(x_ref, g_ref, wq, wk, wv, wg, wf, bf_ref, q_out, k_out, v_out, gate_out, lf_out, *, scale, heads):
    h = (_rms(x_ref[...]) * g_ref[...]).astype(BF16)
    q_out[...] = (_dot(h, wq[...]) * scale).astype(BF16)
    k_out[...] = _dot(h, wk[...])
    v_out[...] = _dot(h, wv[...])
    gate_out[...] = jax.nn.sigmoid(_dot(h, wg[...]))
    f = _dot(h, wf[...]) + bf_ref[...]
    lf = jnp.minimum(f, 0.0) - jnp.log1p(jnp.exp(-jnp.abs(f)))
    lf_out[...] = lf[:, :heads]


def _fox_in(x, g, wq, wk, wv, wg, wf, bf, heads):
    m, d = x.shape
    w = wq.shape[1]
    tm = _row_tile(m)
    row = lambda n: pl.BlockSpec((tm, n), lambda i: (i, 0))
    return pl.pallas_call(
        functools.partial(_fox_in_body, scale=HEAD_DIM ** -0.5, heads=heads),
        grid=(m // tm,),
        in_specs=[row(d), _full((1, d)), _full((d, w)), _full((d, w)), _full((d, w)), _full((d, w)),
                  _full((d, LANE)), _full((1, LANE))],
        out_specs=[row(w), row(w), row(w), row(w), row(heads)],
        out_shape=[jax.ShapeDtypeStruct((m, w), BF16), jax.ShapeDtypeStruct((m, w), F32),
                   jax.ShapeDtypeStruct((m, w), F32), jax.ShapeDtypeStruct((m, w), F32),
                   jax.ShapeDtypeStruct((m, heads), F32)],
        compiler_params=_params("parallel"),
        name="fox_in",
    )(x, g, wq, wk, wv, wg, wf, bf)


CUM_BLOCK = 256


def _cumsum_body(lf_ref, out_ref):
    t, h = lf_ref.shape
    r = lax.broadcasted_iota(jnp.int32, (CUM_BLOCK, CUM_BLOCK), 0)
    c = lax.broadcasted_iota(jnp.int32, (CUM_BLOCK, CUM_BLOCK), 1)
    tri = jnp.where(r >= c, 1.0, 0.0).astype(BF16)
    carry = jnp.zeros((1, h), F32)
    for b in range(t // CUM_BLOCK):
        x = lf_ref[b * CUM_BLOCK:(b + 1) * CUM_BLOCK, :]
        hi = x.astype(BF16)
        r1 = x - hi.astype(F32)
        mid = r1.astype(BF16)
        lo = (r1 - mid.astype(F32)).astype(BF16)
        local = _dot(tri, hi) + _dot(tri, mid) + _dot(tri, lo)
        out_ref[b * CUM_BLOCK:(b + 1) * CUM_BLOCK, :] = local + carry
        carry = carry + local[CUM_BLOCK - 1:CUM_BLOCK, :]


def _cumsum_time(lf):
    b, t, h = lf.shape
    assert t % CUM_BLOCK == 0
    spec = pl.BlockSpec((None, t, h), lambda i: (i, 0, 0))
    return pl.pallas_call(
        _cumsum_body, grid=(b,), in_specs=[spec], out_specs=spec,
        out_shape=jax.ShapeDtypeStruct((b, t, h), F32),
        compiler_params=_params("parallel"), name="logf_cumsum",
    )(lf)


def _softmax_step(s, v_j, m, l, acc):
    m_new = jnp.maximum(m, jnp.max(s, axis=-1, keepdims=True))
    alpha = jnp.exp(m - m_new)
    p = jnp.exp(s - m_new)
    l_new = alpha * l + jnp.sum(p, axis=-1, keepdims=True)
    acc_new = alpha * acc + _dot(p.astype(BF16), v_j)
    return m_new, l_new, acc_new


def _attn_prompt_body(*refs, fox, tq):
    if fox:
        q_ref, k_ref, v_ref, cq_ref, ck_ref, gate_ref, o_ref, kb, vb = refs
    else:
        qn_ref, qr_ref, kn_ref, krd_ref, v_ref, o_ref, kb = refs
    hp = pl.program_id(1)
    qi = pl.program_id(2)
    tk = tq

    @pl.when(qi == 0)
    def _():
        if fox:
            kb[...] = k_ref[...].astype(BF16)
            vb[...] = v_ref[...].astype(BF16)
        else:
            kb[:, :LANE] = kn_ref[...]
            kb[:, LANE:] = krd_ref[...]

    lane = lax.broadcasted_iota(jnp.int32, (tq, LANE), 1)
    zero = jnp.zeros((), BF16)
    qs = []
    for e in range(PAIR):
        in_head = (lane >= e * HEAD_DIM) & (lane < (e + 1) * HEAD_DIM)
        if fox:
            qs.append(jnp.where(in_head, q_ref[...], zero))
        else:
            in_rope = (lane >= e * MLA_ROPE) & (lane < (e + 1) * MLA_ROPE)
            qs.append(jnp.concatenate([jnp.where(in_head, qn_ref[...], zero),
                                       jnp.where(in_rope, qr_ref[...], zero)], axis=1))
    if fox:
        hl = lax.broadcasted_iota(jnp.int32, cq_ref.shape, 1)
        cqs = [jnp.sum(jnp.where(hl == PAIR * hp + e, cq_ref[...], 0.0), axis=1, keepdims=True)
               for e in range(PAIR)]

    row = lax.broadcasted_iota(jnp.int32, (tq, tk), 0)
    col = lax.broadcasted_iota(jnp.int32, (tq, tk), 1)
    visible = (col <= row) if fox else ((col // CHUNK) <= (row // CHUNK))

    def step(j, carry, diagonal):
        start = pl.multiple_of(j * tk, tk)
        k_j = kb[pl.ds(start, tk), :]
        v_j = vb[pl.ds(start, tk), :] if fox else v_ref[pl.ds(start, tk), :]
        out = []
        for e in range(PAIR):
            s = _dot_nt(qs[e], k_j)
            if fox:
                s = s + cqs[e] - ck_ref[e:e + 1, pl.ds(start, tk)]
            if diagonal:
                s = jnp.where(visible, s, NEG)
            out.extend(_softmax_step(s, v_j, *carry[3 * e:3 * e + 3]))
        return tuple(out)

    init = (jnp.full((tq, 1), NEG, F32), jnp.zeros((tq, 1), F32), jnp.zeros((tq, LANE), F32)) * PAIR
    carry = lax.fori_loop(0, qi, lambda j, c: step(j, c, False), init)
    carry = step(qi, carry, True)
    o = jnp.where(lane < HEAD_DIM, carry[2] / carry[1], carry[5] / carry[4])
    if fox:
        o = o * gate_ref[...]
    o_ref[...] = o.astype(BF16)


def _fox_attn_prompt(q, k, v, cum, gate, tq=512):
    b, t, w = q.shape
    heads = cum.shape[-1]
    tq = min(tq, t)
    npair = w // LANE
    ck = jnp.swapaxes(cum, 1, 2).reshape(b, npair, PAIR, t)
    qblk = pl.BlockSpec((None, tq, LANE), lambda bi, hp, qi: (bi, qi, hp))
    kblk = pl.BlockSpec((None, t, LANE), lambda bi, hp, qi: (bi, 0, hp))
    return pl.pallas_call(
        functools.partial(_attn_prompt_body, fox=True, tq=tq),
        grid=(b, npair, t // tq),
        in_specs=[qblk, kblk, kblk,
                  pl.BlockSpec((None, tq, heads), lambda bi, hp, qi: (bi, qi, 0)),
                  pl.BlockSpec((None, None, PAIR, t), lambda bi, hp, qi: (bi, hp, 0, 0)),
                  qblk],
        out_specs=qblk,
        out_shape=jax.ShapeDtypeStruct((b, t, w), BF16),
        scratch_shapes=[pltpu.VMEM((t, LANE), BF16), pltpu.VMEM((t, LANE), BF16)],
        compiler_params=_params("parallel", "parallel", "arbitrary"),
        name="fox_attn_prompt",
    )(q, k, v, cum, ck, gate)


def _mla_attn_prompt(qn, qr, kn, krd, v, tq=512):
    b, t, w = qn.shape
    tq = min(tq, t)
    npair = w // LANE
    qblk = pl.BlockSpec((None, tq, LANE), lambda bi, hp, qi: (bi, qi, hp))
    kblk = pl.BlockSpec((None, t, LANE), lambda bi, hp, qi: (bi, 0, hp))
    return pl.pallas_call(
        functools.partial(_attn_prompt_body, fox=False, tq=tq),
        grid=(b, npair, t // tq),
        in_specs=[qblk, qblk, kblk, pl.BlockSpec((None, t, LANE), lambda bi, hp, qi: (bi, 0, 0)), kblk],
        out_specs=qblk,
        out_shape=jax.ShapeDtypeStruct((b, t, w), BF16),
        scratch_shapes=[pltpu.VMEM((t, 2 * LANE), BF16)],
        compiler_params=_params("parallel", "parallel", "arbitrary"),
        name="mla_attn_prompt",
    )(qn, qr, kn, krd, v)


def _two_part_softmax(s_p, s_n, v_p, v_n):
    m = jnp.maximum(jnp.max(s_p, axis=-1, keepdims=True), jnp.max(s_n, axis=-1, keepdims=True))
    p_p = jnp.exp(s_p - m)
    p_n = jnp.exp(s_n - m)
    l = jnp.sum(p_p, axis=-1, keepdims=True) + jnp.sum(p_n, axis=-1, keepdims=True)
    return (_dot(p_p.astype(BF16), v_p) + _dot(p_n.astype(BF16), v_n)) / l


def _fox_attn_sample_body(q_ref, kp_ref, vp_ref, kn_ref, vn_ref, cq_ref, ckp_ref, ckn_ref, gate_ref, o_ref):
    hp = pl.program_id(1)
    t = q_ref.shape[0]
    lane = lax.broadcasted_iota(jnp.int32, (t, LANE), 1)
    hl = lax.broadcasted_iota(jnp.int32, cq_ref.shape, 1)
    row = lax.broadcasted_iota(jnp.int32, (t, t), 0)
    col = lax.broadcasted_iota(jnp.int32, (t, t), 1)
    k_p = kp_ref[...].astype(BF16)
    v_p = vp_ref[...].astype(BF16)
    k_n = kn_ref[...].astype(BF16)
    v_n = vn_ref[...].astype(BF16)
    outs = []
    for e in range(PAIR):
        in_head = (lane >= e * HEAD_DIM) & (lane < (e + 1) * HEAD_DIM)
        q_e = jnp.where(in_head, q_ref[...], jnp.zeros((), BF16))
        cq = jnp.sum(jnp.where(hl == PAIR * hp + e, cq_ref[...], 0.0), axis=1, keepdims=True)
        s_p = _dot_nt(q_e, k_p) + cq - ckp_ref[e:e + 1, :]
        s_n = _dot_nt(q_e, k_n) + cq - ckn_ref[e:e + 1, :]
        s_n = jnp.where(col <= row, s_n, NEG)
        outs.append(_two_part_softmax(s_p, s_n, v_p, v_n))
    o = jnp.where(lane < HEAD_DIM, outs[0], outs[1]) * gate_ref[...]
    o_ref[...] = o.astype(BF16)


def _fox_attn_sample(q, k_past, v_past, k_new, v_new, cum, gate):
    b, t, w = q.shape
    p = k_past.shape[1]
    heads = cum.shape[-1]
    npair = w // LANE
    ck = jnp.swapaxes(cum, 1, 2).reshape(b, npair, PAIR, p + t)
    new = pl.BlockSpec((None, t, LANE), lambda bi, hp: (bi, 0, hp))
    past = pl.BlockSpec((None, p, LANE), lambda bi, hp: (bi, 0, hp))
    return pl.pallas_call(
        _fox_attn_sample_body,
        grid=(b, npair),
        in_specs=[new, past, past, new, new,
                  pl.BlockSpec((None, t, heads), lambda bi, hp: (bi, 0, 0)),
                  pl.BlockSpec((None, None, PAIR, p), lambda bi, hp: (bi, hp, 0, 0)),
                  pl.BlockSpec((None, None, PAIR, t), lambda bi, hp: (bi, hp, 0, 0)),
                  new],
        out_specs=new,
        out_shape=jax.ShapeDtypeStruct((b, t, w), BF16),
        compiler_params=_params("parallel", "parallel"),
        name="fox_attn_sample",
    )(q, k_past, v_past, k_new, v_new, cum[:, p:], ck[..., :p], ck[..., p:], gate)


def _mla_attn_sample_body(qn_ref, qr_ref, cp_ref, krp_ref, cn_ref, krdn_ref, wkn_ref, wv_ref, o_ref, *, past_len):
    t = qn_ref.shape[0]
    p = cp_ref.shape[0]
    lane = lax.broadcasted_iota(jnp.int32, (t, LANE), 1)
    zero = jnp.zeros((), BF16)
    c_p = cp_ref[...].astype(BF16)
    c_n = cn_ref[...].astype(BF16)
    kn_p = _dot(c_p, wkn_ref[...]).astype(BF16)
    v_p = _dot(c_p, wv_ref[...]).astype(BF16)
    kn_n = _dot(c_n, wkn_ref[...]).astype(BF16)
    v_n = _dot(c_n, wv_ref[...]).astype(BF16)
    sr = lax.broadcasted_iota(jnp.int32, (MLA_ROPE, LANE), 0)
    sc = lax.broadcasted_iota(jnp.int32, (MLA_ROPE, LANE), 1)
    dup = jnp.where((sc == sr) | (sc == sr + MLA_ROPE), 1.0, 0.0).astype(BF16)
    krd_p = _dot(krp_ref[...].astype(BF16), dup).astype(BF16)
    krd_n = krdn_ref[...]
    q_chunk = (past_len + lax.broadcasted_iota(jnp.int32, (t, p), 0)) // CHUNK
    vis_p = (lax.broadcasted_iota(jnp.int32, (t, p), 1) // CHUNK) <= q_chunk
    q_chunk_n = (past_len + lax.broadcasted_iota(jnp.int32, (t, t), 0)) // CHUNK
    vis_n = ((past_len + lax.broadcasted_iota(jnp.int32, (t, t), 1)) // CHUNK) <= q_chunk_n
    outs = []
    for e in range(PAIR):
        in_head = (lane >= e * HEAD_DIM) & (lane < (e + 1) * HEAD_DIM)
        in_rope = (lane >= e * MLA_ROPE) & (lane < (e + 1) * MLA_ROPE)
        qn_e = jnp.where(in_head, qn_ref[...], zero)
        qr_e = jnp.where(in_rope, qr_ref[...], zero)
        s_p = jnp.where(vis_p, _dot_nt(qn_e, kn_p) + _dot_nt(qr_e, krd_p), NEG)
        s_n = jnp.where(vis_n, _dot_nt(qn_e, kn_n) + _dot_nt(qr_e, krd_n), NEG)
        outs.append(_two_part_softmax(s_p, s_n, v_p, v_n))
    o_ref[...] = jnp.where(lane < HEAD_DIM, outs[0], outs[1]).astype(BF16)


def _mla_attn_sample(qn, qr, ckv_past, kr_past, ckv_new, krd_new, wkn, wv):
    b, t, w = qn.shape
    p, c = ckv_past.shape[1:]
    npair = w // LANE
    new = pl.BlockSpec((None, t, LANE), lambda bi, hp: (bi, 0, hp))
    wcol = pl.BlockSpec((c, LANE), lambda bi, hp: (0, hp))
    return pl.pallas_call(
        functools.partial(_mla_attn_sample_body, past_len=p),
        grid=(b, npair),
        in_specs=[new, new,
                  pl.BlockSpec((None, p, c), lambda bi, hp: (bi, 0, 0)),
                  pl.BlockSpec((None, p, MLA_ROPE), lambda bi, hp: (bi, 0, 0)),
                  pl.BlockSpec((None, t, c), lambda bi, hp: (bi, 0, 0)),
                  pl.BlockSpec((None, t, LANE), lambda bi, hp: (bi, 0, 0)),
                  wcol, wcol],
        out_specs=new,
        out_shape=jax.ShapeDtypeStruct((b, t, w), BF16),
        compiler_params=_params("parallel", "arbitrary"),
        name="mla_attn_sample",
    )(qn, qr, ckv_past, kr_past, ckv_new, krd_new, wkn, wv)


def _mla_in_body(x_ref, g_ref, w1, gq_ref, gkv_ref, w2, wkn, wv, cos_ref, sin_ref,
                 qn_out, qr_out, ckv_out, kn_out, v_out, kr_out, krd_out, *, q_lora, kv_lora, scale):
    h = (_rms(x_ref[...]) * g_ref[...]).astype(BF16)
    a = _dot(h, w1[...])
    cos = cos_ref[...]
    sin = sin_ref[...]
    c_q = (_rms(a[:, :q_lora]) * gq_ref[...]).astype(BF16)
    c_kv = _rms(a[:, q_lora:q_lora + kv_lora]) * gkv_ref[...]
    ckv_out[...] = c_kv
    o = q_lora + kv_lora
    krd = a[:, o:o + LANE] * cos + a[:, o + LANE:o + 2 * LANE] * sin
    kr_out[...] = krd[:, :MLA_ROPE]
    krd_out[...] = krd.astype(BF16)
    c_kv = c_kv.astype(BF16)
    kn_out[...] = _dot(c_kv, wkn[...]).astype(BF16)
    v_out[...] = _dot(c_kv, wv[...]).astype(BF16)
    w = qn_out.shape[1]
    qn_out[...] = (_dot(c_q, w2[:, :w]) * scale).astype(BF16)
    for p in range(w // LANE):
        qa = _dot(c_q, w2[:, w + p * LANE:w + (p + 1) * LANE])
        qb = _dot(c_q, w2[:, 2 * w + p * LANE:2 * w + (p + 1) * LANE])
        qr_out[:, p * LANE:(p + 1) * LANE] = ((qa * cos + qb * sin) * scale).astype(BF16)


def _mla_in(x, g, w1, gq, gkv, w2, wkn, wv, cos, sin, scale):
    m, d = x.shape
    q_lora = gq.shape[1]
    kv_lora = gkv.shape[1]
    w = wkn.shape[1]
    tm = _row_tile(m)
    if cos.shape[0] < tm:
        cos = jnp.tile(cos, (tm // cos.shape[0], 1))
        sin = jnp.tile(sin, (tm // sin.shape[0], 1))
    nt = cos.shape[0] // tm
    row = lambda n: pl.BlockSpec((tm, n), lambda i: (i, 0))
    tab = pl.BlockSpec((tm, LANE), lambda i: (i % nt, 0))
    return pl.pallas_call(
        functools.partial(_mla_in_body, q_lora=q_lora, kv_lora=kv_lora, scale=scale),
        grid=(m // tm,),
        in_specs=[row(d), _full((1, d)), _full(w1.shape), _full((1, q_lora)), _full((1, kv_lora)),
                  _full(w2.shape), _full(wkn.shape), _full(wv.shape), tab, tab],
        out_specs=[row(w), row(w), row(kv_lora), row(w), row(w), row(MLA_ROPE), row(LANE)],
        out_shape=[jax.ShapeDtypeStruct((m, w), BF16), jax.ShapeDtypeStruct((m, w), BF16),
                   jax.ShapeDtypeStruct((m, kv_lora), F32), jax.ShapeDtypeStruct((m, w), BF16),
                   jax.ShapeDtypeStruct((m, w), BF16), jax.ShapeDtypeStruct((m, MLA_ROPE), F32),
                   jax.ShapeDtypeStruct((m, LANE), BF16)],
        compiler_params=_params("parallel"),
        name="mla_in",
    )(x, g, w1, gq, gkv, w2, wkn, wv, cos, sin)


def _proj_res_body(a_ref, w_ref, x_ref, o_ref):
    o_ref[...] = x_ref[...] + _dot(a_ref[...], w_ref[...])


def _proj_residual(a, w, x):
    m, k = a.shape
    n = w.shape[1]
    tm = _row_tile(m)
    return pl.pallas_call(
        _proj_res_body, grid=(m // tm,),
        in_specs=[pl.BlockSpec((tm, k), lambda i: (i, 0)), _full((k, n)), pl.BlockSpec((tm, n), lambda i: (i, 0))],
        out_specs=pl.BlockSpec((tm, n), lambda i: (i, 0)),
        out_shape=jax.ShapeDtypeStruct((m, n), F32),
        compiler_params=_params("parallel"), name="proj_residual",
    )(a, w, x)


def _memory_kv_body(m_ref, g_ref, w_ref, k_out, v_out):
    h = (_rms(m_ref[...]) * g_ref[...]).astype(BF16)
    n = k_out.shape[-1]
    k_out[...] = _dot(h, w_ref[:, :n])
    v_out[...] = _dot(h, w_ref[:, n:])


def _memory_kv(mem, g_mem, w_kv):
    m, d = mem.shape
    depth, _, n2 = w_kv.shape
    n = n2 // 2
    tm = _row_tile(m)
    out = pl.BlockSpec((None, tm, n), lambda l, i: (l, i, 0))
    return pl.pallas_call(
        _memory_kv_body, grid=(depth, m // tm),
        in_specs=[pl.BlockSpec((tm, d), lambda l, i: (i, 0)),
                  pl.BlockSpec((None, 1, d), lambda l, i: (l, 0, 0)),
                  pl.BlockSpec((None, d, n2), lambda l, i: (l, 0, 0))],
        out_specs=[out, out],
        out_shape=[jax.ShapeDtypeStruct((depth, m, n), F32)] * 2,
        compiler_params=_params("parallel", "parallel"), name="memory_kv",
    )(mem, g_mem, w_kv)


def _cross_body(x_ref, g_ref, wq, mk_ref, mv_ref, wo, o_ref, att, *, heads):
    bb, tm, d = x_ref.shape
    x = x_ref[...].reshape(bb * tm, d)
    h = (_rms(x) * g_ref[...]).astype(BF16)
    dh = wq.shape[1] // heads
    q = (_dot(h, wq[...]) * dh ** -0.5).astype(BF16)
    for b in range(bb):
        for hd in range(heads):
            cols = slice(hd * dh, (hd + 1) * dh)
            s = _dot_nt(q[b * tm:(b + 1) * tm, cols], mk_ref[b, :, cols].astype(BF16))
            p = jnp.exp(s - jnp.max(s, axis=-1, keepdims=True))
            o = _dot(p.astype(BF16), mv_ref[b, :, cols].astype(BF16)) / jnp.sum(p, axis=-1, keepdims=True)
            att[b * tm:(b + 1) * tm, cols] = o.astype(BF16)
    o_ref[...] = (x + _dot(att[...], wo[...])).reshape(bb, tm, d)


def _cross_block(x, g, wq, mk, mv, wo, heads):
    b, t, d = x.shape
    n, xw = mk.shape[1:]
    tm = min(t, 512)
    bb = max(1, min(b, 64 // tm))
    xblk = pl.BlockSpec((bb, tm, d), lambda bi, ti: (bi, ti, 0))
    mblk = pl.BlockSpec((bb, n, xw), lambda bi, ti: (bi, 0, 0))
    return pl.pallas_call(
        functools.partial(_cross_body, heads=heads),
        grid=(b // bb, t // tm),
        in_specs=[xblk, _full((1, d)), _full((d, xw)), mblk, mblk, _full((xw, d))],
        out_specs=xblk,
        out_shape=jax.ShapeDtypeStruct((b, t, d), F32),
        scratch_shapes=[pltpu.VMEM((bb * tm, xw), BF16)],
        compiler_params=_params("parallel", "parallel"), name="cross_block",
    )(x, g, wq, mk, mv, wo)


FF_BLOCK = 256


def _ffn_body(x_ref, g_ref, wg, wu, wd, *rest, final):
    if final:
        gf_ref, o_ref, y_ref = rest
    else:
        (o_ref,) = rest
    x = x_ref[...]
    h = (_rms(x) * g_ref[...]).astype(BF16)
    acc = x
    for c in range(wg.shape[1] // FF_BLOCK):
        cols = slice(c * FF_BLOCK, (c + 1) * FF_BLOCK)
        gate = _dot(h, wg[:, cols])
        up = _dot(h, wu[:, cols])
        act = (gate * jax.nn.sigmoid(gate) * up).astype(BF16)
        acc = acc + _dot(act, wd[cols, :])
    o_ref[...] = acc
    if final:
        y_ref[...] = _rms(acc) * gf_ref[...]


def _ffn_block(x, g, wg, wu, wd, g_final=None):
    m, d = x.shape
    ff = wg.shape[1]
    assert ff % FF_BLOCK == 0
    tm = _row_tile(m)
    row = pl.BlockSpec((tm, d), lambda i: (i, 0))
    final = g_final is not None
    ins = [row, _full((1, d)), _full((d, ff)), _full((d, ff)), _full((ff, d))]
    args = [x, g, wg, wu, wd]
    if final:
        ins.append(_full((1, d)))
        args.append(g_final)
    out = jax.ShapeDtypeStruct((m, d), F32)
    return pl.pallas_call(
        functools.partial(_ffn_body, final=final), grid=(m // tm,),
        in_specs=ins,
        out_specs=[row, row] if final else row,
        out_shape=[out, out] if final else out,
        compiler_params=_params("parallel"), name="ffn_block",
    )(*args)


def _prep_fox(w_in, b_f, w_out, heads):
    width = w_out.shape[0]
    d = w_in.shape[0]
    wq, wk, wv, wf, wg = jnp.split(w_in.astype(BF16), [width, 2 * width, 3 * width, 3 * width + heads], axis=1)
    wf = jnp.pad(wf, ((0, 0), (0, LANE - heads)))
    bf = jnp.pad(b_f, (0, LANE - heads)).reshape(1, LANE)
    return wq, wk, wv, wg, wf, bf, w_out.astype(BF16)


def _prep_mla(w_a, w_qb, w_kvb, w_out, q_lora, kv_lora, heads):
    half = MLA_ROPE // 2
    w_a = w_a.astype(BF16)
    d = w_a.shape[0]
    x1 = w_a[:, q_lora + kv_lora:q_lora + kv_lora + half]
    x2 = w_a[:, q_lora + kv_lora + half:]
    zeros = jnp.zeros((d, LANE - 2 * MLA_ROPE), BF16)
    w1 = jnp.concatenate([w_a[:, :q_lora + kv_lora], x1, x2, x1, x2, zeros, x2, x1, x2, x1, zeros], axis=1)
    qb = w_qb.astype(BF16).reshape(q_lora, heads, HEAD_DIM + MLA_ROPE)
    wqn = qb[:, :, :HEAD_DIM].reshape(q_lora, heads * HEAD_DIM)
    rope = qb[:, :, HEAD_DIM:]
    swapped = jnp.concatenate([rope[..., half:], rope[..., :half]], axis=-1)

    def pack(r):
        r = r.reshape(q_lora, heads // PAIR, PAIR * MLA_ROPE)
        return jnp.pad(r, ((0, 0), (0, 0), (0, LANE - PAIR * MLA_ROPE))).reshape(q_lora, -1)

    w2 = jnp.concatenate([wqn, pack(rope), pack(swapped)], axis=1)
    kvb = w_kvb.astype(BF16).reshape(kv_lora, heads, 2 * HEAD_DIM)
    wkn = kvb[:, :, :HEAD_DIM].reshape(kv_lora, heads * HEAD_DIM)
    wv = kvb[:, :, HEAD_DIM:].reshape(kv_lora, heads * HEAD_DIM)
    return w1, w2, wkn, wv, w_out.astype(BF16)


def _rope_tables(pos):
    half = MLA_ROPE // 2
    inv = ROPE_THETA ** (-jnp.arange(half, dtype=F32) / half)
    ang = pos.astype(F32)[:, None] * inv[None, :]
    cos, sin = jnp.cos(ang), jnp.sin(ang)
    z = jnp.zeros((pos.shape[0], LANE - 2 * MLA_ROPE), F32)
    return (jnp.concatenate([cos, cos, cos, cos, z], axis=1),
            jnp.concatenate([-sin, sin, -sin, sin, z], axis=1))


def _trunk(x, pos, fox_past, mla_past, mem_k, mem_v, wts):
    b, t, d = x.shape
    m = b * t
    depth = len(wts["ffn"])
    cos, sin = _rope_tables(pos)
    fox_k, fox_v, fox_lf, mla_c, mla_r = [], [], [], [], []
    y = None
    for i in range(depth):
        j = i // 2
        x2 = x.reshape(m, d)
        if i % 2 == 0:
            wq, wk, wv, wg, wf, bf, w_out = wts["fox"][j]
            heads = wts["fox_heads"]
            q, k, v, gate, lf = _fox_in(x2, wts["g_mix"][i], wq, wk, wv, wg, wf, bf, heads)
            width = q.shape[1]
            r3 = lambda a: a.reshape(b, t, a.shape[-1])
            if fox_past is None:
                cum = _cumsum_time(r3(lf))
                o = _fox_attn_prompt(r3(q), r3(k), r3(v), cum, r3(gate))
            else:
                k_past, v_past, lf_past = fox_past[0][j], fox_past[1][j], fox_past[2][j]
                p = k_past.shape[1]
                total = p + t
                padded = -(-total // CUM_BLOCK) * CUM_BLOCK
                lf_all = jnp.concatenate([lf_past, r3(lf), jnp.zeros((b, padded - total, heads), F32)], axis=1)
                cum = _cumsum_time(lf_all)[:, :total]
                o = _fox_attn_sample(r3(q), k_past.reshape(b, p, width), v_past.reshape(b, p, width),
                                     r3(k), r3(v), cum, r3(gate))
            hd = width // heads
            fox_k.append(k.reshape(b, t, heads, hd))
            fox_v.append(v.reshape(b, t, heads, hd))
            fox_lf.append(lf.reshape(b, t, heads))
        else:
            w1, w2, wkn, wv, w_out = wts["mla"][j]
            gq, gkv = wts["g_mla_q"][j], wts["g_mla_kv"][j]
            qn, qr, ckv, kn, v, kr, krd = _mla_in(x2, wts["g_mix"][i], w1, gq, gkv, w2, wkn, wv, cos, sin,
                                                  wts["mla_scale"])
            r3 = lambda a: a.reshape(b, t, a.shape[-1])
            if mla_past is None:
                o = _mla_attn_prompt(r3(qn), r3(qr), r3(kn), r3(krd), r3(v))
            else:
                o = _mla_attn_sample(r3(qn), r3(qr), mla_past[0][j], mla_past[1][j], r3(ckv), r3(krd), wkn, wv)
            mla_c.append(r3(ckv))
            mla_r.append(r3(kr))
        x2 = _proj_residual(o.reshape(m, o.shape[-1]), w_out, x2)
        wxq, wxo = wts["cross"][i]
        x3 = _cross_block(x2.reshape(b, t, d), wts["g_cross"][i], wxq, mem_k[i], mem_v[i], wxo, wts["x_heads"])
        wgt, wup, wdn = wts["ffn"][i]
        if i == depth - 1:
            x2, y = _ffn_block(x3.reshape(m, d), wts["g_ffn"][i], wgt, wup, wdn, wts["g_final"])
        else:
            x2 = _ffn_block(x3.reshape(m, d), wts["g_ffn"][i], wgt, wup, wdn)
        x = x2.reshape(b, t, d)
    return (y.reshape(b, t, d), jnp.stack(fox_k), jnp.stack(fox_v), jnp.stack(fox_lf),
            jnp.stack(mla_c), jnp.stack(mla_r))


def kernel(x_prompt, x_sample, mem_prompt, cache_fox_k, cache_fox_v, cache_fox_logf, cache_mla_ckv, cache_mla_krope, cache_mem_k, cache_mem_v, g_mix, g_cross, g_mem, g_ffn, g_final, w_fox_in, b_fox_f, w_fox_out, w_mla_a, g_mla_q, g_mla_kv, w_mla_qb, w_mla_kvb, w_mla_out, w_x_q, w_x_kv, w_x_o, w_ffn_gu, w_ffn_down):
    depth, d = g_mix.shape
    fox_heads = b_fox_f.shape[1]
    x_heads = cache_mem_k.shape[3]
    q_lora = g_mla_q.shape[1]
    kv_lora = g_mla_kv.shape[1]
    mla_heads = w_mla_out.shape[1] // HEAD_DIM
    ff = w_ffn_down.shape[1]
    row = lambda g: [g[i].reshape(1, -1) for i in range(g.shape[0])]
    gu = w_ffn_gu.astype(BF16)
    wts = {
        "g_mix": row(g_mix), "g_cross": row(g_cross), "g_ffn": row(g_ffn), "g_final": g_final.reshape(1, d),
        "g_mla_q": row(g_mla_q), "g_mla_kv": row(g_mla_kv),
        "fox": [_prep_fox(w_fox_in[j], b_fox_f[j], w_fox_out[j], fox_heads) for j in range(w_fox_in.shape[0])],
        "mla": [_prep_mla(w_mla_a[j], w_mla_qb[j], w_mla_kvb[j], w_mla_out[j], q_lora, kv_lora, mla_heads)
                for j in range(w_mla_a.shape[0])],
        "cross": [(w_x_q[i].astype(BF16), w_x_o[i].astype(BF16)) for i in range(depth)],
        "ffn": [(gu[i, :, :ff], gu[i, :, ff:], w_ffn_down[i].astype(BF16)) for i in range(depth)],
        "fox_heads": fox_heads, "x_heads": x_heads,
        "mla_scale": (HEAD_DIM + MLA_ROPE) ** -0.5,
    }
    bp, n_mem, _ = mem_prompt.shape
    mk, mv = _memory_kv(mem_prompt.reshape(bp * n_mem, d), g_mem.reshape(depth, 1, d), w_x_kv.astype(BF16))
    xw = mk.shape[-1]
    mk = mk.reshape(depth, bp, n_mem, xw)
    mv = mv.reshape(depth, bp, n_mem, xw)
    pos_p = jnp.arange(x_prompt.shape[1], dtype=jnp.int32)
    y_p, fk_p, fv_p, fl_p, mc_p, mr_p = _trunk(x_prompt, pos_p, None, None, mk, mv, wts)
    past_len = cache_fox_k.shape[2]
    pos_s = past_len + jnp.arange(x_sample.shape[1], dtype=jnp.int32)
    bs = x_sample.shape[0]
    y_s, fk_s, fv_s, fl_s, mc_s, mr_s = _trunk(
        x_sample, pos_s, (cache_fox_k, cache_fox_v, cache_fox_logf), (cache_mla_ckv, cache_mla_krope),
        cache_mem_k.reshape(depth, bs, n_mem, xw), cache_mem_v.reshape(depth, bs, n_mem, xw), wts)
    dh = xw // x_heads
    return (y_p, y_s, fk_p, fv_p, fl_p, mc_p, mr_p,
            mk.reshape(depth, bp, n_mem, x_heads, dh), mv.reshape(depth, bp, n_mem, x_heads, dh),
            fk_s, fv_s, fl_s, mc_s, mr_s)
```

```python
import functools

import jax
import jax.numpy as jnp
from jax import lax
from jax.experimental import pallas as pl
from jax.experimental.pallas import tpu as pltpu

EPS = 1e-6
CHUNK = 64
ROPE_THETA = 10000.0
LANE = 128
HEAD_DIM = 64
PAIR = 2
MLA_ROPE = 32
NEG = -1e30
LOG2E = 1.4426950408889634
VMEM_LIMIT = 56 * 1024 * 1024
BF16 = jnp.bfloat16
F32 = jnp.float32


def _dot(a, b):
    return jnp.dot(a, b, preferred_element_type=F32)


def _dot_nt(a, b):
    return lax.dot_general(a, b, (((1,), (1,)), ((), ())), preferred_element_type=F32)


def _rms(x):
    return x * lax.rsqrt(jnp.mean(x * x, axis=-1, keepdims=True) + EPS)


def _params(*sem):
    return pltpu.CompilerParams(dimension_semantics=sem, vmem_limit_bytes=VMEM_LIMIT)


def _row_tile(m, cap=512):
    t = min(m, cap)
    assert m % t == 0
    return t


def _full(shape):
    return pl.BlockSpec(shape, lambda *_: (0,) * len(shape), pipeline_mode=pl.Buffered(1))


def _log_sigmoid(f):
    return jnp.minimum(f, 0.0) - jnp.log1p(jnp.exp(-jnp.abs(f)))


def _split3(x):
    hi = x.astype(BF16)
    r = x - hi.astype(F32)
    mid = r.astype(BF16)
    lo = (r - mid.astype(F32)).astype(BF16)
    return hi, mid, lo


def _layer_map(imap, layer, *idx):
    return (layer,) + tuple(imap(*idx))


def _stacked_call(body, *, grid, in_specs, args, outs, layer, prev, sem, name, scratch=()):
    out_specs, out_shape, stacked = [], [], []
    for k, (shape, dtype, blk, imap, n_layers) in enumerate(outs):
        if n_layers is None:
            out_specs.append(pl.BlockSpec(blk, imap))
            out_shape.append(jax.ShapeDtypeStruct(shape, dtype))
        else:
            out_specs.append(pl.BlockSpec((None,) + tuple(blk), functools.partial(_layer_map, imap, layer)))
            out_shape.append(jax.ShapeDtypeStruct((n_layers,) + tuple(shape), dtype))
            stacked.append(k)
    aliases = {}
    in_specs = list(in_specs)
    args = list(args)
    if prev is not None:
        for k, arr in zip(stacked, prev):
            aliases[len(args)] = k
            in_specs.append(pl.BlockSpec(memory_space=pl.ANY))
            args.append(arr)
    n_alias = len(aliases)

    def wrapped(*refs):
        n_in = len(args) - n_alias
        body(*refs[:n_in], *refs[n_in + n_alias:])

    res = pl.pallas_call(
        wrapped, grid=grid, in_specs=in_specs, out_specs=out_specs, out_shape=out_shape,
        input_output_aliases=aliases, scratch_shapes=list(scratch),
        compiler_params=_params(*sem), name=name,
    )(*args)
    return res, [res[k] for k in stacked]


def _fox_in_body(x_ref, g_ref, wq, wk, wv, wg, wf, bf_ref, q_out, k_out, v_out, gate_out, lf_out, *, scale, heads):
    h = (_rms(x_ref[...]) * g_ref[...]).astype(BF16)
    q_out[...] = (_dot(h, wq[...]) * scale).astype(BF16)
    k_out[...] = _dot(h, wk[...])
    v_out[...] = _dot(h, wv[...])
    gate_out[...] = jax.nn.sigmoid(_dot(h, wg[...]))
    lf_out[...] = _log_sigmoid(_dot(h, wf[...]) + bf_ref[...])[:, :heads]


def _fox_in(x, g, wq, wk, wv, wg, wf, bf, heads):
    m, d = x.shape
    w = wq.shape[1]
    tm = _row_tile(m)
    row = lambda n: pl.BlockSpec((tm, n), lambda i: (i, 0))
    return pl.pallas_call(
        functools.partial(_fox_in_body, scale=HEAD_DIM ** -0.5, heads=heads),
        grid=(m // tm,),
        in_specs=[row(d), _full((1, d)), _full((d, w)), _full((d, w)), _full((d, w)), _full((d, w)),
                  _full((d, LANE)), _full((1, LANE))],
        out_specs=[row(w), row(w), row(w), row(w), row(heads)],
        out_shape=[jax.ShapeDtypeStruct((m, w), BF16), jax.ShapeDtypeStruct((m, w), F32),
                   jax.ShapeDtypeStruct((m, w), F32), jax.ShapeDtypeStruct((m, w), F32),
                   jax.ShapeDtypeStruct((m, heads), F32)],
        compiler_params=_params("parallel"),
        name="fox_in",
    )(x, g, wq, wk, wv, wg, wf, bf)


def _fox_in_t_body(x_ref, g_ref, wqt, wkt, wvt, wk, wg, wft, bf_ref,
                   qt_out, k_out, kt_out, vt_out, gate_out, lft_out, *, scale):
    h = (_rms(x_ref[...]) * g_ref[...]).astype(BF16)
    qt_out[...] = (_dot_nt(wqt[...], h) * scale).astype(BF16)
    k_out[...] = _dot(h, wk[...]).astype(BF16)
    kt_out[...] = _dot_nt(wkt[...], h)
    vt_out[...] = _dot_nt(wvt[...], h)
    gate_out[...] = jax.nn.sigmoid(_dot(h, wg[...]))
    lft_out[...] = _log_sigmoid(_dot_nt(wft[...], h) + bf_ref[...])


def _fox_in_t(x, g, wts, layer, n_layers, prev):
    wqt, wkt, wvt, wk, wg, wft, bf = wts
    b, t, d = x.shape
    w = wk.shape[1]
    heads = wft.shape[0]
    tm = _row_tile(t)
    rows = lambda bi, i: (bi, i, 0)
    cols = lambda bi, i: (bi, 0, i)
    res, stacked = _stacked_call(
        functools.partial(_fox_in_t_body, scale=HEAD_DIM ** -0.5 * LOG2E),
        grid=(b, t // tm),
        in_specs=[pl.BlockSpec((None, tm, d), rows), _full((1, d)), _full((w, d)), _full((w, d)), _full((w, d)),
                  _full((d, w)), _full((d, w)), _full((heads, d)), _full((heads, 1))],
        args=[x, g, wqt, wkt, wvt, wk, wg, wft, bf],
        outs=[((b, w, t), BF16, (None, w, tm), cols, None),
              ((b, t, w), BF16, (None, tm, w), rows, None),
              ((b, w, t), F32, (None, w, tm), cols, n_layers),
              ((b, w, t), F32, (None, w, tm), cols, n_layers),
              ((b, t, w), F32, (None, tm, w), rows, None),
              ((b, heads, t), F32, (None, heads, tm), cols, n_layers)],
        layer=layer, prev=prev, sem=("parallel", "parallel"), name="fox_in_t")
    qt, k, _, _, gate, _ = res
    return qt, k, gate, stacked


CUM_BLOCK = 256


def _cumsum_body(lf_ref, out_ref):
    t, h = lf_ref.shape
    r = lax.broadcasted_iota(jnp.int32, (CUM_BLOCK, CUM_BLOCK), 0)
    c = lax.broadcasted_iota(jnp.int32, (CUM_BLOCK, CUM_BLOCK), 1)
    tri = jnp.where(r >= c, 1.0, 0.0).astype(BF16)
    carry = jnp.zeros((1, h), F32)
    for b in range(t // CUM_BLOCK):
        hi, mid, lo = _split3(lf_ref[b * CUM_BLOCK:(b + 1) * CUM_BLOCK, :])
        local = _dot(tri, hi) + _dot(tri, mid) + _dot(tri, lo)
        out_ref[b * CUM_BLOCK:(b + 1) * CUM_BLOCK, :] = local + carry
        carry = carry + local[CUM_BLOCK - 1:CUM_BLOCK, :]


def _cumsum_time(lf):
    b, t, h = lf.shape
    assert t % CUM_BLOCK == 0
    spec = pl.BlockSpec((None, t, h), lambda i: (i, 0, 0))
    return pl.pallas_call(
        _cumsum_body, grid=(b,), in_specs=[spec], out_specs=spec,
        out_shape=jax.ShapeDtypeStruct((b, t, h), F32),
        compiler_params=_params("parallel"), name="logf_cumsum",
    )(lf)


def _cumsum_lanes_body(lf_ref, out_ref):
    h, t = lf_ref.shape
    r = lax.broadcasted_iota(jnp.int32, (CUM_BLOCK, CUM_BLOCK), 0)
    c = lax.broadcasted_iota(jnp.int32, (CUM_BLOCK, CUM_BLOCK), 1)
    tri = jnp.where(r <= c, 1.0, 0.0).astype(BF16)
    carry = jnp.zeros((h, 1), F32)
    for b in range(t // CUM_BLOCK):
        hi, mid, lo = _split3(lf_ref[:, b * CUM_BLOCK:(b + 1) * CUM_BLOCK])
        out = _dot(hi, tri) + _dot(mid, tri) + _dot(lo, tri) + carry
        out_ref[:, b * CUM_BLOCK:(b + 1) * CUM_BLOCK] = out
        carry = out[:, CUM_BLOCK - 1:CUM_BLOCK]


def _cumsum_lanes(lf_all, layer):
    _, b, h, t = lf_all.shape
    assert t % CUM_BLOCK == 0
    return pl.pallas_call(
        _cumsum_lanes_body, grid=(b,),
        in_specs=[pl.BlockSpec((None, None, h, t), lambda i: (layer, i, 0, 0))],
        out_specs=pl.BlockSpec((None, h, t), lambda i: (i, 0, 0)),
        out_shape=jax.ShapeDtypeStruct((b, h, t), F32),
        compiler_params=_params("parallel"), name="logf_cumsum_t",
    )(lf_all)


N_PIECE = 3


def _attn_prompt_body(*refs, fox, tq):
    if fox:
        qt_ref, k_ref, vt_ref, crow_ref, ccol_ref, gate_ref, o_ref, kx, vb = refs
    else:
        qnt_ref, qrt_ref, kn_ref, krd_ref, vt_ref, o_ref, kx = refs
        vb = vt_ref
    hp = pl.program_id(1)
    qi = pl.program_id(2)
    tk = tq
    t = kx.shape[0]
    ones_lo = PAIR * N_PIECE

    @pl.when(qi == 0)
    def _():
        if fox:
            kx[:, :LANE] = k_ref[...]
            vb[...] = vt_ref[...].astype(BF16)
            heads = ccol_ref.shape[1]
            hrow = lax.broadcasted_iota(jnp.int32, (heads, LANE), 0)
            lcol = lax.broadcasted_iota(jnp.int32, (heads, LANE), 1)
            ext = jnp.zeros((t, LANE), F32)
            for i, piece in enumerate(_split3(ccol_ref[...] * LOG2E)):
                hit = ((hrow == PAIR * hp) & (lcol == i)) | ((hrow == PAIR * hp + 1) & (lcol == N_PIECE + i))
                ext = ext + _dot(piece, jnp.where(hit, 1.0, 0.0).astype(BF16))
            lane = lax.broadcasted_iota(jnp.int32, (t, LANE), 1)
            ext = jnp.where((lane >= ones_lo) & (lane < ones_lo + N_PIECE), 1.0, ext)
            kx[:, LANE:] = ext.astype(BF16)
        else:
            kx[:, :LANE] = kn_ref[...]
            kx[:, LANE:] = krd_ref[...]

    sub = lax.broadcasted_iota(jnp.int32, (LANE, tq), 0)
    zero = jnp.zeros((), BF16)
    qs = []
    for e in range(PAIR):
        in_head = (sub >= e * HEAD_DIM) & (sub < (e + 1) * HEAD_DIM)
        if fox:
            ext = jnp.where((sub >= N_PIECE * e) & (sub < N_PIECE * (e + 1)), -1.0, 0.0)
            for i, piece in enumerate(_split3(crow_ref[e:e + 1, :] * LOG2E)):
                ext = jnp.where(sub == ones_lo + i, piece.astype(F32), ext)
            qs.append(jnp.concatenate([jnp.where(in_head, qt_ref[...], zero), ext.astype(BF16)], axis=0))
        else:
            in_rope = (sub >= e * MLA_ROPE) & (sub < (e + 1) * MLA_ROPE)
            qs.append(jnp.concatenate([jnp.where(in_head, qnt_ref[...], zero),
                                       jnp.where(in_rope, qrt_ref[...], zero)], axis=0))

    key = lax.broadcasted_iota(jnp.int32, (tk, tq), 0)
    qry = lax.broadcasted_iota(jnp.int32, (tk, tq), 1)
    visible = (key <= qry) if fox else ((key // CHUNK) <= (qry // CHUNK))

    def blocks(j0, carry, nblk, diagonal_last):
        starts = [pl.multiple_of((j0 + u) * tk, tk) for u in range(nblk)]
        scores = [[_dot(kx[pl.ds(st, tk), :], qs[e]) for e in range(PAIR)] for st in starts]
        carry = list(carry)
        for u, st in enumerate(starts):
            for e in range(PAIR):
                m, l, acc = carry[3 * e:3 * e + 3]
                s = scores[u][e]
                if diagonal_last and u == nblk - 1:
                    s = jnp.where(visible, s, NEG)
                m_new = jnp.maximum(m, jnp.max(s, axis=0, keepdims=True))
                alpha = jnp.exp2(m - m_new)
                p = jnp.exp2(s - m_new)
                l_new = alpha * l + jnp.sum(p, axis=0, keepdims=True)
                v_j = vb[e * HEAD_DIM:(e + 1) * HEAD_DIM, pl.ds(st, tk)]
                carry[3 * e:3 * e + 3] = [m_new, l_new, alpha * acc + _dot(v_j, p.astype(BF16))]
        return tuple(carry)

    init = (jnp.full((1, tq), NEG, F32), jnp.zeros((1, tq), F32), jnp.zeros((HEAD_DIM, tq), F32)) * PAIR
    odd = qi % 2
    carry = lax.fori_loop(0, qi // 2, lambda i, c: blocks(2 * i, c, 2, False), init)
    carry = lax.fori_loop(0, odd, lambda _, c: blocks(qi - 1, c, 2, True), carry)
    carry = lax.fori_loop(0, 1 - odd, lambda _, c: blocks(qi, c, 1, True), carry)
    o = jnp.concatenate([carry[2] / carry[1], carry[5] / carry[4]], axis=0).T
    if fox:
        o = o * gate_ref[...]
    o_ref[...] = o.astype(BF16)


def _fox_attn_prompt(qt, k, vt_all, layer, cum_t, gate, tq=512):
    b, w, t = qt.shape
    heads = cum_t.shape[1]
    tq = min(tq, t)
    npair = w // LANE
    rowblk = pl.BlockSpec((None, tq, LANE), lambda bi, hp, qi: (bi, qi, hp))
    return pl.pallas_call(
        functools.partial(_attn_prompt_body, fox=True, tq=tq),
        grid=(b, npair, t // tq),
        in_specs=[pl.BlockSpec((None, LANE, tq), lambda bi, hp, qi: (bi, hp, qi)),
                  pl.BlockSpec((None, t, LANE), lambda bi, hp, qi: (bi, 0, hp)),
                  pl.BlockSpec((None, None, LANE, t), lambda bi, hp, qi: (layer, bi, hp, 0)),
                  pl.BlockSpec((None, None, PAIR, tq), lambda bi, hp, qi: (bi, hp, 0, qi)),
                  pl.BlockSpec((None, t, heads), lambda bi, hp, qi: (bi, 0, 0)),
                  rowblk],
        out_specs=rowblk,
        out_shape=jax.ShapeDtypeStruct((b, t, w), BF16),
        scratch_shapes=[pltpu.VMEM((t, 2 * LANE), BF16), pltpu.VMEM((LANE, t), BF16)],
        compiler_params=_params("parallel", "parallel", "arbitrary"),
        name="fox_attn_prompt",
    )(qt, k, vt_all, cum_t.reshape(b, npair, PAIR, t), jnp.swapaxes(cum_t, 1, 2), gate)


def _mla_attn_prompt(qnt, qrt, kn, krd, vt, tq=512):
    b, w, t = qnt.shape
    tq = min(tq, t)
    npair = w // LANE
    qblk = pl.BlockSpec((None, LANE, tq), lambda bi, hp, qi: (bi, hp, qi))
    rowblk = pl.BlockSpec((None, tq, LANE), lambda bi, hp, qi: (bi, qi, hp))
    return pl.pallas_call(
        functools.partial(_attn_prompt_body, fox=False, tq=tq),
        grid=(b, npair, t // tq),
        in_specs=[qblk, qblk,
                  pl.BlockSpec((None, t, LANE), lambda bi, hp, qi: (bi, 0, hp)),
                  pl.BlockSpec((None, t, LANE), lambda bi, hp, qi: (bi, 0, 0)),
                  pl.BlockSpec((None, LANE, t), lambda bi, hp, qi: (bi, hp, 0))],
        out_specs=rowblk,
        out_shape=jax.ShapeDtypeStruct((b, t, w), BF16),
        scratch_shapes=[pltpu.VMEM((t, 2 * LANE), BF16)],
        compiler_params=_params("parallel", "parallel", "arbitrary"),
        name="mla_attn_prompt",
    )(qnt, qrt, kn, krd, vt)


def _two_part_softmax(s_p, s_n, v_p, v_n):
    m = jnp.maximum(jnp.max(s_p, axis=-1, keepdims=True), jnp.max(s_n, axis=-1, keepdims=True))
    p_p = jnp.exp(s_p - m)
    p_n = jnp.exp(s_n - m)
    l = jnp.sum(p_p, axis=-1, keepdims=True) + jnp.sum(p_n, axis=-1, keepdims=True)
    return (_dot(p_p.astype(BF16), v_p) + _dot(p_n.astype(BF16), v_n)) / l


def _fox_attn_sample_body(q_ref, kp_ref, vp_ref, kn_ref, vn_ref, cq_ref, ckp_ref, ckn_ref, gate_ref, o_ref):
    hp = pl.program_id(1)
    t = q_ref.shape[0]
    lane = lax.broadcasted_iota(jnp.int32, (t, LANE), 1)
    hl = lax.broadcasted_iota(jnp.int32, cq_ref.shape, 1)
    row = lax.broadcasted_iota(jnp.int32, (t, t), 0)
    col = lax.broadcasted_iota(jnp.int32, (t, t), 1)
    k_p = kp_ref[...].astype(BF16)
    v_p = vp_ref[...].astype(BF16)
    k_n = kn_ref[...].astype(BF16)
    v_n = vn_ref[...].astype(BF16)
    outs = []
    for e in range(PAIR):
        in_head = (lane >= e * HEAD_DIM) & (lane < (e + 1) * HEAD_DIM)
        q_e = jnp.where(in_head, q_ref[...], jnp.zeros((), BF16))
        cq = jnp.sum(jnp.where(hl == PAIR * hp + e, cq_ref[...], 0.0), axis=1, keepdims=True)
        s_p = _dot_nt(q_e, k_p) + cq - ckp_ref[e:e + 1, :]
        s_n = _dot_nt(q_e, k_n) + cq - ckn_ref[e:e + 1, :]
        s_n = jnp.where(col <= row, s_n, NEG)
        outs.append(_two_part_softmax(s_p, s_n, v_p, v_n))
    o = jnp.where(lane < HEAD_DIM, outs[0], outs[1]) * gate_ref[...]
    o_ref[...] = o.astype(BF16)


def _fox_attn_sample(q, k_past, v_past, k_new, v_new, cum, gate):
    b, t, w = q.shape
    p = k_past.shape[1]
    heads = cum.shape[-1]
    npair = w // LANE
    ck = jnp.swapaxes(cum, 1, 2).reshape(b, npair, PAIR, p + t)
    new = pl.BlockSpec((None, t, LANE), lambda bi, hp: (bi, 0, hp))
    past = pl.BlockSpec((None, p, LANE), lambda bi, hp: (bi, 0, hp))
    return pl.pallas_call(
        _fox_attn_sample_body,
        grid=(b, npair),
        in_specs=[new, past, past, new, new,
                  pl.BlockSpec((None, t, heads), lambda bi, hp: (bi, 0, 0)),
                  pl.BlockSpec((None, None, PAIR, p), lambda bi, hp: (bi, hp, 0, 0)),
                  pl.BlockSpec((None, None, PAIR, t), lambda bi, hp: (bi, hp, 0, 0)),
                  new],
        out_specs=new,
        out_shape=jax.ShapeDtypeStruct((b, t, w), BF16),
        compiler_params=_params("parallel", "parallel"),
        name="fox_attn_sample",
    )(q, k_past, v_past, k_new, v_new, cum[:, p:], ck[..., :p], ck[..., p:], gate)


def _mla_attn_sample_body(qn_ref, qr_ref, cp_ref, krp_ref, cn_ref, krdn_ref, wkn_ref, wv_ref, o_ref, *, past_len):
    t = qn_ref.shape[0]
    p = cp_ref.shape[0]
    lane = lax.broadcasted_iota(jnp.int32, (t, LANE), 1)
    zero = jnp.zeros((), BF16)
    c_p = cp_ref[...].astype(BF16)
    c_n = cn_ref[...].astype(BF16)
    kn_p = _dot(c_p, wkn_ref[...]).astype(BF16)
    v_p = _dot(c_p, wv_ref[...]).astype(BF16)
    kn_n = _dot(c_n, wkn_ref[...]).astype(BF16)
    v_n = _dot(c_n, wv_ref[...]).astype(BF16)
    sr = lax.broadcasted_iota(jnp.int32, (MLA_ROPE, LANE), 0)
    sc = lax.broadcasted_iota(jnp.int32, (MLA_ROPE, LANE), 1)
    dup = jnp.where((sc == sr) | (sc == sr + MLA_ROPE), 1.0, 0.0).astype(BF16)
    krd_p = _dot(krp_ref[...].astype(BF16), dup).astype(BF16)
    krd_n = krdn_ref[...]
    q_chunk = (past_len + lax.broadcasted_iota(jnp.int32, (t, p), 0)) // CHUNK
    vis_p = (lax.broadcasted_iota(jnp.int32, (t, p), 1) // CHUNK) <= q_chunk
    q_chunk_n = (past_len + lax.broadcasted_iota(jnp.int32, (t, t), 0)) // CHUNK
    vis_n = ((past_len + lax.broadcasted_iota(jnp.int32, (t, t), 1)) // CHUNK) <= q_chunk_n
    outs = []
    for e in range(PAIR):
        in_head = (lane >= e * HEAD_DIM) & (lane < (e + 1) * HEAD_DIM)
        in_rope = (lane >= e * MLA_ROPE) & (lane < (e + 1) * MLA_ROPE)
        qn_e = jnp.where(in_head, qn_ref[...], zero)
        qr_e = jnp.where(in_rope, qr_ref[...], zero)
        s_p = jnp.where(vis_p, _dot_nt(qn_e, kn_p) + _dot_nt(qr_e, krd_p), NEG)
        s_n = jnp.where(vis_n, _dot_nt(qn_e, kn_n) + _dot_nt(qr_e, krd_n), NEG)
        outs.append(_two_part_softmax(s_p, s_n, v_p, v_n))
    o_ref[...] = jnp.where(lane < HEAD_DIM, outs[0], outs[1]).astype(BF16)


def _mla_attn_sample(qn, qr, ckv_past, kr_past, ckv_new, krd_new, wkn, wv):
    b, t, w = qn.shape
    p, c = ckv_past.shape[1:]
    npair = w // LANE
    new = pl.BlockSpec((None, t, LANE), lambda bi, hp: (bi, 0, hp))
    wcol = pl.BlockSpec((c, LANE), lambda bi, hp: (0, hp))
    return pl.pallas_call(
        functools.partial(_mla_attn_sample_body, past_len=p),
        grid=(b, npair),
        in_specs=[new, new,
                  pl.BlockSpec((None, p, c), lambda bi, hp: (bi, 0, 0)),
                  pl.BlockSpec((None, p, MLA_ROPE), lambda bi, hp: (bi, 0, 0)),
                  pl.BlockSpec((None, t, c), lambda bi, hp: (bi, 0, 0)),
                  pl.BlockSpec((None, t, LANE), lambda bi, hp: (bi, 0, 0)),
                  wcol, wcol],
        out_specs=new,
        out_shape=jax.ShapeDtypeStruct((b, t, w), BF16),
        compiler_params=_params("parallel", "arbitrary"),
        name="mla_attn_sample",
    )(qn, qr, ckv_past, kr_past, ckv_new, krd_new, wkn, wv)


def _mla_in_body(x_ref, g_ref, w1, gq_ref, gkv_ref, w2, wkn, wv, cos_ref, sin_ref,
                 qn_out, qr_out, ckv_out, kn_out, v_out, kr_out, krd_out, *, q_lora, kv_lora, scale):
    h = (_rms(x_ref[...]) * g_ref[...]).astype(BF16)
    a = _dot(h, w1[...])
    cos = cos_ref[...]
    sin = sin_ref[...]
    c_q = (_rms(a[:, :q_lora]) * gq_ref[...]).astype(BF16)
    c_kv = _rms(a[:, q_lora:q_lora + kv_lora]) * gkv_ref[...]
    ckv_out[...] = c_kv
    o = q_lora + kv_lora
    krd = a[:, o:o + LANE] * cos + a[:, o + LANE:o + 2 * LANE] * sin
    kr_out[...] = krd[:, :MLA_ROPE]
    krd_out[...] = krd.astype(BF16)
    c_kv = c_kv.astype(BF16)
    kn_out[...] = _dot(c_kv, wkn[...]).astype(BF16)
    v_out[...] = _dot(c_kv, wv[...]).astype(BF16)
    w = qn_out.shape[1]
    qn_out[...] = (_dot(c_q, w2[:, :w]) * scale).astype(BF16)
    for p in range(w // LANE):
        qa = _dot(c_q, w2[:, w + p * LANE:w + (p + 1) * LANE])
        qb = _dot(c_q, w2[:, 2 * w + p * LANE:2 * w + (p + 1) * LANE])
        qr_out[:, p * LANE:(p + 1) * LANE] = ((qa * cos + qb * sin) * scale).astype(BF16)


def _mla_in(x, g, w1, gq, gkv, w2, wkn, wv, cos, sin, scale):
    m, d = x.shape
    q_lora = gq.shape[1]
    kv_lora = gkv.shape[1]
    w = wkn.shape[1]
    tm = _row_tile(m)
    if cos.shape[0] < tm:
        cos = jnp.tile(cos, (tm // cos.shape[0], 1))
        sin = jnp.tile(sin, (tm // sin.shape[0], 1))
    nt = cos.shape[0] // tm
    row = lambda n: pl.BlockSpec((tm, n), lambda i: (i, 0))
    tab = pl.BlockSpec((tm, LANE), lambda i: (i % nt, 0))
    return pl.pallas_call(
        functools.partial(_mla_in_body, q_lora=q_lora, kv_lora=kv_lora, scale=scale),
        grid=(m // tm,),
        in_specs=[row(d), _full((1, d)), _full(w1.shape), _full((1, q_lora)), _full((1, kv_lora)),
                  _full(w2.shape), _full(wkn.shape), _full(wv.shape), tab, tab],
        out_specs=[row(w), row(w), row(kv_lora), row(w), row(w), row(MLA_ROPE), row(LANE)],
        out_shape=[jax.ShapeDtypeStruct((m, w), BF16), jax.ShapeDtypeStruct((m, w), BF16),
                   jax.ShapeDtypeStruct((m, kv_lora), F32), jax.ShapeDtypeStruct((m, w), BF16),
                   jax.ShapeDtypeStruct((m, w), BF16), jax.ShapeDtypeStruct((m, MLA_ROPE), F32),
                   jax.ShapeDtypeStruct((m, LANE), BF16)],
        compiler_params=_params("parallel"),
        name="mla_in",
    )(x, g, w1, gq, gkv, w2, wkn, wv, cos, sin)


def _mla_in_t_body(x_ref, g_ref, w1, gq_ref, gkv_ref, wqnt, wqat, wqbt, wkn, wvt, wkrt,
                   cos_ref, sin_ref, cost_ref, sint_ref,
                   qnt_out, qrt_out, kn_out, krd_out, vt_out, ckv_out, krt_out, *, q_lora, kv_lora, scale):
    h = (_rms(x_ref[...]) * g_ref[...]).astype(BF16)
    a = _dot(h, w1[...])
    c_q = (_rms(a[:, :q_lora]) * gq_ref[...]).astype(BF16)
    c_kv = _rms(a[:, q_lora:q_lora + kv_lora]) * gkv_ref[...]
    ckv_out[...] = c_kv
    o = q_lora + kv_lora
    krd_out[...] = (a[:, o:o + LANE] * cos_ref[...] + a[:, o + LANE:o + 2 * LANE] * sin_ref[...]).astype(BF16)
    cost = cost_ref[...]
    sint = sint_ref[...]
    kab = _dot_nt(wkrt[...], h)
    krt_out[...] = kab[:MLA_ROPE] * cost[:MLA_ROPE] + kab[MLA_ROPE:] * sint[:MLA_ROPE]
    c_kv = c_kv.astype(BF16)
    kn_out[...] = _dot(c_kv, wkn[...]).astype(BF16)
    vt_out[...] = _dot_nt(wvt[...], c_kv).astype(BF16)
    qnt_out[...] = (_dot_nt(wqnt[...], c_q) * scale).astype(BF16)
    for p in range(qrt_out.shape[0] // LANE):
        rows = slice(p * LANE, (p + 1) * LANE)
        qa = _dot_nt(wqat[rows, :], c_q)
        qb = _dot_nt(wqbt[rows, :], c_q)
        qrt_out[rows, :] = ((qa * cost + qb * sint) * scale).astype(BF16)


def _mla_in_t(x, g, gq, gkv, wts, tables, scale, layer, n_layers, prev):
    w1, wqnt, wqat, wqbt, wkn, wvt, wkrt = wts
    cos, sin, cost, sint = tables
    b, t, d = x.shape
    q_lora = gq.shape[1]
    kv_lora = gkv.shape[1]
    w = wkn.shape[1]
    tm = _row_tile(t)
    rows = lambda bi, i: (bi, i, 0)
    cols = lambda bi, i: (bi, 0, i)
    res, stacked = _stacked_call(
        functools.partial(_mla_in_t_body, q_lora=q_lora, kv_lora=kv_lora, scale=scale),
        grid=(b, t // tm),
        in_specs=[pl.BlockSpec((None, tm, d), rows), _full((1, d)), _full(w1.shape), _full((1, q_lora)),
                  _full((1, kv_lora)), _full(wqnt.shape), _full(wqat.shape), _full(wqbt.shape), _full(wkn.shape),
                  _full(wvt.shape), _full(wkrt.shape),
                  pl.BlockSpec((tm, LANE), lambda bi, i: (i, 0)), pl.BlockSpec((tm, LANE), lambda bi, i: (i, 0)),
                  pl.BlockSpec((LANE, tm), lambda bi, i: (0, i)), pl.BlockSpec((LANE, tm), lambda bi, i: (0, i))],
        args=[x, g, w1, gq, gkv, wqnt, wqat, wqbt, wkn, wvt, wkrt, cos, sin, cost, sint],
        outs=[((b, w, t), BF16, (None, w, tm), cols, None),
              ((b, w, t), BF16, (None, w, tm), cols, None),
              ((b, t, w), BF16, (None, tm, w), rows, None),
              ((b, t, LANE), BF16, (None, tm, LANE), rows, None),
              ((b, w, t), BF16, (None, w, tm), cols, None),
              ((b, t, kv_lora), F32, (None, tm, kv_lora), rows, n_layers),
              ((b, MLA_ROPE, t), F32, (None, MLA_ROPE, tm), cols, n_layers)],
        layer=layer, prev=prev, sem=("parallel", "parallel"), name="mla_in_t")
    return res[:5], stacked


def _proj_res_body(a_ref, w_ref, x_ref, o_ref):
    o_ref[...] = x_ref[...] + _dot(a_ref[...], w_ref[...])


def _proj_residual(a, w, x):
    m, k = a.shape
    n = w.shape[1]
    tm = _row_tile(m)
    return pl.pallas_call(
        _proj_res_body, grid=(m // tm,),
        in_specs=[pl.BlockSpec((tm, k), lambda i: (i, 0)), _full((k, n)), pl.BlockSpec((tm, n), lambda i: (i, 0))],
        out_specs=pl.BlockSpec((tm, n), lambda i: (i, 0)),
        out_shape=jax.ShapeDtypeStruct((m, n), F32),
        compiler_params=_params("parallel"), name="proj_residual",
    )(a, w, x)


def _memory_kv_body(m_ref, g_ref, w_ref, k_out, v_out):
    h = (_rms(m_ref[...]) * g_ref[...]).astype(BF16)
    n = k_out.shape[-1]
    k_out[...] = _dot(h, w_ref[:, :n])
    v_out[...] = _dot(h, w_ref[:, n:])


def _memory_kv(mem, g_mem, w_kv):
    m, d = mem.shape
    depth, _, n2 = w_kv.shape
    n = n2 // 2
    tm = _row_tile(m)
    out = pl.BlockSpec((None, tm, n), lambda l, i: (l, i, 0))
    return pl.pallas_call(
        _memory_kv_body, grid=(depth, m // tm),
        in_specs=[pl.BlockSpec((tm, d), lambda l, i: (i, 0)),
                  pl.BlockSpec((None, 1, d), lambda l, i: (l, 0, 0)),
                  pl.BlockSpec((None, d, n2), lambda l, i: (l, 0, 0))],
        out_specs=[out, out],
        out_shape=[jax.ShapeDtypeStruct((depth, m, n), F32)] * 2,
        compiler_params=_params("parallel", "parallel"), name="memory_kv",
    )(mem, g_mem, w_kv)


def _cross_body(x_ref, g_ref, wq, mk_ref, mv_ref, wo, o_ref, att, *, heads):
    bb, tm, d = x_ref.shape
    x = x_ref[...].reshape(bb * tm, d)
    h = (_rms(x) * g_ref[...]).astype(BF16)
    dh = wq.shape[1] // heads
    q = (_dot(h, wq[...]) * dh ** -0.5).astype(BF16)
    for b in range(bb):
        for hd in range(heads):
            cols = slice(hd * dh, (hd + 1) * dh)
            s = _dot_nt(q[b * tm:(b + 1) * tm, cols], mk_ref[b, :, cols].astype(BF16))
            p = jnp.exp(s - jnp.max(s, axis=-1, keepdims=True))
            o = _dot(p.astype(BF16), mv_ref[b, :, cols].astype(BF16)) / jnp.sum(p, axis=-1, keepdims=True)
            att[b * tm:(b + 1) * tm, cols] = o.astype(BF16)
    o_ref[...] = (x + _dot(att[...], wo[...])).reshape(bb, tm, d)


def _cross_block(x, g, wq, mk, mv, wo, heads):
    b, t, d = x.shape
    n, xw = mk.shape[1:]
    tm = min(t, 512)
    bb = max(1, min(b, 64 // tm))
    xblk = pl.BlockSpec((bb, tm, d), lambda bi, ti: (bi, ti, 0))
    mblk = pl.BlockSpec((bb, n, xw), lambda bi, ti: (bi, 0, 0))
    return pl.pallas_call(
        functools.partial(_cross_body, heads=heads),
        grid=(b // bb, t // tm),
        in_specs=[xblk, _full((1, d)), _full((d, xw)), mblk, mblk, _full((xw, d))],
        out_specs=xblk,
        out_shape=jax.ShapeDtypeStruct((b, t, d), F32),
        scratch_shapes=[pltpu.VMEM((bb * tm, xw), BF16)],
        compiler_params=_params("parallel", "parallel"), name="cross_block",
    )(x, g, wq, mk, mv, wo)


FF_BLOCK = 256


def _ffn_body(x_ref, g_ref, wg, wu, wd, *rest, final):
    if final:
        gf_ref, o_ref, y_ref = rest
    else:
        (o_ref,) = rest
    x = x_ref[...]
    h = (_rms(x) * g_ref[...]).astype(BF16)
    acc = x
    for c in range(wg.shape[1] // FF_BLOCK):
        cols = slice(c * FF_BLOCK, (c + 1) * FF_BLOCK)
        gate = _dot(h, wg[:, cols])
        up = _dot(h, wu[:, cols])
        act = (gate * jax.nn.sigmoid(gate) * up).astype(BF16)
        acc = acc + _dot(act, wd[cols, :])
    o_ref[...] = acc
    if final:
        y_ref[...] = _rms(acc) * gf_ref[...]


def _ffn_block(x, g, wg, wu, wd, g_final=None):
    m, d = x.shape
    ff = wg.shape[1]
    assert ff % FF_BLOCK == 0
    tm = _row_tile(m)
    row = pl.BlockSpec((tm, d), lambda i: (i, 0))
    final = g_final is not None
    ins = [row, _full((1, d)), _full((d, ff)), _full((d, ff)), _full((ff, d))]
    args = [x, g, wg, wu, wd]
    if final:
        ins.append(_full((1, d)))
        args.append(g_final)
    out = jax.ShapeDtypeStruct((m, d), F32)
    return pl.pallas_call(
        functools.partial(_ffn_body, final=final), grid=(m // tm,),
        in_specs=ins,
        out_specs=[row, row] if final else row,
        out_shape=[out, out] if final else out,
        compiler_params=_params("parallel"), name="ffn_block",
    )(*args)


def _prep_fox(w_in, b_f, w_out, heads):
    width = w_out.shape[0]
    w_in = w_in.astype(BF16)
    wq, wk, wv, wf, wg = jnp.split(w_in, [width, 2 * width, 3 * width, 3 * width + heads], axis=1)
    rows = (wq, wk, wv, wg, jnp.pad(wf, ((0, 0), (0, LANE - heads))),
            jnp.pad(b_f, (0, LANE - heads)).reshape(1, LANE))
    cols = (wq.T, wk.T, wv.T, wk, wg, wf.T, b_f.reshape(heads, 1))
    return {"rows": rows, "cols": cols, "out": w_out.astype(BF16)}


def _prep_mla(w_a, w_qb, w_kvb, w_out, q_lora, kv_lora, heads):
    half = MLA_ROPE // 2
    w_a = w_a.astype(BF16)
    d = w_a.shape[0]
    x1 = w_a[:, q_lora + kv_lora:q_lora + kv_lora + half]
    x2 = w_a[:, q_lora + kv_lora + half:]
    zeros = jnp.zeros((d, LANE - 2 * MLA_ROPE), BF16)
    w1 = jnp.concatenate([w_a[:, :q_lora + kv_lora], x1, x2, x1, x2, zeros, x2, x1, x2, x1, zeros], axis=1)
    wkrt = jnp.concatenate([x1, x2, x2, x1], axis=1).T
    qb = w_qb.astype(BF16).reshape(q_lora, heads, HEAD_DIM + MLA_ROPE)
    wqn = qb[:, :, :HEAD_DIM].reshape(q_lora, heads * HEAD_DIM)
    rope = qb[:, :, HEAD_DIM:]
    swapped = jnp.concatenate([rope[..., half:], rope[..., :half]], axis=-1)

    def pack(r):
        r = r.reshape(q_lora, heads // PAIR, PAIR * MLA_ROPE)
        return jnp.pad(r, ((0, 0), (0, 0), (0, LANE - PAIR * MLA_ROPE))).reshape(q_lora, -1)

    wqa, wqb = pack(rope), pack(swapped)
    kvb = w_kvb.astype(BF16).reshape(kv_lora, heads, 2 * HEAD_DIM)
    wkn = kvb[:, :, :HEAD_DIM].reshape(kv_lora, heads * HEAD_DIM)
    wv = kvb[:, :, HEAD_DIM:].reshape(kv_lora, heads * HEAD_DIM)
    rows = (w1, jnp.concatenate([wqn, wqa, wqb], axis=1), wkn, wv)
    cols = (w1, wqn.T, wqa.T, wqb.T, wkn, wv.T, wkrt)
    return {"rows": rows, "cols": cols, "out": w_out.astype(BF16)}


def _rope_tables(pos):
    half = MLA_ROPE // 2
    inv = ROPE_THETA ** (-jnp.arange(half, dtype=F32) / half)
    ang = pos.astype(F32)[:, None] * inv[None, :]
    cos, sin = jnp.cos(ang), jnp.sin(ang)
    z = jnp.zeros((pos.shape[0], LANE - 2 * MLA_ROPE), F32)
    return (jnp.concatenate([cos, cos, cos, cos, z], axis=1),
            jnp.concatenate([-sin, sin, -sin, sin, z], axis=1))


def _after_mixer(x2, o, w_out, i, b, t, mem_k, mem_v, wts):
    d = x2.shape[1]
    x2 = _proj_residual(o, w_out, x2)
    wxq, wxo = wts["cross"][i]
    x3 = _cross_block(x2.reshape(b, t, d), wts["g_cross"][i], wxq, mem_k[i], mem_v[i], wxo, wts["x_heads"])
    wgt, wup, wdn = wts["ffn"][i]
    last = i == len(wts["ffn"]) - 1
    return _ffn_block(x3.reshape(b * t, d), wts["g_ffn"][i], wgt, wup, wdn, wts["g_final"] if last else None)


def _trunk_prompt(x, mem_k, mem_v, wts):
    b, t, d = x.shape
    m = b * t
    depth = len(wts["ffn"])
    n_fox, n_mla = len(wts["fox"]), len(wts["mla"])
    cos, sin = _rope_tables(jnp.arange(t, dtype=jnp.int32))
    tables = (cos, sin, cos.T, sin.T)
    fox_state = mla_state = None
    for i in range(depth):
        j = i // 2
        if i % 2 == 0:
            wf = wts["fox"][j]
            qt, k, gate, fox_state = _fox_in_t(x, wts["g_mix"][i], wf["cols"], j, n_fox, fox_state)
            cum_t = _cumsum_lanes(fox_state[2], j)
            o = _fox_attn_prompt(qt, k, fox_state[1], j, cum_t, gate)
        else:
            wf = wts["mla"][j]
            (qnt, qrt, kn, krd, vt), mla_state = _mla_in_t(
                x, wts["g_mix"][i], wts["g_mla_q"][j], wts["g_mla_kv"][j], wf["cols"], tables,
                wts["mla_scale"] * LOG2E, j, n_mla, mla_state)
            o = _mla_attn_prompt(qnt, qrt, kn, krd, vt)
        res = _after_mixer(x.reshape(m, d), o.reshape(m, o.shape[-1]), wf["out"], i, b, t, mem_k, mem_v, wts)
        x2, y = res if i == depth - 1 else (res, None)
        x = x2.reshape(b, t, d)
    kt, vt, lft = fox_state
    ckv, krt = mla_state
    heads = lft.shape[2]
    unfold = lambda a: jnp.transpose(a.reshape(n_fox, b, heads, a.shape[2] // heads, t), (0, 1, 4, 2, 3))
    return (y.reshape(b, t, d), unfold(kt), unfold(vt), jnp.swapaxes(lft, 2, 3), ckv, jnp.swapaxes(krt, 2, 3))


def _trunk_sample(x, pos, fox_past, mla_past, mem_k, mem_v, wts):
    b, t, d = x.shape
    m = b * t
    depth = len(wts["ffn"])
    cos, sin = _rope_tables(pos)
    heads = wts["fox_heads"]
    fox_k, fox_v, fox_lf, mla_c, mla_r = [], [], [], [], []
    r3 = lambda a: a.reshape(b, t, a.shape[-1])
    for i in range(depth):
        j = i // 2
        x2 = x.reshape(m, d)
        if i % 2 == 0:
            wf = wts["fox"][j]
            q, k, v, gate, lf = _fox_in(x2, wts["g_mix"][i], *wf["rows"], heads)
            width = q.shape[1]
            k_past, v_past, lf_past = fox_past[0][j], fox_past[1][j], fox_past[2][j]
            p = k_past.shape[1]
            total = p + t
            padded = -(-total // CUM_BLOCK) * CUM_BLOCK
            lf_all = jnp.concatenate([lf_past, r3(lf), jnp.zeros((b, padded - total, heads), F32)], axis=1)
            cum = _cumsum_time(lf_all)[:, :total]
            o = _fox_attn_sample(r3(q), k_past.reshape(b, p, width), v_past.reshape(b, p, width),
                                 r3(k), r3(v), cum, r3(gate))
            hd = width // heads
            fox_k.append(k.reshape(b, t, heads, hd))
            fox_v.append(v.reshape(b, t, heads, hd))
            fox_lf.append(lf.reshape(b, t, heads))
        else:
            wf = wts["mla"][j]
            w1, w2, wkn, wv = wf["rows"]
            qn, qr, ckv, kn, v, kr, krd = _mla_in(x2, wts["g_mix"][i], w1, wts["g_mla_q"][j], wts["g_mla_kv"][j],
                                                  w2, wkn, wv, cos, sin, wts["mla_scale"])
            o = _mla_attn_sample(r3(qn), r3(qr), mla_past[0][j], mla_past[1][j], r3(ckv), r3(krd), wkn, wv)
            mla_c.append(r3(ckv))
            mla_r.append(r3(kr))
        res = _after_mixer(x2, o.reshape(m, o.shape[-1]), wf["out"], i, b, t, mem_k, mem_v, wts)
        x2, y = res if i == depth - 1 else (res, None)
        x = x2.reshape(b, t, d)
    return (y.reshape(b, t, d), jnp.stack(fox_k), jnp.stack(fox_v), jnp.stack(fox_lf),
            jnp.stack(mla_c), jnp.stack(mla_r))


def kernel(x_prompt, x_sample, mem_prompt, cache_fox_k, cache_fox_v, cache_fox_logf, cache_mla_ckv, cache_mla_krope, cache_mem_k, cache_mem_v, g_mix, g_cross, g_mem, g_ffn, g_final, w_fox_in, b_fox_f, w_fox_out, w_mla_a, g_mla_q, g_mla_kv, w_mla_qb, w_mla_kvb, w_mla_out, w_x_q, w_x_kv, w_x_o, w_ffn_gu, w_ffn_down):
    depth, d = g_mix.shape
    fox_heads = b_fox_f.shape[1]
    x_heads = cache_mem_k.shape[3]
    q_lora = g_mla_q.shape[1]
    kv_lora = g_mla_kv.shape[1]
    mla_heads = w_mla_out.shape[1] // HEAD_DIM
    ff = w_ffn_down.shape[1]
    row = lambda g: [g[i].reshape(1, -1) for i in range(g.shape[0])]
    gu = w_ffn_gu.astype(BF16)
    wts = {
        "g_mix": row(g_mix), "g_cross": row(g_cross), "g_ffn": row(g_ffn), "g_final": g_final.reshape(1, d),
        "g_mla_q": row(g_mla_q), "g_mla_kv": row(g_mla_kv),
        "fox": [_prep_fox(w_fox_in[j], b_fox_f[j], w_fox_out[j], fox_heads) for j in range(w_fox_in.shape[0])],
        "mla": [_prep_mla(w_mla_a[j], w_mla_qb[j], w_mla_kvb[j], w_mla_out[j], q_lora, kv_lora, mla_heads)
                for j in range(w_mla_a.shape[0])],
        "cross": [(w_x_q[i].astype(BF16), w_x_o[i].astype(BF16)) for i in range(depth)],
        "ffn": [(gu[i, :, :ff], gu[i, :, ff:], w_ffn_down[i].astype(BF16)) for i in range(depth)],
        "fox_heads": fox_heads, "x_heads": x_heads,
        "mla_scale": (HEAD_DIM + MLA_ROPE) ** -0.5,
    }
    bp, n_mem, _ = mem_prompt.shape
    mk, mv = _memory_kv(mem_prompt.reshape(bp * n_mem, d), g_mem.reshape(depth, 1, d), w_x_kv.astype(BF16))
    xw = mk.shape[-1]
    mk = mk.reshape(depth, bp, n_mem, xw)
    mv = mv.reshape(depth, bp, n_mem, xw)
    y_p, fk_p, fv_p, fl_p, mc_p, mr_p = _trunk_prompt(x_prompt, mk, mv, wts)
    past_len = cache_fox_k.shape[2]
    pos_s = past_len + jnp.arange(x_sample.shape[1], dtype=jnp.int32)
    bs = x_sample.shape[0]
    y_s, fk_s, fv_s, fl_s, mc_s, mr_s = _trunk_sample(
        x_sample, pos_s, (cache_fox_k, cache_fox_v, cache_fox_logf), (cache_mla_ckv, cache_mla_krope),
        cache_mem_k.reshape(depth, bs, n_mem, xw), cache_mem_v.reshape(depth, bs, n_mem, xw), wts)
    dh = xw // x_heads
    return (y_p, y_s, fk_p, fv_p, fl_p, mc_p, mr_p,
            mk.reshape(depth, bp, n_mem, x_heads, dh), mv.reshape(depth, bp, n_mem, x_heads, dh),
            fk_s, fv_s, fl_s, mc_s, mr_s)
```

```python
import functools

import jax
import jax.numpy as jnp
from jax import lax
from jax.experimental import pallas as pl
from jax.experimental.pallas import tpu as pltpu

EPS = 1e-6
CHUNK = 64
ROPE_THETA = 10000.0
LANE = 128
HEAD_DIM = 64
PAIR = 2
MLA_ROPE = 32
NEG = -1e30
LOG2E = 1.4426950408889634
VMEM_LIMIT = 56 * 1024 * 1024
BF16 = jnp.bfloat16
F32 = jnp.float32


def _dot(a, b):
    return jnp.dot(a, b, preferred_element_type=F32)


def _dot_nt(a, b):
    return lax.dot_general(a, b, (((1,), (1,)), ((), ())), preferred_element_type=F32)


def _rms(x):
    return x * lax.rsqrt(jnp.mean(x * x, axis=-1, keepdims=True) + EPS)


def _params(*sem):
    return pltpu.CompilerParams(dimension_semantics=sem, vmem_limit_bytes=VMEM_LIMIT)


def _row_tile(m, cap=512):
    t = min(m, cap)
    assert m % t == 0
    return t


def _full(shape):
    return pl.BlockSpec(shape, lambda *_: (0,) * len(shape), pipeline_mode=pl.Buffered(1))


def _log_sigmoid(f):
    return jnp.minimum(f, 0.0) - jnp.log1p(jnp.exp(-jnp.abs(f)))


def _split3(x):
    hi = x.astype(BF16)
    r = x - hi.astype(F32)
    mid = r.astype(BF16)
    lo = (r - mid.astype(F32)).astype(BF16)
    return hi, mid, lo


def _layer_map(imap, layer, *idx):
    return (layer,) + tuple(imap(*idx))


def _stacked_call(body, *, grid, in_specs, args, outs, layer, prev, sem, name, scratch=()):
    out_specs, out_shape, stacked = [], [], []
    for k, (shape, dtype, blk, imap, n_layers) in enumerate(outs):
        if n_layers is None:
            out_specs.append(pl.BlockSpec(blk, imap))
            out_shape.append(jax.ShapeDtypeStruct(shape, dtype))
        else:
            out_specs.append(pl.BlockSpec((None,) + tuple(blk), functools.partial(_layer_map, imap, layer)))
            out_shape.append(jax.ShapeDtypeStruct((n_layers,) + tuple(shape), dtype))
            stacked.append(k)
    aliases = {}
    in_specs = list(in_specs)
    args = list(args)
    if prev is not None:
        for k, arr in zip(stacked, prev):
            aliases[len(args)] = k
            in_specs.append(pl.BlockSpec(memory_space=pl.ANY))
            args.append(arr)
    n_alias = len(aliases)

    def wrapped(*refs):
        n_in = len(args) - n_alias
        body(*refs[:n_in], *refs[n_in + n_alias:])

    res = pl.pallas_call(
        wrapped, grid=grid, in_specs=in_specs, out_specs=out_specs, out_shape=out_shape,
        input_output_aliases=aliases, scratch_shapes=list(scratch),
        compiler_params=_params(*sem), name=name,
    )(*args)
    return res, [res[k] for k in stacked]


def _fox_in_body(x_ref, g_ref, wq, wk, wv, wg, wf, bf_ref, q_out, k_out, v_out, gate_out, lf_out, *, scale, heads):
    h = (_rms(x_ref[...]) * g_ref[...]).astype(BF16)
    q_out[...] = (_dot(h, wq[...]) * scale).astype(BF16)
    k_out[...] = _dot(h, wk[...])
    v_out[...] = _dot(h, wv[...])
    gate_out[...] = jax.nn.sigmoid(_dot(h, wg[...]))
    lf_out[...] = _log_sigmoid(_dot(h, wf[...]) + bf_ref[...])[:, :heads]


def _fox_in(x, g, wq, wk, wv, wg, wf, bf, heads):
    m, d = x.shape
    w = wq.shape[1]
    tm = _row_tile(m)
    row = lambda n: pl.BlockSpec((tm, n), lambda i: (i, 0))
    return pl.pallas_call(
        functools.partial(_fox_in_body, scale=HEAD_DIM ** -0.5, heads=heads),
        grid=(m // tm,),
        in_specs=[row(d), _full((1, d)), _full((d, w)), _full((d, w)), _full((d, w)), _full((d, w)),
                  _full((d, LANE)), _full((1, LANE))],
        out_specs=[row(w), row(w), row(w), row(w), row(heads)],
        out_shape=[jax.ShapeDtypeStruct((m, w), BF16), jax.ShapeDtypeStruct((m, w), F32),
                   jax.ShapeDtypeStruct((m, w), F32), jax.ShapeDtypeStruct((m, w), F32),
                   jax.ShapeDtypeStruct((m, heads), F32)],
        compiler_params=_params("parallel"),
        name="fox_in",
    )(x, g, wq, wk, wv, wg, wf, bf)


def _fox_in_t_body(x_ref, g_ref, wqt, wkt, wvt, wk, wg, wft, bf_ref,
                   qt_out, k_out, kt_out, vt_out, gate_out, lft_out, *, scale):
    h = (_rms(x_ref[...]) * g_ref[...]).astype(BF16)
    qt_out[...] = (_dot_nt(wqt[...], h) * scale).astype(BF16)
    k_out[...] = _dot(h, wk[...]).astype(BF16)
    kt_out[...] = _dot_nt(wkt[...], h)
    vt_out[...] = _dot_nt(wvt[...], h)
    gate_out[...] = jax.nn.sigmoid(_dot(h, wg[...]))
    lft_out[...] = _log_sigmoid(_dot_nt(wft[...], h) + bf_ref[...])


def _fox_in_t(x, g, wts, layer, n_layers, prev):
    wqt, wkt, wvt, wk, wg, wft, bf = wts
    b, t, d = x.shape
    w = wk.shape[1]
    heads = wft.shape[0]
    tm = _row_tile(t)
    rows = lambda bi, i: (bi, i, 0)
    cols = lambda bi, i: (bi, 0, i)
    res, stacked = _stacked_call(
        functools.partial(_fox_in_t_body, scale=HEAD_DIM ** -0.5 * LOG2E),
        grid=(b, t // tm),
        in_specs=[pl.BlockSpec((None, tm, d), rows), _full((1, d)), _full((w, d)), _full((w, d)), _full((w, d)),
                  _full((d, w)), _full((d, w)), _full((heads, d)), _full((heads, 1))],
        args=[x, g, wqt, wkt, wvt, wk, wg, wft, bf],
        outs=[((b, w, t), BF16, (None, w, tm), cols, None),
              ((b, t, w), BF16, (None, tm, w), rows, None),
              ((b, w, t), F32, (None, w, tm), cols, n_layers),
              ((b, w, t), F32, (None, w, tm), cols, n_layers),
              ((b, t, w), F32, (None, tm, w), rows, None),
              ((b, heads, t), F32, (None, heads, tm), cols, n_layers)],
        layer=layer, prev=prev, sem=("parallel", "parallel"), name="fox_in_t")
    qt, k, _, _, gate, _ = res
    return qt, k, gate, stacked


CUM_BLOCK = 256


def _cumsum_lanes_body(lf_ref, out_ref):
    h, t = lf_ref.shape
    r = lax.broadcasted_iota(jnp.int32, (CUM_BLOCK, CUM_BLOCK), 0)
    c = lax.broadcasted_iota(jnp.int32, (CUM_BLOCK, CUM_BLOCK), 1)
    tri = jnp.where(r <= c, 1.0, 0.0).astype(BF16)
    carry = jnp.zeros((h, 1), F32)
    for b in range(t // CUM_BLOCK):
        hi, mid, lo = _split3(lf_ref[:, b * CUM_BLOCK:(b + 1) * CUM_BLOCK])
        out = _dot(hi, tri) + _dot(mid, tri) + _dot(lo, tri) + carry
        out_ref[:, b * CUM_BLOCK:(b + 1) * CUM_BLOCK] = out
        carry = out[:, CUM_BLOCK - 1:CUM_BLOCK]


def _cumsum_lanes(lf, layer=None):
    b, h, t = lf.shape[-3:]
    assert t % CUM_BLOCK == 0
    if layer is None:
        spec = pl.BlockSpec((None, h, t), lambda i: (i, 0, 0))
    else:
        spec = pl.BlockSpec((None, None, h, t), lambda i: (layer, i, 0, 0))
    return pl.pallas_call(
        _cumsum_lanes_body, grid=(b,),
        in_specs=[spec],
        out_specs=pl.BlockSpec((None, h, t), lambda i: (i, 0, 0)),
        out_shape=jax.ShapeDtypeStruct((b, h, t), F32),
        compiler_params=_params("parallel"), name="logf_cumsum_t",
    )(lf)


N_PIECE = 3


def _attn_prompt_body(*refs, fox, tq):
    if fox:
        qt_ref, k_ref, vt_ref, crow_ref, ccol_ref, gate_ref, o_ref, kx, vb = refs
    else:
        qnt_ref, qrt_ref, kn_ref, krd_ref, vt_ref, o_ref, kx = refs
        vb = vt_ref
    hp = pl.program_id(1)
    qi = pl.program_id(2)
    tk = tq
    t = kx.shape[0]
    ones_lo = PAIR * N_PIECE

    @pl.when(qi == 0)
    def _():
        if fox:
            kx[:, :LANE] = k_ref[...]
            vb[...] = vt_ref[...].astype(BF16)
            heads = ccol_ref.shape[1]
            hrow = lax.broadcasted_iota(jnp.int32, (heads, LANE), 0)
            lcol = lax.broadcasted_iota(jnp.int32, (heads, LANE), 1)
            ext = jnp.zeros((t, LANE), F32)
            for i, piece in enumerate(_split3(ccol_ref[...] * LOG2E)):
                hit = ((hrow == PAIR * hp) & (lcol == i)) | ((hrow == PAIR * hp + 1) & (lcol == N_PIECE + i))
                ext = ext + _dot(piece, jnp.where(hit, 1.0, 0.0).astype(BF16))
            lane = lax.broadcasted_iota(jnp.int32, (t, LANE), 1)
            ext = jnp.where((lane >= ones_lo) & (lane < ones_lo + N_PIECE), 1.0, ext)
            kx[:, LANE:] = ext.astype(BF16)
        else:
            kx[:, :LANE] = kn_ref[...]
            kx[:, LANE:] = krd_ref[...]

    sub = lax.broadcasted_iota(jnp.int32, (LANE, tq), 0)
    zero = jnp.zeros((), BF16)
    qs = []
    for e in range(PAIR):
        in_head = (sub >= e * HEAD_DIM) & (sub < (e + 1) * HEAD_DIM)
        if fox:
            ext = jnp.where((sub >= N_PIECE * e) & (sub < N_PIECE * (e + 1)), -1.0, 0.0)
            for i, piece in enumerate(_split3(crow_ref[e:e + 1, :] * LOG2E)):
                ext = jnp.where(sub == ones_lo + i, piece.astype(F32), ext)
            qs.append(jnp.concatenate([jnp.where(in_head, qt_ref[...], zero), ext.astype(BF16)], axis=0))
        else:
            in_rope = (sub >= e * MLA_ROPE) & (sub < (e + 1) * MLA_ROPE)
            qs.append(jnp.concatenate([jnp.where(in_head, qnt_ref[...], zero),
                                       jnp.where(in_rope, qrt_ref[...], zero)], axis=0))

    key = lax.broadcasted_iota(jnp.int32, (tk, tq), 0)
    qry = lax.broadcasted_iota(jnp.int32, (tk, tq), 1)
    visible = (key <= qry) if fox else ((key // CHUNK) <= (qry // CHUNK))

    def blocks(j0, carry, nblk, diagonal_last):
        starts = [pl.multiple_of((j0 + u) * tk, tk) for u in range(nblk)]
        scores = [[_dot(kx[pl.ds(st, tk), :], qs[e]) for e in range(PAIR)] for st in starts]
        carry = list(carry)
        for u, st in enumerate(starts):
            for e in range(PAIR):
                m, l, acc = carry[3 * e:3 * e + 3]
                s = scores[u][e]
                if diagonal_last and u == nblk - 1:
                    s = jnp.where(visible, s, NEG)
                m_new = jnp.maximum(m, jnp.max(s, axis=0, keepdims=True))
                alpha = jnp.exp2(m - m_new)
                p = jnp.exp2(s - m_new)
                l_new = alpha * l + jnp.sum(p, axis=0, keepdims=True)
                v_j = vb[e * HEAD_DIM:(e + 1) * HEAD_DIM, pl.ds(st, tk)]
                carry[3 * e:3 * e + 3] = [m_new, l_new, alpha * acc + _dot(v_j, p.astype(BF16))]
        return tuple(carry)

    init = (jnp.full((1, tq), NEG, F32), jnp.zeros((1, tq), F32), jnp.zeros((HEAD_DIM, tq), F32)) * PAIR
    odd = qi % 2
    carry = lax.fori_loop(0, qi // 2, lambda i, c: blocks(2 * i, c, 2, False), init)
    carry = lax.fori_loop(0, odd, lambda _, c: blocks(qi - 1, c, 2, True), carry)
    carry = lax.fori_loop(0, 1 - odd, lambda _, c: blocks(qi, c, 1, True), carry)
    o = jnp.concatenate([carry[2] / carry[1], carry[5] / carry[4]], axis=0).T
    if fox:
        o = o * gate_ref[...]
    o_ref[...] = o.astype(BF16)


def _fox_attn_prompt(qt, k, vt_all, layer, cum_t, gate, tq=512):
    b, w, t = qt.shape
    heads = cum_t.shape[1]
    tq = min(tq, t)
    npair = w // LANE
    rowblk = pl.BlockSpec((None, tq, LANE), lambda bi, hp, qi: (bi, qi, hp))
    return pl.pallas_call(
        functools.partial(_attn_prompt_body, fox=True, tq=tq),
        grid=(b, npair, t // tq),
        in_specs=[pl.BlockSpec((None, LANE, tq), lambda bi, hp, qi: (bi, hp, qi)),
                  pl.BlockSpec((None, t, LANE), lambda bi, hp, qi: (bi, 0, hp)),
                  pl.BlockSpec((None, None, LANE, t), lambda bi, hp, qi: (layer, bi, hp, 0)),
                  pl.BlockSpec((None, None, PAIR, tq), lambda bi, hp, qi: (bi, hp, 0, qi)),
                  pl.BlockSpec((None, t, heads), lambda bi, hp, qi: (bi, 0, 0)),
                  rowblk],
        out_specs=rowblk,
        out_shape=jax.ShapeDtypeStruct((b, t, w), BF16),
        scratch_shapes=[pltpu.VMEM((t, 2 * LANE), BF16), pltpu.VMEM((LANE, t), BF16)],
        compiler_params=_params("parallel", "parallel", "arbitrary"),
        name="fox_attn_prompt",
    )(qt, k, vt_all, cum_t.reshape(b, npair, PAIR, t), jnp.swapaxes(cum_t, 1, 2), gate)


def _mla_attn_prompt(qnt, qrt, kn, krd, vt, tq=512):
    b, w, t = qnt.shape
    tq = min(tq, t)
    npair = w // LANE
    qblk = pl.BlockSpec((None, LANE, tq), lambda bi, hp, qi: (bi, hp, qi))
    rowblk = pl.BlockSpec((None, tq, LANE), lambda bi, hp, qi: (bi, qi, hp))
    return pl.pallas_call(
        functools.partial(_attn_prompt_body, fox=False, tq=tq),
        grid=(b, npair, t // tq),
        in_specs=[qblk, qblk,
                  pl.BlockSpec((None, t, LANE), lambda bi, hp, qi: (bi, 0, hp)),
                  pl.BlockSpec((None, t, LANE), lambda bi, hp, qi: (bi, 0, 0)),
                  pl.BlockSpec((None, LANE, t), lambda bi, hp, qi: (bi, hp, 0))],
        out_specs=rowblk,
        out_shape=jax.ShapeDtypeStruct((b, t, w), BF16),
        scratch_shapes=[pltpu.VMEM((t, 2 * LANE), BF16)],
        compiler_params=_params("parallel", "parallel", "arbitrary"),
        name="mla_attn_prompt",
    )(qnt, qrt, kn, krd, vt)


def _fox_attn_sample_body(q_ref, kt_ref, vt_ref, kn_ref, vn_ref, cq_ref, ck_ref, gate_ref, o_ref):
    hp = pl.program_id(1)
    t = q_ref.shape[0]
    p = kt_ref.shape[1]
    rows = PAIR * t
    lane = lax.broadcasted_iota(jnp.int32, (rows, LANE), 1)
    row = lax.broadcasted_iota(jnp.int32, (rows, LANE), 0)
    q2 = jnp.concatenate([q_ref[...]] * PAIR, axis=0)
    q2 = jnp.where((lane // HEAD_DIM) == (row // t), q2, jnp.zeros((), BF16))
    hl = lax.broadcasted_iota(jnp.int32, cq_ref.shape, 1)
    cq = jnp.concatenate([jnp.sum(jnp.where(hl == PAIR * hp + e, cq_ref[...], 0.0), axis=1, keepdims=True)
                          for e in range(PAIR)], axis=0)
    first_p = lax.broadcasted_iota(jnp.int32, (rows, p), 0) < t
    rn = lax.broadcasted_iota(jnp.int32, (rows, t), 0)
    cn = lax.broadcasted_iota(jnp.int32, (rows, t), 1)
    s_p = _dot(q2, kt_ref[...].astype(BF16)) + cq - jnp.where(first_p, ck_ref[0:1, :p], ck_ref[1:2, :p])
    s_n = _dot_nt(q2, kn_ref[...].astype(BF16)) + cq - jnp.where(rn < t, ck_ref[0:1, p:], ck_ref[1:2, p:])
    s_n = jnp.where(cn <= lax.rem(rn, t), s_n, NEG)
    m = jnp.maximum(jnp.max(s_p, axis=-1, keepdims=True), jnp.max(s_n, axis=-1, keepdims=True))
    p_p = jnp.exp(s_p - m)
    p_n = jnp.exp(s_n - m)
    l = jnp.sum(p_p, axis=-1, keepdims=True) + jnp.sum(p_n, axis=-1, keepdims=True)
    o = (_dot_nt(p_p.astype(BF16), vt_ref[...].astype(BF16)) + _dot(p_n.astype(BF16), vn_ref[...].astype(BF16))) / l
    o = jnp.where(lax.broadcasted_iota(jnp.int32, (t, LANE), 1) < HEAD_DIM, o[:t], o[t:]) * gate_ref[...]
    o_ref[...] = o.astype(BF16)


def _fox_attn_sample(q, kt_cache, vt_cache, layer, k_new, v_new, cum_t, gate):
    b, t, w = q.shape
    p = kt_cache.shape[-1]
    heads = cum_t.shape[1]
    npair = w // LANE
    new = pl.BlockSpec((None, t, LANE), lambda bi, hp: (bi, 0, hp))
    past = pl.BlockSpec((None, None, LANE, p), lambda bi, hp: (layer, bi, hp, 0))
    return pl.pallas_call(
        _fox_attn_sample_body,
        grid=(b, npair),
        in_specs=[new, past, past, new, new,
                  pl.BlockSpec((None, t, heads), lambda bi, hp: (bi, 0, 0)),
                  pl.BlockSpec((None, None, PAIR, p + t), lambda bi, hp: (bi, hp, 0, 0)),
                  new],
        out_specs=new,
        out_shape=jax.ShapeDtypeStruct((b, t, w), BF16),
        compiler_params=_params("parallel", "parallel"),
        name="fox_attn_sample",
    )(q, kt_cache, vt_cache, k_new, v_new, jnp.swapaxes(cum_t[:, :, p:], 1, 2),
      cum_t.reshape(b, npair, PAIR, p + t), gate)


def _mla_attn_sample_body(qn_ref, qr_ref, cp_ref, krp_ref, cn_ref, krn_ref, wkn_ref, wv_ref, o_ref, *, past_len):
    t, w = qn_ref.shape
    p = cp_ref.shape[0]
    heads = w // HEAD_DIM
    rows = heads * t
    lane = lax.broadcasted_iota(jnp.int32, (rows, w), 1)
    row = lax.broadcasted_iota(jnp.int32, (rows, w), 0)
    own = (lane // HEAD_DIM) == (row // t)
    q_wide = jnp.where(own, jnp.concatenate([qn_ref[...]] * heads, axis=0), jnp.zeros((), BF16))
    q_lat = _dot_nt(q_wide, wkn_ref[...]).astype(BF16)
    qr = qr_ref[...]
    c_p = cp_ref[...].astype(BF16)
    c_n = cn_ref[...].astype(BF16)
    s_p = _dot_nt(q_lat, c_p) + _dot(qr, krp_ref[...].astype(BF16))
    s_n = _dot_nt(q_lat, c_n) + _dot_nt(qr, krn_ref[...].astype(BF16))
    q_chunk_p = (past_len + lax.rem(lax.broadcasted_iota(jnp.int32, (rows, p), 0), t)) // CHUNK
    s_p = jnp.where((lax.broadcasted_iota(jnp.int32, (rows, p), 1) // CHUNK) <= q_chunk_p, s_p, NEG)
    q_chunk_n = (past_len + lax.rem(lax.broadcasted_iota(jnp.int32, (rows, t), 0), t)) // CHUNK
    s_n = jnp.where(((past_len + lax.broadcasted_iota(jnp.int32, (rows, t), 1)) // CHUNK) <= q_chunk_n, s_n, NEG)
    m = jnp.maximum(jnp.max(s_p, axis=-1, keepdims=True), jnp.max(s_n, axis=-1, keepdims=True))
    p_p = jnp.exp(s_p - m)
    p_n = jnp.exp(s_n - m)
    l = jnp.sum(p_p, axis=-1, keepdims=True) + jnp.sum(p_n, axis=-1, keepdims=True)
    o_lat = (_dot(p_p.astype(BF16), c_p) + _dot(p_n.astype(BF16), c_n)) / l
    o_wide = jnp.where(own, _dot(o_lat.astype(BF16), wv_ref[...]), 0.0)
    o = o_wide[:t]
    for h in range(1, heads):
        o = o + o_wide[h * t:(h + 1) * t]
    o_ref[...] = o.astype(BF16)


def _mla_attn_sample(qn, qr_rows, ckv_cache, krt_cache, layer, ckv_new, kr_new, wkn, wv):
    b, t, w = qn.shape
    p, c = ckv_cache.shape[2:]
    rows = qr_rows.shape[1]
    blk = lambda *s: pl.BlockSpec((None,) + s, lambda bi: (bi,) + (0,) * len(s))
    past = lambda *s: pl.BlockSpec((None, None) + s, lambda bi: (layer, bi) + (0,) * len(s))
    return pl.pallas_call(
        functools.partial(_mla_attn_sample_body, past_len=p),
        grid=(b,),
        in_specs=[blk(t, w), blk(rows, MLA_ROPE), past(p, c), past(MLA_ROPE, p), blk(t, c), blk(t, MLA_ROPE),
                  _full(wkn.shape), _full(wv.shape)],
        out_specs=blk(t, w),
        out_shape=jax.ShapeDtypeStruct((b, t, w), BF16),
        compiler_params=_params("parallel"),
        name="mla_attn_sample",
    )(qn, qr_rows, ckv_cache, krt_cache, ckv_new, kr_new, wkn, wv)


def _mla_in_body(x_ref, g_ref, w1, gq_ref, gkv_ref, w2, cos_ref, sin_ref, cosq_ref, sinq_ref,
                 qn_out, qr_out, ckv_out, kr_out, *, q_lora, kv_lora, scale):
    h = (_rms(x_ref[...]) * g_ref[...]).astype(BF16)
    a = _dot(h, w1[...])
    c_q = (_rms(a[:, :q_lora]) * gq_ref[...]).astype(BF16)
    ckv_out[...] = _rms(a[:, q_lora:q_lora + kv_lora]) * gkv_ref[...]
    o = q_lora + kv_lora
    krd = a[:, o:o + LANE] * cos_ref[...] + a[:, o + LANE:o + 2 * LANE] * sin_ref[...]
    kr_out[...] = krd[:, :MLA_ROPE]
    w = qn_out.shape[1]
    r = qr_out.shape[1]
    qn_out[...] = (_dot(c_q, w2[:, :w]) * scale).astype(BF16)
    qr = _dot(c_q, w2[:, w:w + r]) * cosq_ref[...] + _dot(c_q, w2[:, w + r:w + 2 * r]) * sinq_ref[...]
    qr_out[...] = (qr * scale).astype(BF16)


def _mla_in(x, g, w1, gq, gkv, w2, cos, sin, heads, scale):
    m, d = x.shape
    q_lora = gq.shape[1]
    kv_lora = gkv.shape[1]
    w = heads * HEAD_DIM
    r = heads * MLA_ROPE
    tm = _row_tile(m)
    t = cos.shape[0]
    assert tm % t == 0
    per_head = lambda tab: jnp.tile(tab[:, :MLA_ROPE], (tm // t, heads))
    cosq, sinq = per_head(cos), per_head(sin)
    cos, sin = jnp.tile(cos, (tm // t, 1)), jnp.tile(sin, (tm // t, 1))
    row = lambda n: pl.BlockSpec((tm, n), lambda i: (i, 0))
    return pl.pallas_call(
        functools.partial(_mla_in_body, q_lora=q_lora, kv_lora=kv_lora, scale=scale),
        grid=(m // tm,),
        in_specs=[row(d), _full((1, d)), _full(w1.shape), _full((1, q_lora)), _full((1, kv_lora)),
                  _full(w2.shape), _full((tm, LANE)), _full((tm, LANE)), _full((tm, r)), _full((tm, r))],
        out_specs=[row(w), row(r), row(kv_lora), row(MLA_ROPE)],
        out_shape=[jax.ShapeDtypeStruct((m, w), BF16), jax.ShapeDtypeStruct((m, r), BF16),
                   jax.ShapeDtypeStruct((m, kv_lora), F32), jax.ShapeDtypeStruct((m, MLA_ROPE), F32)],
        compiler_params=_params("parallel"),
        name="mla_in",
    )(x, g, w1, gq, gkv, w2, cos, sin, cosq, sinq)


def _mla_in_t_body(x_ref, g_ref, w1, gq_ref, gkv_ref, wqnt, wqat, wqbt, wkn, wvt, wkrt,
                   cos_ref, sin_ref, cost_ref, sint_ref,
                   qnt_out, qrt_out, kn_out, krd_out, vt_out, ckv_out, krt_out, *, q_lora, kv_lora, scale):
    h = (_rms(x_ref[...]) * g_ref[...]).astype(BF16)
    a = _dot(h, w1[...])
    c_q = (_rms(a[:, :q_lora]) * gq_ref[...]).astype(BF16)
    c_kv = _rms(a[:, q_lora:q_lora + kv_lora]) * gkv_ref[...]
    ckv_out[...] = c_kv
    o = q_lora + kv_lora
    krd_out[...] = (a[:, o:o + LANE] * cos_ref[...] + a[:, o + LANE:o + 2 * LANE] * sin_ref[...]).astype(BF16)
    cost = cost_ref[...]
    sint = sint_ref[...]
    kab = _dot_nt(wkrt[...], h)
    krt_out[...] = kab[:MLA_ROPE] * cost[:MLA_ROPE] + kab[MLA_ROPE:] * sint[:MLA_ROPE]
    c_kv = c_kv.astype(BF16)
    kn_out[...] = _dot(c_kv, wkn[...]).astype(BF16)
    vt_out[...] = _dot_nt(wvt[...], c_kv).astype(BF16)
    qnt_out[...] = (_dot_nt(wqnt[...], c_q) * scale).astype(BF16)
    for p in range(qrt_out.shape[0] // LANE):
        rows = slice(p * LANE, (p + 1) * LANE)
        qa = _dot_nt(wqat[rows, :], c_q)
        qb = _dot_nt(wqbt[rows, :], c_q)
        qrt_out[rows, :] = ((qa * cost + qb * sint) * scale).astype(BF16)


def _mla_in_t(x, g, gq, gkv, wts, tables, scale, layer, n_layers, prev):
    w1, wqnt, wqat, wqbt, wkn, wvt, wkrt = wts
    cos, sin, cost, sint = tables
    b, t, d = x.shape
    q_lora = gq.shape[1]
    kv_lora = gkv.shape[1]
    w = wkn.shape[1]
    tm = _row_tile(t)
    rows = lambda bi, i: (bi, i, 0)
    cols = lambda bi, i: (bi, 0, i)
    res, stacked = _stacked_call(
        functools.partial(_mla_in_t_body, q_lora=q_lora, kv_lora=kv_lora, scale=scale),
        grid=(b, t // tm),
        in_specs=[pl.BlockSpec((None, tm, d), rows), _full((1, d)), _full(w1.shape), _full((1, q_lora)),
                  _full((1, kv_lora)), _full(wqnt.shape), _full(wqat.shape), _full(wqbt.shape), _full(wkn.shape),
                  _full(wvt.shape), _full(wkrt.shape),
                  pl.BlockSpec((tm, LANE), lambda bi, i: (i, 0)), pl.BlockSpec((tm, LANE), lambda bi, i: (i, 0)),
                  pl.BlockSpec((LANE, tm), lambda bi, i: (0, i)), pl.BlockSpec((LANE, tm), lambda bi, i: (0, i))],
        args=[x, g, w1, gq, gkv, wqnt, wqat, wqbt, wkn, wvt, wkrt, cos, sin, cost, sint],
        outs=[((b, w, t), BF16, (None, w, tm), cols, None),
              ((b, w, t), BF16, (None, w, tm), cols, None),
              ((b, t, w), BF16, (None, tm, w), rows, None),
              ((b, t, LANE), BF16, (None, tm, LANE), rows, None),
              ((b, w, t), BF16, (None, w, tm), cols, None),
              ((b, t, kv_lora), F32, (None, tm, kv_lora), rows, n_layers),
              ((b, MLA_ROPE, t), F32, (None, MLA_ROPE, tm), cols, n_layers)],
        layer=layer, prev=prev, sem=("parallel", "parallel"), name="mla_in_t")
    return res[:5], stacked


def _proj_res_body(a_ref, w_ref, x_ref, o_ref):
    o_ref[...] = x_ref[...] + _dot(a_ref[...], w_ref[...])


def _proj_residual(a, w, x):
    m, k = a.shape
    n = w.shape[1]
    tm = _row_tile(m)
    return pl.pallas_call(
        _proj_res_body, grid=(m // tm,),
        in_specs=[pl.BlockSpec((tm, k), lambda i: (i, 0)), _full((k, n)), pl.BlockSpec((tm, n), lambda i: (i, 0))],
        out_specs=pl.BlockSpec((tm, n), lambda i: (i, 0)),
        out_shape=jax.ShapeDtypeStruct((m, n), F32),
        compiler_params=_params("parallel"), name="proj_residual",
    )(a, w, x)


def _memory_kv_body(m_ref, g_ref, w_ref, k_out, v_out):
    h = (_rms(m_ref[...]) * g_ref[...]).astype(BF16)
    n = k_out.shape[-1]
    k_out[...] = _dot(h, w_ref[:, :n])
    v_out[...] = _dot(h, w_ref[:, n:])


def _memory_kv(mem, g_mem, w_kv):
    m, d = mem.shape
    depth, _, n2 = w_kv.shape
    n = n2 // 2
    tm = _row_tile(m)
    out = pl.BlockSpec((None, tm, n), lambda l, i: (l, i, 0))
    return pl.pallas_call(
        _memory_kv_body, grid=(depth, m // tm),
        in_specs=[pl.BlockSpec((tm, d), lambda l, i: (i, 0)),
                  pl.BlockSpec((None, 1, d), lambda l, i: (l, 0, 0)),
                  pl.BlockSpec((None, d, n2), lambda l, i: (l, 0, 0))],
        out_specs=[out, out],
        out_shape=[jax.ShapeDtypeStruct((depth, m, n), F32)] * 2,
        compiler_params=_params("parallel", "parallel"), name="memory_kv",
    )(mem, g_mem, w_kv)


def _cross_body(x_ref, g_ref, wq, mk_ref, mv_ref, wo, o_ref, att, *, heads):
    bb, tm, d = x_ref.shape
    x = x_ref[...].reshape(bb * tm, d)
    h = (_rms(x) * g_ref[...]).astype(BF16)
    dh = wq.shape[1] // heads
    q = (_dot(h, wq[...]) * dh ** -0.5).astype(BF16)
    for b in range(bb):
        for hd in range(heads):
            cols = slice(hd * dh, (hd + 1) * dh)
            s = _dot_nt(q[b * tm:(b + 1) * tm, cols], mk_ref[b, :, cols].astype(BF16))
            p = jnp.exp(s - jnp.max(s, axis=-1, keepdims=True))
            o = _dot(p.astype(BF16), mv_ref[b, :, cols].astype(BF16)) / jnp.sum(p, axis=-1, keepdims=True)
            att[b * tm:(b + 1) * tm, cols] = o.astype(BF16)
    o_ref[...] = (x + _dot(att[...], wo[...])).reshape(bb, tm, d)


def _cross_block(x, g, wq, mk, mv, layer, wo, heads):
    b, t, d = x.shape
    n, xw = mk.shape[2:]
    tm = min(t, 512)
    bb = max(1, min(b, 64 // tm))
    xblk = pl.BlockSpec((bb, tm, d), lambda bi, ti: (bi, ti, 0))
    mblk = pl.BlockSpec((None, bb, n, xw), lambda bi, ti: (layer, bi, 0, 0))
    return pl.pallas_call(
        functools.partial(_cross_body, heads=heads),
        grid=(b // bb, t // tm),
        in_specs=[xblk, _full((1, d)), _full((d, xw)), mblk, mblk, _full((xw, d))],
        out_specs=xblk,
        out_shape=jax.ShapeDtypeStruct((b, t, d), F32),
        scratch_shapes=[pltpu.VMEM((bb * tm, xw), BF16)],
        compiler_params=_params("parallel", "parallel"), name="cross_block",
    )(x, g, wq, mk, mv, wo)


FF_BLOCK = 256


def _ffn_body(x_ref, g_ref, wg, wu, wd, *rest, final):
    if final:
        gf_ref, o_ref, y_ref = rest
    else:
        (o_ref,) = rest
    x = x_ref[...]
    h = (_rms(x) * g_ref[...]).astype(BF16)
    acc = x
    for c in range(wg.shape[1] // FF_BLOCK):
        cols = slice(c * FF_BLOCK, (c + 1) * FF_BLOCK)
        gate = _dot(h, wg[:, cols])
        up = _dot(h, wu[:, cols])
        act = (gate * jax.nn.sigmoid(gate) * up).astype(BF16)
        acc = acc + _dot(act, wd[cols, :])
    o_ref[...] = acc
    if final:
        y_ref[...] = _rms(acc) * gf_ref[...]


def _ffn_block(x, g, wg, wu, wd, g_final=None):
    m, d = x.shape
    ff = wg.shape[1]
    assert ff % FF_BLOCK == 0
    tm = _row_tile(m)
    row = pl.BlockSpec((tm, d), lambda i: (i, 0))
    final = g_final is not None
    ins = [row, _full((1, d)), _full((d, ff)), _full((d, ff)), _full((ff, d))]
    args = [x, g, wg, wu, wd]
    if final:
        ins.append(_full((1, d)))
        args.append(g_final)
    out = jax.ShapeDtypeStruct((m, d), F32)
    return pl.pallas_call(
        functools.partial(_ffn_body, final=final), grid=(m // tm,),
        in_specs=ins,
        out_specs=[row, row] if final else row,
        out_shape=[out, out] if final else out,
        compiler_params=_params("parallel"), name="ffn_block",
    )(*args)


def _prep_fox(w_in, b_f, w_out, heads):
    width = w_out.shape[0]
    w_in = w_in.astype(BF16)
    wq, wk, wv, wf, wg = jnp.split(w_in, [width, 2 * width, 3 * width, 3 * width + heads], axis=1)
    rows = (wq, wk, wv, wg, jnp.pad(wf, ((0, 0), (0, LANE - heads))),
            jnp.pad(b_f, (0, LANE - heads)).reshape(1, LANE))
    cols = (wq.T, wk.T, wv.T, wk, wg, wf.T, b_f.reshape(heads, 1))
    return {"rows": rows, "cols": cols, "out": w_out.astype(BF16)}


def _prep_mla(w_a, w_qb, w_kvb, w_out, q_lora, kv_lora, heads):
    half = MLA_ROPE // 2
    w_a = w_a.astype(BF16)
    d = w_a.shape[0]
    x1 = w_a[:, q_lora + kv_lora:q_lora + kv_lora + half]
    x2 = w_a[:, q_lora + kv_lora + half:]
    zeros = jnp.zeros((d, LANE - 2 * MLA_ROPE), BF16)
    w1 = jnp.concatenate([w_a[:, :q_lora + kv_lora], x1, x2, x1, x2, zeros, x2, x1, x2, x1, zeros], axis=1)
    wkrt = jnp.concatenate([x1, x2, x2, x1], axis=1).T
    qb = w_qb.astype(BF16).reshape(q_lora, heads, HEAD_DIM + MLA_ROPE)
    wqn = qb[:, :, :HEAD_DIM].reshape(q_lora, heads * HEAD_DIM)
    rope = qb[:, :, HEAD_DIM:]
    swapped = jnp.concatenate([rope[..., half:], rope[..., :half]], axis=-1)

    def pack(r):
        r = r.reshape(q_lora, heads // PAIR, PAIR * MLA_ROPE)
        return jnp.pad(r, ((0, 0), (0, 0), (0, LANE - PAIR * MLA_ROPE))).reshape(q_lora, -1)

    wqa, wqb = pack(rope), pack(swapped)
    kvb = w_kvb.astype(BF16).reshape(kv_lora, heads, 2 * HEAD_DIM)
    wkn = kvb[:, :, :HEAD_DIM].reshape(kv_lora, heads * HEAD_DIM)
    wv = kvb[:, :, HEAD_DIM:].reshape(kv_lora, heads * HEAD_DIM)
    flat = lambda r: r.reshape(q_lora, heads * MLA_ROPE)
    rows = (w1, jnp.concatenate([wqn, flat(rope), flat(swapped)], axis=1), wkn, wv)
    cols = (w1, wqn.T, wqa.T, wqb.T, wkn, wv.T, wkrt)
    return {"rows": rows, "cols": cols, "out": w_out.astype(BF16)}


def _rope_tables(pos):
    half = MLA_ROPE // 2
    inv = ROPE_THETA ** (-jnp.arange(half, dtype=F32) / half)
    ang = pos.astype(F32)[:, None] * inv[None, :]
    cos, sin = jnp.cos(ang), jnp.sin(ang)
    z = jnp.zeros((pos.shape[0], LANE - 2 * MLA_ROPE), F32)
    return (jnp.concatenate([cos, cos, cos, cos, z], axis=1),
            jnp.concatenate([-sin, sin, -sin, sin, z], axis=1))


def _after_mixer(x2, o, w_out, i, b, t, mem_k, mem_v, wts):
    d = x2.shape[1]
    x2 = _proj_residual(o, w_out, x2)
    wxq, wxo = wts["cross"][i]
    x3 = _cross_block(x2.reshape(b, t, d), wts["g_cross"][i], wxq, mem_k, mem_v, i, wxo, wts["x_heads"])
    wgt, wup, wdn = wts["ffn"][i]
    last = i == len(wts["ffn"]) - 1
    return _ffn_block(x3.reshape(b * t, d), wts["g_ffn"][i], wgt, wup, wdn, wts["g_final"] if last else None)


def _trunk_prompt(x, mem_k, mem_v, wts):
    b, t, d = x.shape
    m = b * t
    depth = len(wts["ffn"])
    n_fox, n_mla = len(wts["fox"]), len(wts["mla"])
    cos, sin = _rope_tables(jnp.arange(t, dtype=jnp.int32))
    tables = (cos, sin, cos.T, sin.T)
    fox_state = mla_state = None
    for i in range(depth):
        j = i // 2
        if i % 2 == 0:
            wf = wts["fox"][j]
            qt, k, gate, fox_state = _fox_in_t(x, wts["g_mix"][i], wf["cols"], j, n_fox, fox_state)
            cum_t = _cumsum_lanes(fox_state[2], j)
            o = _fox_attn_prompt(qt, k, fox_state[1], j, cum_t, gate)
        else:
            wf = wts["mla"][j]
            (qnt, qrt, kn, krd, vt), mla_state = _mla_in_t(
                x, wts["g_mix"][i], wts["g_mla_q"][j], wts["g_mla_kv"][j], wf["cols"], tables,
                wts["mla_scale"] * LOG2E, j, n_mla, mla_state)
            o = _mla_attn_prompt(qnt, qrt, kn, krd, vt)
        res = _after_mixer(x.reshape(m, d), o.reshape(m, o.shape[-1]), wf["out"], i, b, t, mem_k, mem_v, wts)
        x2, y = res if i == depth - 1 else (res, None)
        x = x2.reshape(b, t, d)
    kt, vt, lft = fox_state
    ckv, krt = mla_state
    heads = lft.shape[2]
    unfold = lambda a: jnp.transpose(a.reshape(n_fox, b, heads, a.shape[2] // heads, t), (0, 1, 4, 2, 3))
    return (y.reshape(b, t, d), unfold(kt), unfold(vt), jnp.swapaxes(lft, 2, 3), ckv, jnp.swapaxes(krt, 2, 3))


def _trunk_sample(x, pos, fox_past, mla_past, mem_k, mem_v, wts):
    b, t, d = x.shape
    m = b * t
    depth = len(wts["ffn"])
    cos, sin = _rope_tables(pos)
    heads = wts["fox_heads"]
    mla_heads = wts["mla_heads"]
    fox_k, fox_v, fox_lf, mla_c, mla_r = [], [], [], [], []
    r3 = lambda a: a.reshape(b, t, a.shape[-1])
    n_fox, _, p, _, hd = fox_past[0].shape
    kt_cache = jnp.transpose(fox_past[0], (0, 1, 3, 4, 2)).reshape(n_fox, b, heads * hd, p)
    vt_cache = jnp.transpose(fox_past[1], (0, 1, 3, 4, 2)).reshape(n_fox, b, heads * hd, p)
    lft_cache = jnp.swapaxes(fox_past[2], 2, 3)
    krt_cache = jnp.swapaxes(mla_past[1], 2, 3)
    total = p + t
    pad = jnp.zeros((b, heads, -(-total // CUM_BLOCK) * CUM_BLOCK - total), F32)
    for i in range(depth):
        j = i // 2
        x2 = x.reshape(m, d)
        if i % 2 == 0:
            wf = wts["fox"][j]
            q, k, v, gate, lf = _fox_in(x2, wts["g_mix"][i], *wf["rows"], heads)
            lf_all = jnp.concatenate([lft_cache[j], jnp.swapaxes(r3(lf), 1, 2), pad], axis=2)
            cum_t = _cumsum_lanes(lf_all)[:, :, :total]
            o = _fox_attn_sample(r3(q), kt_cache, vt_cache, j, r3(k), r3(v), cum_t, r3(gate))
            fox_k.append(k.reshape(b, t, heads, hd))
            fox_v.append(v.reshape(b, t, heads, hd))
            fox_lf.append(lf.reshape(b, t, heads))
        else:
            wf = wts["mla"][j]
            w1, w2, wkn, wv = wf["rows"]
            qn, qr, ckv, kr = _mla_in(x2, wts["g_mix"][i], w1, wts["g_mla_q"][j], wts["g_mla_kv"][j],
                                      w2, cos, sin, mla_heads, wts["mla_scale"])
            qr_rows = jnp.swapaxes(qr.reshape(b, t, mla_heads, MLA_ROPE), 1, 2).reshape(b, mla_heads * t, MLA_ROPE)
            o = _mla_attn_sample(r3(qn), qr_rows, mla_past[0], krt_cache, j, r3(ckv), r3(kr), wkn, wv)
            mla_c.append(r3(ckv))
            mla_r.append(r3(kr))
        res = _after_mixer(x2, o.reshape(m, o.shape[-1]), wf["out"], i, b, t, mem_k, mem_v, wts)
        x2, y = res if i == depth - 1 else (res, None)
        x = x2.reshape(b, t, d)
    return (y.reshape(b, t, d), jnp.stack(fox_k), jnp.stack(fox_v), jnp.stack(fox_lf),
            jnp.stack(mla_c), jnp.stack(mla_r))


def kernel(x_prompt, x_sample, mem_prompt, cache_fox_k, cache_fox_v, cache_fox_logf, cache_mla_ckv, cache_mla_krope, cache_mem_k, cache_mem_v, g_mix, g_cross, g_mem, g_ffn, g_final, w_fox_in, b_fox_f, w_fox_out, w_mla_a, g_mla_q, g_mla_kv, w_mla_qb, w_mla_kvb, w_mla_out, w_x_q, w_x_kv, w_x_o, w_ffn_gu, w_ffn_down):
    depth, d = g_mix.shape
    fox_heads = b_fox_f.shape[1]
    x_heads = cache_mem_k.shape[3]
    q_lora = g_mla_q.shape[1]
    kv_lora = g_mla_kv.shape[1]
    mla_heads = w_mla_out.shape[1] // HEAD_DIM
    ff = w_ffn_down.shape[1]
    row = lambda g: [g[i].reshape(1, -1) for i in range(g.shape[0])]
    gu = w_ffn_gu.astype(BF16)
    wts = {
        "g_mix": row(g_mix), "g_cross": row(g_cross), "g_ffn": row(g_ffn), "g_final": g_final.reshape(1, d),
        "g_mla_q": row(g_mla_q), "g_mla_kv": row(g_mla_kv),
        "fox": [_prep_fox(w_fox_in[j], b_fox_f[j], w_fox_out[j], fox_heads) for j in range(w_fox_in.shape[0])],
        "mla": [_prep_mla(w_mla_a[j], w_mla_qb[j], w_mla_kvb[j], w_mla_out[j], q_lora, kv_lora, mla_heads)
                for j in range(w_mla_a.shape[0])],
        "cross": [(w_x_q[i].astype(BF16), w_x_o[i].astype(BF16)) for i in range(depth)],
        "ffn": [(gu[i, :, :ff], gu[i, :, ff:], w_ffn_down[i].astype(BF16)) for i in range(depth)],
        "fox_heads": fox_heads, "x_heads": x_heads, "mla_heads": mla_heads,
        "mla_scale": (HEAD_DIM + MLA_ROPE) ** -0.5,
    }
    bp, n_mem, _ = mem_prompt.shape
    mk, mv = _memory_kv(mem_prompt.reshape(bp * n_mem, d), g_mem.reshape(depth, 1, d), w_x_kv.astype(BF16))
    xw = mk.shape[-1]
    mk = mk.reshape(depth, bp, n_mem, xw)
    mv = mv.reshape(depth, bp, n_mem, xw)
    y_p, fk_p, fv_p, fl_p, mc_p, mr_p = _trunk_prompt(x_prompt, mk, mv, wts)
    past_len = cache_fox_k.shape[2]
    pos_s = past_len + jnp.arange(x_sample.shape[1], dtype=jnp.int32)
    bs = x_sample.shape[0]
    y_s, fk_s, fv_s, fl_s, mc_s, mr_s = _trunk_sample(
        x_sample, pos_s, (cache_fox_k, cache_fox_v, cache_fox_logf), (cache_mla_ckv, cache_mla_krope),
        cache_mem_k.reshape(depth, bs, n_mem, xw), cache_mem_v.reshape(depth, bs, n_mem, xw), wts)
    dh = xw // x_heads
    return (y_p, y_s, fk_p, fv_p, fl_p, mc_p, mr_p,
            mk.reshape(depth, bp, n_mem, x_heads, dh), mv.reshape(depth, bp, n_mem, x_heads, dh),
            fk_s, fv_s, fl_s, mc_s, mr_s)
```

```python
import functools

import jax
import jax.numpy as jnp
from jax import lax
from jax.experimental import pallas as pl
from jax.experimental.pallas import tpu as pltpu

EPS = 1e-6
CHUNK = 64
ROPE_THETA = 10000.0
LANE = 128
HEAD_DIM = 64
PAIR = 2
MLA_ROPE = 32
NEG = -1e30
LOG2E = 1.4426950408889634
VMEM_LIMIT = 56 * 1024 * 1024
BF16 = jnp.bfloat16
F32 = jnp.float32


def _dot(a, b):
    return jnp.dot(a, b, preferred_element_type=F32)


def _dot_nt(a, b):
    return lax.dot_general(a, b, (((1,), (1,)), ((), ())), preferred_element_type=F32)


def _rms(x):
    return x * lax.rsqrt(jnp.mean(x * x, axis=-1, keepdims=True) + EPS)


def _params(*sem):
    return pltpu.CompilerParams(dimension_semantics=sem, vmem_limit_bytes=VMEM_LIMIT)


def _row_tile(m, cap=512):
    t = min(m, cap)
    assert m % t == 0
    return t


def _full(shape):
    return pl.BlockSpec(shape, lambda *_: (0,) * len(shape), pipeline_mode=pl.Buffered(1))


def _log_sigmoid(f):
    return jnp.minimum(f, 0.0) - jnp.log1p(jnp.exp(-jnp.abs(f)))


def _split3(x):
    hi = x.astype(BF16)
    r = x - hi.astype(F32)
    mid = r.astype(BF16)
    lo = (r - mid.astype(F32)).astype(BF16)
    return hi, mid, lo


def _layer_map(imap, layer, *idx):
    return (layer,) + tuple(imap(*idx))


def _stacked_call(body, *, grid, in_specs, args, outs, layer, prev, sem, name, scratch=()):
    out_specs, out_shape, stacked = [], [], []
    for k, (shape, dtype, blk, imap, n_layers) in enumerate(outs):
        if n_layers is None:
            out_specs.append(pl.BlockSpec(blk, imap))
            out_shape.append(jax.ShapeDtypeStruct(shape, dtype))
        else:
            out_specs.append(pl.BlockSpec((None,) + tuple(blk), functools.partial(_layer_map, imap, layer)))
            out_shape.append(jax.ShapeDtypeStruct((n_layers,) + tuple(shape), dtype))
            stacked.append(k)
    aliases = {}
    in_specs = list(in_specs)
    args = list(args)
    if prev is not None:
        for k, arr in zip(stacked, prev):
            aliases[len(args)] = k
            in_specs.append(pl.BlockSpec(memory_space=pl.ANY))
            args.append(arr)
    n_alias = len(aliases)

    def wrapped(*refs):
        n_in = len(args) - n_alias
        body(*refs[:n_in], *refs[n_in + n_alias:])

    res = pl.pallas_call(
        wrapped, grid=grid, in_specs=in_specs, out_specs=out_specs, out_shape=out_shape,
        input_output_aliases=aliases, scratch_shapes=list(scratch),
        compiler_params=_params(*sem), name=name,
    )(*args)
    return res, [res[k] for k in stacked]


def _fox_in_body(x_ref, g_ref, wq, wk, wv, wg, wf, bf_ref, q_out, k_out, v_out, gate_out, lf_out, *, scale, heads):
    h = (_rms(x_ref[...]) * g_ref[...]).astype(BF16)
    q_out[...] = (_dot(h, wq[...]) * scale).astype(BF16)
    k_out[...] = _dot(h, wk[...])
    v_out[...] = _dot(h, wv[...])
    gate_out[...] = jax.nn.sigmoid(_dot(h, wg[...]))
    lf_out[...] = _log_sigmoid(_dot(h, wf[...]) + bf_ref[...])[:, :heads]


def _fox_in(x, g, wq, wk, wv, wg, wf, bf, heads):
    m, d = x.shape
    w = wq.shape[1]
    tm = _row_tile(m)
    row = lambda n: pl.BlockSpec((tm, n), lambda i: (i, 0))
    return pl.pallas_call(
        functools.partial(_fox_in_body, scale=HEAD_DIM ** -0.5, heads=heads),
        grid=(m // tm,),
        in_specs=[row(d), _full((1, d)), _full((d, w)), _full((d, w)), _full((d, w)), _full((d, w)),
                  _full((d, LANE)), _full((1, LANE))],
        out_specs=[row(w), row(w), row(w), row(w), row(heads)],
        out_shape=[jax.ShapeDtypeStruct((m, w), BF16), jax.ShapeDtypeStruct((m, w), F32),
                   jax.ShapeDtypeStruct((m, w), F32), jax.ShapeDtypeStruct((m, w), F32),
                   jax.ShapeDtypeStruct((m, heads), F32)],
        compiler_params=_params("parallel"),
        name="fox_in",
    )(x, g, wq, wk, wv, wg, wf, bf)


def _fox_in_t_body(x_ref, g_ref, wqt, wkt, wvt, wk, wg, wft, bf_ref,
                   qt_out, k_out, kt_out, vt_out, gate_out, lft_out, *, scale):
    h = (_rms(x_ref[...]) * g_ref[...]).astype(BF16)
    qt_out[...] = (_dot_nt(wqt[...], h) * scale).astype(BF16)
    k_out[...] = _dot(h, wk[...]).astype(BF16)
    kt_out[...] = _dot_nt(wkt[...], h)
    vt_out[...] = _dot_nt(wvt[...], h)
    gate_out[...] = jax.nn.sigmoid(_dot(h, wg[...]))
    lft_out[...] = _log_sigmoid(_dot_nt(wft[...], h) + bf_ref[...])


def _fox_in_t(x, g, wts, layer, n_layers, prev):
    wqt, wkt, wvt, wk, wg, wft, bf = wts
    b, t, d = x.shape
    w = wk.shape[1]
    heads = wft.shape[0]
    tm = _row_tile(t)
    rows = lambda bi, i: (bi, i, 0)
    cols = lambda bi, i: (bi, 0, i)
    res, stacked = _stacked_call(
        functools.partial(_fox_in_t_body, scale=HEAD_DIM ** -0.5 * LOG2E),
        grid=(b, t // tm),
        in_specs=[pl.BlockSpec((None, tm, d), rows), _full((1, d)), _full((w, d)), _full((w, d)), _full((w, d)),
                  _full((d, w)), _full((d, w)), _full((heads, d)), _full((heads, 1))],
        args=[x, g, wqt, wkt, wvt, wk, wg, wft, bf],
        outs=[((b, w, t), BF16, (None, w, tm), cols, None),
              ((b, t, w), BF16, (None, tm, w), rows, None),
              ((b, w, t), F32, (None, w, tm), cols, n_layers),
              ((b, w, t), F32, (None, w, tm), cols, n_layers),
              ((b, t, w), F32, (None, tm, w), rows, None),
              ((b, heads, t), F32, (None, heads, tm), cols, n_layers)],
        layer=layer, prev=prev, sem=("parallel", "parallel"), name="fox_in_t")
    qt, k, _, _, gate, _ = res
    return qt, k, gate, stacked


CUM_BLOCK = 256


def _cumsum_lanes_body(lf_ref, out_ref):
    h, t = lf_ref.shape
    r = lax.broadcasted_iota(jnp.int32, (CUM_BLOCK, CUM_BLOCK), 0)
    c = lax.broadcasted_iota(jnp.int32, (CUM_BLOCK, CUM_BLOCK), 1)
    tri = jnp.where(r <= c, 1.0, 0.0).astype(BF16)
    carry = jnp.zeros((h, 1), F32)
    for b in range(t // CUM_BLOCK):
        hi, mid, lo = _split3(lf_ref[:, b * CUM_BLOCK:(b + 1) * CUM_BLOCK])
        out = _dot(hi, tri) + _dot(mid, tri) + _dot(lo, tri) + carry
        out_ref[:, b * CUM_BLOCK:(b + 1) * CUM_BLOCK] = out
        carry = out[:, CUM_BLOCK - 1:CUM_BLOCK]


def _cumsum_lanes(lf, layer=None):
    b, h, t = lf.shape[-3:]
    assert t % CUM_BLOCK == 0
    if layer is None:
        spec = pl.BlockSpec((None, h, t), lambda i: (i, 0, 0))
    else:
        spec = pl.BlockSpec((None, None, h, t), lambda i: (layer, i, 0, 0))
    return pl.pallas_call(
        _cumsum_lanes_body, grid=(b,),
        in_specs=[spec],
        out_specs=pl.BlockSpec((None, h, t), lambda i: (i, 0, 0)),
        out_shape=jax.ShapeDtypeStruct((b, h, t), F32),
        compiler_params=_params("parallel"), name="logf_cumsum_t",
    )(lf)


N_PIECE = 3
TRIP_BLOCKS = 4


def _attn_prompt_body(*refs, fox, tq):
    if fox:
        qt_ref, k_ref, vt_ref, crow_ref, ccol_ref, gate_ref, o_ref, kx, vb = refs
    else:
        qnt_ref, qrt_ref, kn_ref, krd_ref, vt_ref, o_ref, kx = refs
        vb = vt_ref
    hp = pl.program_id(1)
    qi = pl.program_id(2)
    tk = tq
    t = kx.shape[0]
    ones_lo = PAIR * N_PIECE

    @pl.when(qi == 0)
    def _():
        if fox:
            kx[:, :LANE] = k_ref[...]
            vb[...] = vt_ref[...].astype(BF16)
            heads = ccol_ref.shape[1]
            hrow = lax.broadcasted_iota(jnp.int32, (heads, LANE), 0)
            lcol = lax.broadcasted_iota(jnp.int32, (heads, LANE), 1)
            ext = jnp.zeros((t, LANE), F32)
            for i, piece in enumerate(_split3(ccol_ref[...] * LOG2E)):
                hit = ((hrow == PAIR * hp) & (lcol == i)) | ((hrow == PAIR * hp + 1) & (lcol == N_PIECE + i))
                ext = ext + _dot(piece, jnp.where(hit, 1.0, 0.0).astype(BF16))
            lane = lax.broadcasted_iota(jnp.int32, (t, LANE), 1)
            ext = jnp.where((lane >= ones_lo) & (lane < ones_lo + N_PIECE), 1.0, ext)
            kx[:, LANE:] = ext.astype(BF16)
        else:
            kx[:, :LANE] = kn_ref[...]
            kx[:, LANE:] = krd_ref[...]

    sub = lax.broadcasted_iota(jnp.int32, (LANE, tq), 0)
    zero = jnp.zeros((), BF16)
    qs = []
    for e in range(PAIR):
        in_head = (sub >= e * HEAD_DIM) & (sub < (e + 1) * HEAD_DIM)
        if fox:
            ext = jnp.where((sub >= N_PIECE * e) & (sub < N_PIECE * (e + 1)), -1.0, 0.0)
            for i, piece in enumerate(_split3(crow_ref[e:e + 1, :] * LOG2E)):
                ext = jnp.where(sub == ones_lo + i, piece.astype(F32), ext)
            qs.append(jnp.concatenate([jnp.where(in_head, qt_ref[...], zero), ext.astype(BF16)], axis=0))
        else:
            in_rope = (sub >= e * MLA_ROPE) & (sub < (e + 1) * MLA_ROPE)
            qs.append(jnp.concatenate([jnp.where(in_head, qnt_ref[...], zero),
                                       jnp.where(in_rope, qrt_ref[...], zero)], axis=0))

    key = lax.broadcasted_iota(jnp.int32, (tk, tq), 0)
    qry = lax.broadcasted_iota(jnp.int32, (tk, tq), 1)
    visible = (key <= qry) if fox else ((key // CHUNK) <= (qry // CHUNK))

    def blocks(j0, carry, nblk, diagonal_last):
        starts = [pl.multiple_of((j0 + u) * tk, tk) for u in range(nblk)]
        scores = [[_dot(kx[pl.ds(st, tk), :], qs[e]) for e in range(PAIR)] for st in starts]
        carry = list(carry)
        for u, st in enumerate(starts):
            for e in range(PAIR):
                m, l, acc = carry[3 * e:3 * e + 3]
                s = scores[u][e]
                if diagonal_last and u == nblk - 1:
                    s = jnp.where(visible, s, NEG)
                m_new = jnp.maximum(m, jnp.max(s, axis=0, keepdims=True))
                alpha = jnp.exp2(m - m_new)
                p = jnp.exp2(s - m_new)
                l_new = alpha * l + jnp.sum(p, axis=0, keepdims=True)
                v_j = vb[e * HEAD_DIM:(e + 1) * HEAD_DIM, pl.ds(st, tk)]
                carry[3 * e:3 * e + 3] = [m_new, l_new, alpha * acc + _dot(v_j, p.astype(BF16))]
        return tuple(carry)

    init = (jnp.full((1, tq), NEG, F32), jnp.zeros((1, tq), F32), jnp.zeros((HEAD_DIM, tq), F32)) * PAIR
    main = qi // TRIP_BLOCKS
    left = qi - main * TRIP_BLOCKS
    carry = lax.fori_loop(0, main, lambda i, c: blocks(TRIP_BLOCKS * i, c, TRIP_BLOCKS, False), init)
    for n in range(TRIP_BLOCKS):
        carry = lax.fori_loop(0, (left == n).astype(jnp.int32),
                              lambda _, c, n=n: blocks(qi - n, c, n + 1, True), carry)
    o = jnp.concatenate([carry[2] / carry[1], carry[5] / carry[4]], axis=0).T
    if fox:
        o = o * gate_ref[...]
    o_ref[...] = o.astype(BF16)


def _fox_attn_prompt(qt, k, vt_all, layer, cum_t, gate, tq=512):
    b, w, t = qt.shape
    heads = cum_t.shape[1]
    tq = min(tq, t)
    npair = w // LANE
    rowblk = pl.BlockSpec((None, tq, LANE), lambda bi, hp, qi: (bi, qi, hp))
    return pl.pallas_call(
        functools.partial(_attn_prompt_body, fox=True, tq=tq),
        grid=(b, npair, t // tq),
        in_specs=[pl.BlockSpec((None, LANE, tq), lambda bi, hp, qi: (bi, hp, qi)),
                  pl.BlockSpec((None, t, LANE), lambda bi, hp, qi: (bi, 0, hp)),
                  pl.BlockSpec((None, None, LANE, t), lambda bi, hp, qi: (layer, bi, hp, 0)),
                  pl.BlockSpec((None, None, PAIR, tq), lambda bi, hp, qi: (bi, hp, 0, qi)),
                  pl.BlockSpec((None, t, heads), lambda bi, hp, qi: (bi, 0, 0)),
                  rowblk],
        out_specs=rowblk,
        out_shape=jax.ShapeDtypeStruct((b, t, w), BF16),
        scratch_shapes=[pltpu.VMEM((t, 2 * LANE), BF16), pltpu.VMEM((LANE, t), BF16)],
        compiler_params=_params("parallel", "parallel", "arbitrary"),
        name="fox_attn_prompt",
    )(qt, k, vt_all, cum_t.reshape(b, npair, PAIR, t), jnp.swapaxes(cum_t, 1, 2), gate)


def _mla_attn_prompt(qnt, qrt, kn, krd, vt, tq=512):
    b, w, t = qnt.shape
    tq = min(tq, t)
    npair = w // LANE
    qblk = pl.BlockSpec((None, LANE, tq), lambda bi, hp, qi: (bi, hp, qi))
    rowblk = pl.BlockSpec((None, tq, LANE), lambda bi, hp, qi: (bi, qi, hp))
    return pl.pallas_call(
        functools.partial(_attn_prompt_body, fox=False, tq=tq),
        grid=(b, npair, t // tq),
        in_specs=[qblk, qblk,
                  pl.BlockSpec((None, t, LANE), lambda bi, hp, qi: (bi, 0, hp)),
                  pl.BlockSpec((None, t, LANE), lambda bi, hp, qi: (bi, 0, 0)),
                  pl.BlockSpec((None, LANE, t), lambda bi, hp, qi: (bi, hp, 0))],
        out_specs=rowblk,
        out_shape=jax.ShapeDtypeStruct((b, t, w), BF16),
        scratch_shapes=[pltpu.VMEM((t, 2 * LANE), BF16)],
        compiler_params=_params("parallel", "parallel", "arbitrary"),
        name="mla_attn_prompt",
    )(qnt, qrt, kn, krd, vt)


def _fox_attn_sample_body(q_ref, kt_ref, vt_ref, kn_ref, vn_ref, cq_ref, ck_ref, gate_ref, o_ref):
    hp = pl.program_id(1)
    t = q_ref.shape[0]
    p = kt_ref.shape[1]
    rows = PAIR * t
    lane = lax.broadcasted_iota(jnp.int32, (rows, LANE), 1)
    row = lax.broadcasted_iota(jnp.int32, (rows, LANE), 0)
    q2 = jnp.concatenate([q_ref[...]] * PAIR, axis=0)
    q2 = jnp.where((lane // HEAD_DIM) == (row // t), q2, jnp.zeros((), BF16))
    hl = lax.broadcasted_iota(jnp.int32, cq_ref.shape, 1)
    cq = jnp.concatenate([jnp.sum(jnp.where(hl == PAIR * hp + e, cq_ref[...], 0.0), axis=1, keepdims=True)
                          for e in range(PAIR)], axis=0)
    first_p = lax.broadcasted_iota(jnp.int32, (rows, p), 0) < t
    rn = lax.broadcasted_iota(jnp.int32, (rows, t), 0)
    cn = lax.broadcasted_iota(jnp.int32, (rows, t), 1)
    s_p = _dot(q2, kt_ref[...].astype(BF16)) + cq - jnp.where(first_p, ck_ref[0:1, :p], ck_ref[1:2, :p])
    s_n = _dot_nt(q2, kn_ref[...].astype(BF16)) + cq - jnp.where(rn < t, ck_ref[0:1, p:], ck_ref[1:2, p:])
    s_n = jnp.where(cn <= lax.rem(rn, t), s_n, NEG)
    m = jnp.maximum(jnp.max(s_p, axis=-1, keepdims=True), jnp.max(s_n, axis=-1, keepdims=True))
    p_p = jnp.exp(s_p - m)
    p_n = jnp.exp(s_n - m)
    l = jnp.sum(p_p, axis=-1, keepdims=True) + jnp.sum(p_n, axis=-1, keepdims=True)
    o = (_dot_nt(p_p.astype(BF16), vt_ref[...].astype(BF16)) + _dot(p_n.astype(BF16), vn_ref[...].astype(BF16))) / l
    o = jnp.where(lax.broadcasted_iota(jnp.int32, (t, LANE), 1) < HEAD_DIM, o[:t], o[t:]) * gate_ref[...]
    o_ref[...] = o.astype(BF16)


def _fox_attn_sample(q, kt_cache, vt_cache, layer, k_new, v_new, cum_t, gate):
    b, t, w = q.shape
    p = kt_cache.shape[-1]
    heads = cum_t.shape[1]
    npair = w // LANE
    new = pl.BlockSpec((None, t, LANE), lambda bi, hp: (bi, 0, hp))
    past = pl.BlockSpec((None, None, LANE, p), lambda bi, hp: (layer, bi, hp, 0))
    return pl.pallas_call(
        _fox_attn_sample_body,
        grid=(b, npair),
        in_specs=[new, past, past, new, new,
                  pl.BlockSpec((None, t, heads), lambda bi, hp: (bi, 0, 0)),
                  pl.BlockSpec((None, None, PAIR, p + t), lambda bi, hp: (bi, hp, 0, 0)),
                  new],
        out_specs=new,
        out_shape=jax.ShapeDtypeStruct((b, t, w), BF16),
        compiler_params=_params("parallel", "parallel"),
        name="fox_attn_sample",
    )(q, kt_cache, vt_cache, k_new, v_new, jnp.swapaxes(cum_t[:, :, p:], 1, 2),
      cum_t.reshape(b, npair, PAIR, p + t), gate)


def _mla_attn_sample_body(qn_ref, qr_ref, cp_ref, krp_ref, cn_ref, krn_ref, wkn_ref, wv_ref, o_ref, *, past_len):
    t, w = qn_ref.shape
    p = cp_ref.shape[0]
    heads = w // HEAD_DIM
    rows = heads * t
    lane = lax.broadcasted_iota(jnp.int32, (rows, w), 1)
    row = lax.broadcasted_iota(jnp.int32, (rows, w), 0)
    own = (lane // HEAD_DIM) == (row // t)
    q_wide = jnp.where(own, jnp.concatenate([qn_ref[...]] * heads, axis=0), jnp.zeros((), BF16))
    q_lat = _dot_nt(q_wide, wkn_ref[...]).astype(BF16)
    qr = qr_ref[...]
    c_p = cp_ref[...].astype(BF16)
    c_n = cn_ref[...].astype(BF16)
    s_p = _dot_nt(q_lat, c_p) + _dot(qr, krp_ref[...].astype(BF16))
    s_n = _dot_nt(q_lat, c_n) + _dot_nt(qr, krn_ref[...].astype(BF16))
    q_chunk_p = (past_len + lax.rem(lax.broadcasted_iota(jnp.int32, (rows, p), 0), t)) // CHUNK
    s_p = jnp.where((lax.broadcasted_iota(jnp.int32, (rows, p), 1) // CHUNK) <= q_chunk_p, s_p, NEG)
    q_chunk_n = (past_len + lax.rem(lax.broadcasted_iota(jnp.int32, (rows, t), 0), t)) // CHUNK
    s_n = jnp.where(((past_len + lax.broadcasted_iota(jnp.int32, (rows, t), 1)) // CHUNK) <= q_chunk_n, s_n, NEG)
    m = jnp.maximum(jnp.max(s_p, axis=-1, keepdims=True), jnp.max(s_n, axis=-1, keepdims=True))
    p_p = jnp.exp(s_p - m)
    p_n = jnp.exp(s_n - m)
    l = jnp.sum(p_p, axis=-1, keepdims=True) + jnp.sum(p_n, axis=-1, keepdims=True)
    o_lat = (_dot(p_p.astype(BF16), c_p) + _dot(p_n.astype(BF16), c_n)) / l
    o_wide = jnp.where(own, _dot(o_lat.astype(BF16), wv_ref[...]), 0.0)
    o = o_wide[:t]
    for h in range(1, heads):
        o = o + o_wide[h * t:(h + 1) * t]
    o_ref[...] = o.astype(BF16)


def _mla_attn_sample(qn, qr_rows, ckv_cache, krt_cache, layer, ckv_new, kr_new, wkn, wv):
    b, t, w = qn.shape
    p, c = ckv_cache.shape[2:]
    rows = qr_rows.shape[1]
    blk = lambda *s: pl.BlockSpec((None,) + s, lambda bi: (bi,) + (0,) * len(s))
    past = lambda *s: pl.BlockSpec((None, None) + s, lambda bi: (layer, bi) + (0,) * len(s))
    return pl.pallas_call(
        functools.partial(_mla_attn_sample_body, past_len=p),
        grid=(b,),
        in_specs=[blk(t, w), blk(rows, MLA_ROPE), past(p, c), past(MLA_ROPE, p), blk(t, c), blk(t, MLA_ROPE),
                  _full(wkn.shape), _full(wv.shape)],
        out_specs=blk(t, w),
        out_shape=jax.ShapeDtypeStruct((b, t, w), BF16),
        compiler_params=_params("parallel"),
        name="mla_attn_sample",
    )(qn, qr_rows, ckv_cache, krt_cache, ckv_new, kr_new, wkn, wv)


def _mla_in_body(x_ref, g_ref, w1, gq_ref, gkv_ref, w2, cos_ref, sin_ref, cosq_ref, sinq_ref,
                 qn_out, qr_out, ckv_out, kr_out, *, q_lora, kv_lora, scale):
    h = (_rms(x_ref[...]) * g_ref[...]).astype(BF16)
    a = _dot(h, w1[...])
    c_q = (_rms(a[:, :q_lora]) * gq_ref[...]).astype(BF16)
    ckv_out[...] = _rms(a[:, q_lora:q_lora + kv_lora]) * gkv_ref[...]
    o = q_lora + kv_lora
    krd = a[:, o:o + LANE] * cos_ref[...] + a[:, o + LANE:o + 2 * LANE] * sin_ref[...]
    kr_out[...] = krd[:, :MLA_ROPE]
    w = qn_out.shape[1]
    r = qr_out.shape[1]
    qn_out[...] = (_dot(c_q, w2[:, :w]) * scale).astype(BF16)
    qr = _dot(c_q, w2[:, w:w + r]) * cosq_ref[...] + _dot(c_q, w2[:, w + r:w + 2 * r]) * sinq_ref[...]
    qr_out[...] = (qr * scale).astype(BF16)


def _mla_in(x, g, w1, gq, gkv, w2, cos, sin, heads, scale):
    m, d = x.shape
    q_lora = gq.shape[1]
    kv_lora = gkv.shape[1]
    w = heads * HEAD_DIM
    r = heads * MLA_ROPE
    tm = _row_tile(m)
    t = cos.shape[0]
    assert tm % t == 0
    per_head = lambda tab: jnp.tile(tab[:, :MLA_ROPE], (tm // t, heads))
    cosq, sinq = per_head(cos), per_head(sin)
    cos, sin = jnp.tile(cos, (tm // t, 1)), jnp.tile(sin, (tm // t, 1))
    row = lambda n: pl.BlockSpec((tm, n), lambda i: (i, 0))
    return pl.pallas_call(
        functools.partial(_mla_in_body, q_lora=q_lora, kv_lora=kv_lora, scale=scale),
        grid=(m // tm,),
        in_specs=[row(d), _full((1, d)), _full(w1.shape), _full((1, q_lora)), _full((1, kv_lora)),
                  _full(w2.shape), _full((tm, LANE)), _full((tm, LANE)), _full((tm, r)), _full((tm, r))],
        out_specs=[row(w), row(r), row(kv_lora), row(MLA_ROPE)],
        out_shape=[jax.ShapeDtypeStruct((m, w), BF16), jax.ShapeDtypeStruct((m, r), BF16),
                   jax.ShapeDtypeStruct((m, kv_lora), F32), jax.ShapeDtypeStruct((m, MLA_ROPE), F32)],
        compiler_params=_params("parallel"),
        name="mla_in",
    )(x, g, w1, gq, gkv, w2, cos, sin, cosq, sinq)


def _mla_in_t_body(x_ref, g_ref, w1, gq_ref, gkv_ref, wqnt, wqat, wqbt, wkn, wvt, wkrt,
                   cos_ref, sin_ref, cost_ref, sint_ref,
                   qnt_out, qrt_out, kn_out, krd_out, vt_out, ckv_out, krt_out, *, q_lora, kv_lora, scale):
    h = (_rms(x_ref[...]) * g_ref[...]).astype(BF16)
    a = _dot(h, w1[...])
    c_q = (_rms(a[:, :q_lora]) * gq_ref[...]).astype(BF16)
    c_kv = _rms(a[:, q_lora:q_lora + kv_lora]) * gkv_ref[...]
    ckv_out[...] = c_kv
    o = q_lora + kv_lora
    krd_out[...] = (a[:, o:o + LANE] * cos_ref[...] + a[:, o + LANE:o + 2 * LANE] * sin_ref[...]).astype(BF16)
    cost = cost_ref[...]
    sint = sint_ref[...]
    kab = _dot_nt(wkrt[...], h)
    krt_out[...] = kab[:MLA_ROPE] * cost[:MLA_ROPE] + kab[MLA_ROPE:] * sint[:MLA_ROPE]
    c_kv = c_kv.astype(BF16)
    kn_out[...] = _dot(c_kv, wkn[...]).astype(BF16)
    vt_out[...] = _dot_nt(wvt[...], c_kv).astype(BF16)
    qnt_out[...] = (_dot_nt(wqnt[...], c_q) * scale).astype(BF16)
    for p in range(qrt_out.shape[0] // LANE):
        rows = slice(p * LANE, (p + 1) * LANE)
        qa = _dot_nt(wqat[rows, :], c_q)
        qb = _dot_nt(wqbt[rows, :], c_q)
        qrt_out[rows, :] = ((qa * cost + qb * sint) * scale).astype(BF16)


def _mla_in_t(x, g, gq, gkv, wts, tables, scale, layer, n_layers, prev):
    w1, wqnt, wqat, wqbt, wkn, wvt, wkrt = wts
    cos, sin, cost, sint = tables
    b, t, d = x.shape
    q_lora = gq.shape[1]
    kv_lora = gkv.shape[1]
    w = wkn.shape[1]
    tm = _row_tile(t)
    rows = lambda bi, i: (bi, i, 0)
    cols = lambda bi, i: (bi, 0, i)
    res, stacked = _stacked_call(
        functools.partial(_mla_in_t_body, q_lora=q_lora, kv_lora=kv_lora, scale=scale),
        grid=(b, t // tm),
        in_specs=[pl.BlockSpec((None, tm, d), rows), _full((1, d)), _full(w1.shape), _full((1, q_lora)),
                  _full((1, kv_lora)), _full(wqnt.shape), _full(wqat.shape), _full(wqbt.shape), _full(wkn.shape),
                  _full(wvt.shape), _full(wkrt.shape),
                  pl.BlockSpec((tm, LANE), lambda bi, i: (i, 0)), pl.BlockSpec((tm, LANE), lambda bi, i: (i, 0)),
                  pl.BlockSpec((LANE, tm), lambda bi, i: (0, i)), pl.BlockSpec((LANE, tm), lambda bi, i: (0, i))],
        args=[x, g, w1, gq, gkv, wqnt, wqat, wqbt, wkn, wvt, wkrt, cos, sin, cost, sint],
        outs=[((b, w, t), BF16, (None, w, tm), cols, None),
              ((b, w, t), BF16, (None, w, tm), cols, None),
              ((b, t, w), BF16, (None, tm, w), rows, None),
              ((b, t, LANE), BF16, (None, tm, LANE), rows, None),
              ((b, w, t), BF16, (None, w, tm), cols, None),
              ((b, t, kv_lora), F32, (None, tm, kv_lora), rows, n_layers),
              ((b, MLA_ROPE, t), F32, (None, MLA_ROPE, tm), cols, n_layers)],
        layer=layer, prev=prev, sem=("parallel", "parallel"), name="mla_in_t")
    return res[:5], stacked


def _memory_kv_body(m_ref, g_ref, w_ref, k_out, v_out):
    h = (_rms(m_ref[...]) * g_ref[...]).astype(BF16)
    n = k_out.shape[-1]
    k_out[...] = _dot(h, w_ref[:, :n])
    v_out[...] = _dot(h, w_ref[:, n:])


def _memory_kv(mem, g_mem, w_kv):
    m, d = mem.shape
    depth, _, n2 = w_kv.shape
    n = n2 // 2
    tm = _row_tile(m)
    out = pl.BlockSpec((None, tm, n), lambda l, i: (l, i, 0))
    return pl.pallas_call(
        _memory_kv_body, grid=(depth, m // tm),
        in_specs=[pl.BlockSpec((tm, d), lambda l, i: (i, 0)),
                  pl.BlockSpec((None, 1, d), lambda l, i: (l, 0, 0)),
                  pl.BlockSpec((None, d, n2), lambda l, i: (l, 0, 0))],
        out_specs=[out, out],
        out_shape=[jax.ShapeDtypeStruct((depth, m, n), F32)] * 2,
        compiler_params=_params("parallel", "parallel"), name="memory_kv",
    )(mem, g_mem, w_kv)


FF_BLOCK = 256


def _layer_tail_body(o_ref, x_ref, wout, gc_ref, wq, mk_ref, mv_ref, wxo, gf_ref, wg, wu, wd, *rest, heads, final):
    if final:
        gl_ref, x_out, y_out, att = rest
    else:
        x_out, att = rest
    bb, tm, d = x_ref.shape
    x = x_ref[...].reshape(bb * tm, d) + _dot(o_ref[...].reshape(bb * tm, o_ref.shape[-1]), wout[...])
    h = (_rms(x) * gc_ref[...]).astype(BF16)
    dh = wq.shape[1] // heads
    q = (_dot(h, wq[...]) * dh ** -0.5).astype(BF16)
    for b in range(bb):
        for hd in range(heads):
            cols = slice(hd * dh, (hd + 1) * dh)
            s = _dot_nt(q[b * tm:(b + 1) * tm, cols], mk_ref[b, :, cols].astype(BF16))
            p = jnp.exp(s - jnp.max(s, axis=-1, keepdims=True))
            o = _dot(p.astype(BF16), mv_ref[b, :, cols].astype(BF16)) / jnp.sum(p, axis=-1, keepdims=True)
            att[b * tm:(b + 1) * tm, cols] = o.astype(BF16)
    x = x + _dot(att[...], wxo[...])
    h = (_rms(x) * gf_ref[...]).astype(BF16)
    for c in range(wg.shape[1] // FF_BLOCK):
        cols = slice(c * FF_BLOCK, (c + 1) * FF_BLOCK)
        gate = _dot(h, wg[:, cols])
        up = _dot(h, wu[:, cols])
        x = x + _dot((gate * jax.nn.sigmoid(gate) * up).astype(BF16), wd[cols, :])
    x_out[...] = x.reshape(bb, tm, d)
    if final:
        y_out[...] = (_rms(x) * gl_ref[...]).reshape(bb, tm, d)


def _layer_tail(o, x, w_out, g_cross, wq, mk, mv, layer, wxo, g_ffn, wg, wu, wd, heads, g_final=None):
    b, t, d = x.shape
    w = o.shape[-1]
    n, xw = mk.shape[2:]
    ff = wg.shape[1]
    assert ff % FF_BLOCK == 0
    tm = min(t, 512)
    bb = max(1, min(b, 64 // tm))
    blk = lambda c: pl.BlockSpec((bb, tm, c), lambda bi, ti: (bi, ti, 0))
    mblk = pl.BlockSpec((None, bb, n, xw), lambda bi, ti: (layer, bi, 0, 0))
    final = g_final is not None
    ins = [blk(w), blk(d), _full((w, d)), _full((1, d)), _full((d, xw)), mblk, mblk, _full((xw, d)),
           _full((1, d)), _full((d, ff)), _full((d, ff)), _full((ff, d))]
    args = [o, x, w_out, g_cross, wq, mk, mv, wxo, g_ffn, wg, wu, wd]
    if final:
        ins.append(_full((1, d)))
        args.append(g_final)
    out = jax.ShapeDtypeStruct((b, t, d), F32)
    return pl.pallas_call(
        functools.partial(_layer_tail_body, heads=heads, final=final),
        grid=(b // bb, t // tm),
        in_specs=ins,
        out_specs=[blk(d), blk(d)] if final else blk(d),
        out_shape=[out, out] if final else out,
        scratch_shapes=[pltpu.VMEM((bb * tm, xw), BF16)],
        compiler_params=_params("parallel", "parallel"), name="layer_tail",
    )(*args)


def _prep_fox(w_in, b_f, w_out, heads):
    width = w_out.shape[0]
    w_in = w_in.astype(BF16)
    wq, wk, wv, wf, wg = jnp.split(w_in, [width, 2 * width, 3 * width, 3 * width + heads], axis=1)
    rows = (wq, wk, wv, wg, jnp.pad(wf, ((0, 0), (0, LANE - heads))),
            jnp.pad(b_f, (0, LANE - heads)).reshape(1, LANE))
    cols = (wq.T, wk.T, wv.T, wk, wg, wf.T, b_f.reshape(heads, 1))
    return {"rows": rows, "cols": cols, "out": w_out.astype(BF16)}


def _prep_mla(w_a, w_qb, w_kvb, w_out, q_lora, kv_lora, heads):
    half = MLA_ROPE // 2
    w_a = w_a.astype(BF16)
    d = w_a.shape[0]
    x1 = w_a[:, q_lora + kv_lora:q_lora + kv_lora + half]
    x2 = w_a[:, q_lora + kv_lora + half:]
    zeros = jnp.zeros((d, LANE - 2 * MLA_ROPE), BF16)
    w1 = jnp.concatenate([w_a[:, :q_lora + kv_lora], x1, x2, x1, x2, zeros, x2, x1, x2, x1, zeros], axis=1)
    wkrt = jnp.concatenate([x1, x2, x2, x1], axis=1).T
    qb = w_qb.astype(BF16).reshape(q_lora, heads, HEAD_DIM + MLA_ROPE)
    wqn = qb[:, :, :HEAD_DIM].reshape(q_lora, heads * HEAD_DIM)
    rope = qb[:, :, HEAD_DIM:]
    swapped = jnp.concatenate([rope[..., half:], rope[..., :half]], axis=-1)

    def pack(r):
        r = r.reshape(q_lora, heads // PAIR, PAIR * MLA_ROPE)
        return jnp.pad(r, ((0, 0), (0, 0), (0, LANE - PAIR * MLA_ROPE))).reshape(q_lora, -1)

    wqa, wqb = pack(rope), pack(swapped)
    kvb = w_kvb.astype(BF16).reshape(kv_lora, heads, 2 * HEAD_DIM)
    wkn = kvb[:, :, :HEAD_DIM].reshape(kv_lora, heads * HEAD_DIM)
    wv = kvb[:, :, HEAD_DIM:].reshape(kv_lora, heads * HEAD_DIM)
    flat = lambda r: r.reshape(q_lora, heads * MLA_ROPE)
    rows = (w1, jnp.concatenate([wqn, flat(rope), flat(swapped)], axis=1), wkn, wv)
    cols = (w1, wqn.T, wqa.T, wqb.T, wkn, wv.T, wkrt)
    return {"rows": rows, "cols": cols, "out": w_out.astype(BF16)}


def _rope_tables(pos):
    half = MLA_ROPE // 2
    inv = ROPE_THETA ** (-jnp.arange(half, dtype=F32) / half)
    ang = pos.astype(F32)[:, None] * inv[None, :]
    cos, sin = jnp.cos(ang), jnp.sin(ang)
    z = jnp.zeros((pos.shape[0], LANE - 2 * MLA_ROPE), F32)
    return (jnp.concatenate([cos, cos, cos, cos, z], axis=1),
            jnp.concatenate([-sin, sin, -sin, sin, z], axis=1))


def _after_mixer(x, o, w_out, i, mem_k, mem_v, wts):
    wxq, wxo = wts["cross"][i]
    wgt, wup, wdn = wts["ffn"][i]
    last = i == len(wts["ffn"]) - 1
    res = _layer_tail(o, x, w_out, wts["g_cross"][i], wxq, mem_k, mem_v, i, wxo, wts["g_ffn"][i], wgt, wup, wdn,
                      wts["x_heads"], wts["g_final"] if last else None)
    return res if last else (res, None)


def _trunk_prompt(x, mem_k, mem_v, wts):
    b, t, d = x.shape
    m = b * t
    depth = len(wts["ffn"])
    n_fox, n_mla = len(wts["fox"]), len(wts["mla"])
    cos, sin = _rope_tables(jnp.arange(t, dtype=jnp.int32))
    tables = (cos, sin, cos.T, sin.T)
    fox_state = mla_state = None
    for i in range(depth):
        j = i // 2
        if i % 2 == 0:
            wf = wts["fox"][j]
            qt, k, gate, fox_state = _fox_in_t(x, wts["g_mix"][i], wf["cols"], j, n_fox, fox_state)
            cum_t = _cumsum_lanes(fox_state[2], j)
            o = _fox_attn_prompt(qt, k, fox_state[1], j, cum_t, gate)
        else:
            wf = wts["mla"][j]
            (qnt, qrt, kn, krd, vt), mla_state = _mla_in_t(
                x, wts["g_mix"][i], wts["g_mla_q"][j], wts["g_mla_kv"][j], wf["cols"], tables,
                wts["mla_scale"] * LOG2E, j, n_mla, mla_state)
            o = _mla_attn_prompt(qnt, qrt, kn, krd, vt)
        x, y = _after_mixer(x, o, wf["out"], i, mem_k, mem_v, wts)
    kt, vt, lft = fox_state
    ckv, krt = mla_state
    heads = lft.shape[2]
    unfold = lambda a: jnp.transpose(a.reshape(n_fox, b, heads, a.shape[2] // heads, t), (0, 1, 4, 2, 3))
    return (y.reshape(b, t, d), unfold(kt), unfold(vt), jnp.swapaxes(lft, 2, 3), ckv, jnp.swapaxes(krt, 2, 3))


def _trunk_sample(x, pos, fox_past, mla_past, mem_k, mem_v, wts):
    b, t, d = x.shape
    m = b * t
    depth = len(wts["ffn"])
    cos, sin = _rope_tables(pos)
    heads = wts["fox_heads"]
    mla_heads = wts["mla_heads"]
    fox_k, fox_v, fox_lf, mla_c, mla_r = [], [], [], [], []
    r3 = lambda a: a.reshape(b, t, a.shape[-1])
    n_fox, _, p, _, hd = fox_past[0].shape
    kt_cache = jnp.transpose(fox_past[0], (0, 1, 3, 4, 2)).reshape(n_fox, b, heads * hd, p)
    vt_cache = jnp.transpose(fox_past[1], (0, 1, 3, 4, 2)).reshape(n_fox, b, heads * hd, p)
    lft_cache = jnp.swapaxes(fox_past[2], 2, 3)
    krt_cache = jnp.swapaxes(mla_past[1], 2, 3)
    total = p + t
    pad = jnp.zeros((b, heads, -(-total // CUM_BLOCK) * CUM_BLOCK - total), F32)
    for i in range(depth):
        j = i // 2
        x2 = x.reshape(m, d)
        if i % 2 == 0:
            wf = wts["fox"][j]
            q, k, v, gate, lf = _fox_in(x2, wts["g_mix"][i], *wf["rows"], heads)
            lf_all = jnp.concatenate([lft_cache[j], jnp.swapaxes(r3(lf), 1, 2), pad], axis=2)
            cum_t = _cumsum_lanes(lf_all)[:, :, :total]
            o = _fox_attn_sample(r3(q), kt_cache, vt_cache, j, r3(k), r3(v), cum_t, r3(gate))
            fox_k.append(k.reshape(b, t, heads, hd))
            fox_v.append(v.reshape(b, t, heads, hd))
            fox_lf.append(lf.reshape(b, t, heads))
        else:
            wf = wts["mla"][j]
            w1, w2, wkn, wv = wf["rows"]
            qn, qr, ckv, kr = _mla_in(x2, wts["g_mix"][i], w1, wts["g_mla_q"][j], wts["g_mla_kv"][j],
                                      w2, cos, sin, mla_heads, wts["mla_scale"])
            qr_rows = jnp.swapaxes(qr.reshape(b, t, mla_heads, MLA_ROPE), 1, 2).reshape(b, mla_heads * t, MLA_ROPE)
            o = _mla_attn_sample(r3(qn), qr_rows, mla_past[0], krt_cache, j, r3(ckv), r3(kr), wkn, wv)
            mla_c.append(r3(ckv))
            mla_r.append(r3(kr))
        x, y = _after_mixer(x, o, wf["out"], i, mem_k, mem_v, wts)
    return (y, jnp.stack(fox_k), jnp.stack(fox_v), jnp.stack(fox_lf),
            jnp.stack(mla_c), jnp.stack(mla_r))


def kernel(x_prompt, x_sample, mem_prompt, cache_fox_k, cache_fox_v, cache_fox_logf, cache_mla_ckv, cache_mla_krope, cache_mem_k, cache_mem_v, g_mix, g_cross, g_mem, g_ffn, g_final, w_fox_in, b_fox_f, w_fox_out, w_mla_a, g_mla_q, g_mla_kv, w_mla_qb, w_mla_kvb, w_mla_out, w_x_q, w_x_kv, w_x_o, w_ffn_gu, w_ffn_down):
    depth, d = g_mix.shape
    fox_heads = b_fox_f.shape[1]
    x_heads = cache_mem_k.shape[3]
    q_lora = g_mla_q.shape[1]
    kv_lora = g_mla_kv.shape[1]
    mla_heads = w_mla_out.shape[1] // HEAD_DIM
    ff = w_ffn_down.shape[1]
    row = lambda g: [g[i].reshape(1, -1) for i in range(g.shape[0])]
    gu = w_ffn_gu.astype(BF16)
    wts = {
        "g_mix": row(g_mix), "g_cross": row(g_cross), "g_ffn": row(g_ffn), "g_final": g_final.reshape(1, d),
        "g_mla_q": row(g_mla_q), "g_mla_kv": row(g_mla_kv),
        "fox": [_prep_fox(w_fox_in[j], b_fox_f[j], w_fox_out[j], fox_heads) for j in range(w_fox_in.shape[0])],
        "mla": [_prep_mla(w_mla_a[j], w_mla_qb[j], w_mla_kvb[j], w_mla_out[j], q_lora, kv_lora, mla_heads)
                for j in range(w_mla_a.shape[0])],
        "cross": [(w_x_q[i].astype(BF16), w_x_o[i].astype(BF16)) for i in range(depth)],
        "ffn": [(gu[i, :, :ff], gu[i, :, ff:], w_ffn_down[i].astype(BF16)) for i in range(depth)],
        "fox_heads": fox_heads, "x_heads": x_heads, "mla_heads": mla_heads,
        "mla_scale": (HEAD_DIM + MLA_ROPE) ** -0.5,
    }
    bp, n_mem, _ = mem_prompt.shape
    mk, mv = _memory_kv(mem_prompt.reshape(bp * n_mem, d), g_mem.reshape(depth, 1, d), w_x_kv.astype(BF16))
    xw = mk.shape[-1]
    mk = mk.reshape(depth, bp, n_mem, xw)
    mv = mv.reshape(depth, bp, n_mem, xw)
    y_p, fk_p, fv_p, fl_p, mc_p, mr_p = _trunk_prompt(x_prompt, mk, mv, wts)
    past_len = cache_fox_k.shape[2]
    pos_s = past_len + jnp.arange(x_sample.shape[1], dtype=jnp.int32)
    bs = x_sample.shape[0]
    y_s, fk_s, fv_s, fl_s, mc_s, mr_s = _trunk_sample(
        x_sample, pos_s, (cache_fox_k, cache_fox_v, cache_fox_logf), (cache_mla_ckv, cache_mla_krope),
        cache_mem_k.reshape(depth, bs, n_mem, xw), cache_mem_v.reshape(depth, bs, n_mem, xw), wts)
    dh = xw // x_heads
    return (y_p, y_s, fk_p, fv_p, fl_p, mc_p, mr_p,
            mk.reshape(depth, bp, n_mem, x_heads, dh), mv.reshape(depth, bp, n_mem, x_heads, dh),
            fk_s, fv_s, fl_s, mc_s, mr_s)
```

```python
import functools

import jax
import jax.numpy as jnp
from jax import lax
from jax.experimental import pallas as pl
from jax.experimental.pallas import tpu as pltpu

EPS = 1e-6
CHUNK = 64
ROPE_THETA = 10000.0
LANE = 128
HEAD_DIM = 64
PAIR = 2
MLA_ROPE = 32
NEG = -1e30
LOG2E = 1.4426950408889634
VMEM_LIMIT = 56 * 1024 * 1024
BF16 = jnp.bfloat16
F32 = jnp.float32


def _dot(a, b):
    return jnp.dot(a, b, preferred_element_type=F32)


def _dot_nt(a, b):
    return lax.dot_general(a, b, (((1,), (1,)), ((), ())), preferred_element_type=F32)


def _rms(x):
    return x * lax.rsqrt(jnp.mean(x * x, axis=-1, keepdims=True) + EPS)


def _params(*sem):
    return pltpu.CompilerParams(dimension_semantics=sem, vmem_limit_bytes=VMEM_LIMIT)


def _row_tile(m, cap=512):
    t = min(m, cap)
    assert m % t == 0
    return t


def _full(shape):
    return pl.BlockSpec(shape, lambda *_: (0,) * len(shape), pipeline_mode=pl.Buffered(1))


def _log_sigmoid(f):
    return jnp.minimum(f, 0.0) - jnp.log1p(jnp.exp(-jnp.abs(f)))


def _split3(x):
    hi = x.astype(BF16)
    r = x - hi.astype(F32)
    mid = r.astype(BF16)
    lo = (r - mid.astype(F32)).astype(BF16)
    return hi, mid, lo


def _layer_map(imap, layer, *idx):
    return (layer,) + tuple(imap(*idx))


def _stacked_call(body, *, grid, in_specs, args, outs, layer, prev, sem, name, scratch=()):
    out_specs, out_shape, stacked = [], [], []
    for k, (shape, dtype, blk, imap, n_layers) in enumerate(outs):
        if n_layers is None:
            out_specs.append(pl.BlockSpec(blk, imap))
            out_shape.append(jax.ShapeDtypeStruct(shape, dtype))
        else:
            out_specs.append(pl.BlockSpec((None,) + tuple(blk), functools.partial(_layer_map, imap, layer)))
            out_shape.append(jax.ShapeDtypeStruct((n_layers,) + tuple(shape), dtype))
            stacked.append(k)
    aliases = {}
    in_specs = list(in_specs)
    args = list(args)
    if prev is not None:
        for k, arr in zip(stacked, prev):
            aliases[len(args)] = k
            in_specs.append(pl.BlockSpec(memory_space=pl.ANY))
            args.append(arr)
    n_alias = len(aliases)

    def wrapped(*refs):
        n_in = len(args) - n_alias
        body(*refs[:n_in], *refs[n_in + n_alias:])

    res = pl.pallas_call(
        wrapped, grid=grid, in_specs=in_specs, out_specs=out_specs, out_shape=out_shape,
        input_output_aliases=aliases, scratch_shapes=list(scratch),
        compiler_params=_params(*sem), name=name,
    )(*args)
    return res, [res[k] for k in stacked]


def _fox_in_body(x_ref, g_ref, wq, wk, wv, wg, wf, bf_ref, q_out, k_out, v_out, gate_out, lf_out, *, scale, heads):
    h = (_rms(x_ref[...]) * g_ref[...]).astype(BF16)
    q_out[...] = (_dot(h, wq[...]) * scale).astype(BF16)
    k_out[...] = _dot(h, wk[...])
    v_out[...] = _dot(h, wv[...])
    gate_out[...] = jax.nn.sigmoid(_dot(h, wg[...]))
    lf_out[...] = _log_sigmoid(_dot(h, wf[...]) + bf_ref[...])[:, :heads]


def _fox_in(x, g, wq, wk, wv, wg, wf, bf, heads):
    m, d = x.shape
    w = wq.shape[1]
    tm = _row_tile(m)
    row = lambda n: pl.BlockSpec((tm, n), lambda i: (i, 0))
    return pl.pallas_call(
        functools.partial(_fox_in_body, scale=HEAD_DIM ** -0.5, heads=heads),
        grid=(m // tm,),
        in_specs=[row(d), _full((1, d)), _full((d, w)), _full((d, w)), _full((d, w)), _full((d, w)),
                  _full((d, LANE)), _full((1, LANE))],
        out_specs=[row(w), row(w), row(w), row(w), row(heads)],
        out_shape=[jax.ShapeDtypeStruct((m, w), BF16), jax.ShapeDtypeStruct((m, w), F32),
                   jax.ShapeDtypeStruct((m, w), F32), jax.ShapeDtypeStruct((m, w), F32),
                   jax.ShapeDtypeStruct((m, heads), F32)],
        compiler_params=_params("parallel"),
        name="fox_in",
    )(x, g, wq, wk, wv, wg, wf, bf)


def _fox_in_t_body(x_ref, g_ref, wqt, wkt, wvt, wk, wg, wft, bf_ref,
                   qt_out, k_out, kt_out, vt_out, gate_out, lft_out, *, scale):
    h = (_rms(x_ref[...]) * g_ref[...]).astype(BF16)
    qt_out[...] = (_dot_nt(wqt[...], h) * scale).astype(BF16)
    k_out[...] = _dot(h, wk[...]).astype(BF16)
    kt_out[...] = _dot_nt(wkt[...], h)
    vt_out[...] = _dot_nt(wvt[...], h)
    gate_out[...] = jax.nn.sigmoid(_dot(h, wg[...]))
    lft_out[...] = _log_sigmoid(_dot_nt(wft[...], h) + bf_ref[...])


def _fox_in_t(x, g, wts, layer, n_layers, prev):
    wqt, wkt, wvt, wk, wg, wft, bf = wts
    b, t, d = x.shape
    w = wk.shape[1]
    heads = wft.shape[0]
    tm = _row_tile(t)
    rows = lambda bi, i: (bi, i, 0)
    cols = lambda bi, i: (bi, 0, i)
    res, stacked = _stacked_call(
        functools.partial(_fox_in_t_body, scale=HEAD_DIM ** -0.5 * LOG2E),
        grid=(b, t // tm),
        in_specs=[pl.BlockSpec((None, tm, d), rows), _full((1, d)), _full((w, d)), _full((w, d)), _full((w, d)),
                  _full((d, w)), _full((d, w)), _full((heads, d)), _full((heads, 1))],
        args=[x, g, wqt, wkt, wvt, wk, wg, wft, bf],
        outs=[((b, w, t), BF16, (None, w, tm), cols, None),
              ((b, t, w), BF16, (None, tm, w), rows, None),
              ((b, w, t), F32, (None, w, tm), cols, n_layers),
              ((b, w, t), F32, (None, w, tm), cols, n_layers),
              ((b, t, w), F32, (None, tm, w), rows, None),
              ((b, heads, t), F32, (None, heads, tm), cols, n_layers)],
        layer=layer, prev=prev, sem=("parallel", "parallel"), name="fox_in_t")
    qt, k, _, _, gate, _ = res
    return qt, k, gate, stacked


CUM_BLOCK = 256


def _cumsum_lanes_body(lf_ref, out_ref):
    h, t = lf_ref.shape
    r = lax.broadcasted_iota(jnp.int32, (CUM_BLOCK, CUM_BLOCK), 0)
    c = lax.broadcasted_iota(jnp.int32, (CUM_BLOCK, CUM_BLOCK), 1)
    tri = jnp.where(r <= c, 1.0, 0.0).astype(BF16)
    carry = jnp.zeros((h, 1), F32)
    for b in range(t // CUM_BLOCK):
        hi, mid, lo = _split3(lf_ref[:, b * CUM_BLOCK:(b + 1) * CUM_BLOCK])
        out = _dot(hi, tri) + _dot(mid, tri) + _dot(lo, tri) + carry
        out_ref[:, b * CUM_BLOCK:(b + 1) * CUM_BLOCK] = out
        carry = out[:, CUM_BLOCK - 1:CUM_BLOCK]


def _cumsum_lanes(lf, layer=None):
    b, h, t = lf.shape[-3:]
    assert t % CUM_BLOCK == 0
    if layer is None:
        spec = pl.BlockSpec((None, h, t), lambda i: (i, 0, 0))
    else:
        spec = pl.BlockSpec((None, None, h, t), lambda i: (layer, i, 0, 0))
    return pl.pallas_call(
        _cumsum_lanes_body, grid=(b,),
        in_specs=[spec],
        out_specs=pl.BlockSpec((None, h, t), lambda i: (i, 0, 0)),
        out_shape=jax.ShapeDtypeStruct((b, h, t), F32),
        compiler_params=_params("parallel"), name="logf_cumsum_t",
    )(lf)


N_PIECE = 3
TRIP_BLOCKS = 4


def _attn_prompt_body(*refs, fox, tq):
    if fox:
        qt_ref, k_ref, vt_ref, crow_ref, ccol_ref, gate_ref, o_ref, kx, vb = refs
    else:
        qnt_ref, qrt_ref, kn_ref, krd_ref, vt_ref, o_ref, kx = refs
        vb = vt_ref
    hp = pl.program_id(1)
    tk = tq
    t = kx.shape[0]
    ones_lo = PAIR * N_PIECE

    if fox:
        kx[:, :LANE] = k_ref[...]
        vb[...] = vt_ref[...].astype(BF16)
        heads = ccol_ref.shape[1]
        hrow = lax.broadcasted_iota(jnp.int32, (heads, LANE), 0)
        lcol = lax.broadcasted_iota(jnp.int32, (heads, LANE), 1)
        ext = jnp.zeros((t, LANE), F32)
        for i, piece in enumerate(_split3(ccol_ref[...] * LOG2E)):
            hit = ((hrow == PAIR * hp) & (lcol == i)) | ((hrow == PAIR * hp + 1) & (lcol == N_PIECE + i))
            ext = ext + _dot(piece, jnp.where(hit, 1.0, 0.0).astype(BF16))
        lane = lax.broadcasted_iota(jnp.int32, (t, LANE), 1)
        ext = jnp.where((lane >= ones_lo) & (lane < ones_lo + N_PIECE), 1.0, ext)
        kx[:, LANE:] = ext.astype(BF16)
    else:
        kx[:, :LANE] = kn_ref[...]
        kx[:, LANE:] = krd_ref[...]

    lax.fori_loop(0, t // tq, functools.partial(_attn_query_block, refs, kx, vb, fox=fox, tq=tq), 0)


def _attn_query_block(refs, kx, vb, qi, _, *, fox, tq):
    if fox:
        qt_ref, _, _, crow_ref, _, gate_ref, o_ref = refs[:7]
    else:
        qnt_ref, qrt_ref, _, _, _, o_ref = refs[:6]
    tk = tq
    ones_lo = PAIR * N_PIECE
    cols = pl.ds(pl.multiple_of(qi * tq, tq), tq)

    sub = lax.broadcasted_iota(jnp.int32, (LANE, tq), 0)
    zero = jnp.zeros((), BF16)
    qs = []
    for e in range(PAIR):
        in_head = (sub >= e * HEAD_DIM) & (sub < (e + 1) * HEAD_DIM)
        if fox:
            ext = jnp.where((sub >= N_PIECE * e) & (sub < N_PIECE * (e + 1)), -1.0, 0.0)
            for i, piece in enumerate(_split3(crow_ref[e:e + 1, cols] * LOG2E)):
                ext = jnp.where(sub == ones_lo + i, piece.astype(F32), ext)
            qs.append(jnp.concatenate([jnp.where(in_head, qt_ref[:, cols], zero), ext.astype(BF16)], axis=0))
        else:
            in_rope = (sub >= e * MLA_ROPE) & (sub < (e + 1) * MLA_ROPE)
            qs.append(jnp.concatenate([jnp.where(in_head, qnt_ref[:, cols], zero),
                                       jnp.where(in_rope, qrt_ref[:, cols], zero)], axis=0))

    key = lax.broadcasted_iota(jnp.int32, (tk, tq), 0)
    qry = lax.broadcasted_iota(jnp.int32, (tk, tq), 1)
    visible = (key <= qry) if fox else ((key // CHUNK) <= (qry // CHUNK))

    def blocks(j0, carry, nblk, diagonal_last):
        starts = [pl.multiple_of((j0 + u) * tk, tk) for u in range(nblk)]
        scores = [[_dot(kx[pl.ds(st, tk), :], qs[e]) for e in range(PAIR)] for st in starts]
        carry = list(carry)
        for u, st in enumerate(starts):
            for e in range(PAIR):
                m, l, acc = carry[3 * e:3 * e + 3]
                s = scores[u][e]
                if diagonal_last and u == nblk - 1:
                    s = jnp.where(visible, s, NEG)
                m_new = jnp.maximum(m, jnp.max(s, axis=0, keepdims=True))
                alpha = jnp.exp2(m - m_new)
                p = jnp.exp2(s - m_new)
                l_new = alpha * l + jnp.sum(p, axis=0, keepdims=True)
                v_j = vb[e * HEAD_DIM:(e + 1) * HEAD_DIM, pl.ds(st, tk)]
                carry[3 * e:3 * e + 3] = [m_new, l_new, alpha * acc + _dot(v_j, p.astype(BF16))]
        return tuple(carry)

    init = (jnp.full((1, tq), NEG, F32), jnp.zeros((1, tq), F32), jnp.zeros((HEAD_DIM, tq), F32)) * PAIR
    main = qi // TRIP_BLOCKS
    left = qi - main * TRIP_BLOCKS
    carry = lax.fori_loop(0, main, lambda i, c: blocks(TRIP_BLOCKS * i, c, TRIP_BLOCKS, False), init)
    for n in range(TRIP_BLOCKS):
        carry = lax.fori_loop(0, (left == n).astype(jnp.int32),
                              lambda _, c, n=n: blocks(qi - n, c, n + 1, True), carry)
    o = jnp.concatenate([carry[2] / carry[1], carry[5] / carry[4]], axis=0).T
    if fox:
        o = o * gate_ref[cols, :]
    o_ref[cols, :] = o.astype(BF16)
    return 0


def _fox_attn_prompt(qt, k, vt_all, layer, cum_t, gate, tq=512):
    b, w, t = qt.shape
    heads = cum_t.shape[1]
    tq = min(tq, t)
    npair = w // LANE
    rowblk = pl.BlockSpec((None, t, LANE), lambda bi, hp: (bi, 0, hp))
    return pl.pallas_call(
        functools.partial(_attn_prompt_body, fox=True, tq=tq),
        grid=(b, npair),
        in_specs=[pl.BlockSpec((None, LANE, t), lambda bi, hp: (bi, hp, 0)),
                  rowblk,
                  pl.BlockSpec((None, None, LANE, t), lambda bi, hp: (layer, bi, hp, 0)),
                  pl.BlockSpec((None, None, PAIR, t), lambda bi, hp: (bi, hp, 0, 0)),
                  pl.BlockSpec((None, t, heads), lambda bi, hp: (bi, 0, 0)),
                  rowblk],
        out_specs=rowblk,
        out_shape=jax.ShapeDtypeStruct((b, t, w), BF16),
        scratch_shapes=[pltpu.VMEM((t, 2 * LANE), BF16), pltpu.VMEM((LANE, t), BF16)],
        compiler_params=_params("parallel", "parallel"),
        name="fox_attn_prompt",
    )(qt, k, vt_all, cum_t.reshape(b, npair, PAIR, t), jnp.swapaxes(cum_t, 1, 2), gate)


def _mla_attn_prompt(qnt, qrt, kn, krd, vt, tq=512):
    b, w, t = qnt.shape
    tq = min(tq, t)
    npair = w // LANE
    colblk = pl.BlockSpec((None, LANE, t), lambda bi, hp: (bi, hp, 0))
    rowblk = pl.BlockSpec((None, t, LANE), lambda bi, hp: (bi, 0, hp))
    return pl.pallas_call(
        functools.partial(_attn_prompt_body, fox=False, tq=tq),
        grid=(b, npair),
        in_specs=[colblk, colblk, rowblk, pl.BlockSpec((None, t, LANE), lambda bi, hp: (bi, 0, 0)), colblk],
        out_specs=rowblk,
        out_shape=jax.ShapeDtypeStruct((b, t, w), BF16),
        scratch_shapes=[pltpu.VMEM((t, 2 * LANE), BF16)],
        compiler_params=_params("parallel", "parallel"),
        name="mla_attn_prompt",
    )(qnt, qrt, kn, krd, vt)


def _fox_attn_sample_body(q_ref, kt_ref, vt_ref, kn_ref, vn_ref, cq_ref, ck_ref, gate_ref, o_ref):
    hp = pl.program_id(1)
    t = q_ref.shape[0]
    p = kt_ref.shape[1]
    rows = PAIR * t
    lane = lax.broadcasted_iota(jnp.int32, (rows, LANE), 1)
    row = lax.broadcasted_iota(jnp.int32, (rows, LANE), 0)
    q2 = jnp.concatenate([q_ref[...]] * PAIR, axis=0)
    q2 = jnp.where((lane // HEAD_DIM) == (row // t), q2, jnp.zeros((), BF16))
    hl = lax.broadcasted_iota(jnp.int32, cq_ref.shape, 1)
    cq = jnp.concatenate([jnp.sum(jnp.where(hl == PAIR * hp + e, cq_ref[...], 0.0), axis=1, keepdims=True)
                          for e in range(PAIR)], axis=0)
    first_p = lax.broadcasted_iota(jnp.int32, (rows, p), 0) < t
    rn = lax.broadcasted_iota(jnp.int32, (rows, t), 0)
    cn = lax.broadcasted_iota(jnp.int32, (rows, t), 1)
    s_p = _dot(q2, kt_ref[...].astype(BF16)) + cq - jnp.where(first_p, ck_ref[0:1, :p], ck_ref[1:2, :p])
    s_n = _dot_nt(q2, kn_ref[...].astype(BF16)) + cq - jnp.where(rn < t, ck_ref[0:1, p:], ck_ref[1:2, p:])
    s_n = jnp.where(cn <= lax.rem(rn, t), s_n, NEG)
    m = jnp.maximum(jnp.max(s_p, axis=-1, keepdims=True), jnp.max(s_n, axis=-1, keepdims=True))
    p_p = jnp.exp(s_p - m)
    p_n = jnp.exp(s_n - m)
    l = jnp.sum(p_p, axis=-1, keepdims=True) + jnp.sum(p_n, axis=-1, keepdims=True)
    o = (_dot_nt(p_p.astype(BF16), vt_ref[...].astype(BF16)) + _dot(p_n.astype(BF16), vn_ref[...].astype(BF16))) / l
    o = jnp.where(lax.broadcasted_iota(jnp.int32, (t, LANE), 1) < HEAD_DIM, o[:t], o[t:]) * gate_ref[...]
    o_ref[...] = o.astype(BF16)


def _fox_attn_sample(q, kt_cache, vt_cache, layer, k_new, v_new, cum_t, gate):
    b, t, w = q.shape
    p = kt_cache.shape[-1]
    heads = cum_t.shape[1]
    npair = w // LANE
    new = pl.BlockSpec((None, t, LANE), lambda bi, hp: (bi, 0, hp))
    past = pl.BlockSpec((None, None, LANE, p), lambda bi, hp: (layer, bi, hp, 0))
    return pl.pallas_call(
        _fox_attn_sample_body,
        grid=(b, npair),
        in_specs=[new, past, past, new, new,
                  pl.BlockSpec((None, t, heads), lambda bi, hp: (bi, 0, 0)),
                  pl.BlockSpec((None, None, PAIR, p + t), lambda bi, hp: (bi, hp, 0, 0)),
                  new],
        out_specs=new,
        out_shape=jax.ShapeDtypeStruct((b, t, w), BF16),
        compiler_params=_params("parallel", "parallel"),
        name="fox_attn_sample",
    )(q, kt_cache, vt_cache, k_new, v_new, jnp.swapaxes(cum_t[:, :, p:], 1, 2),
      cum_t.reshape(b, npair, PAIR, p + t), gate)


def _mla_attn_sample_body(qn_ref, qr_ref, cp_ref, krp_ref, cn_ref, krn_ref, wkn_ref, wv_ref, o_ref, *, past_len):
    t, w = qn_ref.shape
    p = cp_ref.shape[0]
    heads = w // HEAD_DIM
    rows = heads * t
    lane = lax.broadcasted_iota(jnp.int32, (rows, w), 1)
    row = lax.broadcasted_iota(jnp.int32, (rows, w), 0)
    own = (lane // HEAD_DIM) == (row // t)
    q_wide = jnp.where(own, jnp.concatenate([qn_ref[...]] * heads, axis=0), jnp.zeros((), BF16))
    q_lat = _dot_nt(q_wide, wkn_ref[...]).astype(BF16)
    qr = qr_ref[...]
    c_p = cp_ref[...].astype(BF16)
    c_n = cn_ref[...].astype(BF16)
    s_p = _dot_nt(q_lat, c_p) + _dot(qr, krp_ref[...].astype(BF16))
    s_n = _dot_nt(q_lat, c_n) + _dot_nt(qr, krn_ref[...].astype(BF16))
    q_chunk_p = (past_len + lax.rem(lax.broadcasted_iota(jnp.int32, (rows, p), 0), t)) // CHUNK
    s_p = jnp.where((lax.broadcasted_iota(jnp.int32, (rows, p), 1) // CHUNK) <= q_chunk_p, s_p, NEG)
    q_chunk_n = (past_len + lax.rem(lax.broadcasted_iota(jnp.int32, (rows, t), 0), t)) // CHUNK
    s_n = jnp.where(((past_len + lax.broadcasted_iota(jnp.int32, (rows, t), 1)) // CHUNK) <= q_chunk_n, s_n, NEG)
    m = jnp.maximum(jnp.max(s_p, axis=-1, keepdims=True), jnp.max(s_n, axis=-1, keepdims=True))
    p_p = jnp.exp(s_p - m)
    p_n = jnp.exp(s_n - m)
    l = jnp.sum(p_p, axis=-1, keepdims=True) + jnp.sum(p_n, axis=-1, keepdims=True)
    o_lat = (_dot(p_p.astype(BF16), c_p) + _dot(p_n.astype(BF16), c_n)) / l
    o_wide = jnp.where(own, _dot(o_lat.astype(BF16), wv_ref[...]), 0.0)
    o = o_wide[:t]
    for h in range(1, heads):
        o = o + o_wide[h * t:(h + 1) * t]
    o_ref[...] = o.astype(BF16)


def _mla_attn_sample(qn, qr_rows, ckv_cache, krt_cache, layer, ckv_new, kr_new, wkn, wv):
    b, t, w = qn.shape
    p, c = ckv_cache.shape[2:]
    rows = qr_rows.shape[1]
    blk = lambda *s: pl.BlockSpec((None,) + s, lambda bi: (bi,) + (0,) * len(s))
    past = lambda *s: pl.BlockSpec((None, None) + s, lambda bi: (layer, bi) + (0,) * len(s))
    return pl.pallas_call(
        functools.partial(_mla_attn_sample_body, past_len=p),
        grid=(b,),
        in_specs=[blk(t, w), blk(rows, MLA_ROPE), past(p, c), past(MLA_ROPE, p), blk(t, c), blk(t, MLA_ROPE),
                  _full(wkn.shape), _full(wv.shape)],
        out_specs=blk(t, w),
        out_shape=jax.ShapeDtypeStruct((b, t, w), BF16),
        compiler_params=_params("parallel"),
        name="mla_attn_sample",
    )(qn, qr_rows, ckv_cache, krt_cache, ckv_new, kr_new, wkn, wv)


def _mla_in_body(x_ref, g_ref, w1, gq_ref, gkv_ref, w2, cos_ref, sin_ref, cosq_ref, sinq_ref,
                 qn_out, qr_out, ckv_out, kr_out, *, q_lora, kv_lora, scale):
    h = (_rms(x_ref[...]) * g_ref[...]).astype(BF16)
    a = _dot(h, w1[...])
    c_q = (_rms(a[:, :q_lora]) * gq_ref[...]).astype(BF16)
    ckv_out[...] = _rms(a[:, q_lora:q_lora + kv_lora]) * gkv_ref[...]
    o = q_lora + kv_lora
    krd = a[:, o:o + LANE] * cos_ref[...] + a[:, o + LANE:o + 2 * LANE] * sin_ref[...]
    kr_out[...] = krd[:, :MLA_ROPE]
    w = qn_out.shape[1]
    r = qr_out.shape[1]
    qn_out[...] = (_dot(c_q, w2[:, :w]) * scale).astype(BF16)
    qr = _dot(c_q, w2[:, w:w + r]) * cosq_ref[...] + _dot(c_q, w2[:, w + r:w + 2 * r]) * sinq_ref[...]
    qr_out[...] = (qr * scale).astype(BF16)


def _mla_in(x, g, w1, gq, gkv, w2, cos, sin, heads, scale):
    m, d = x.shape
    q_lora = gq.shape[1]
    kv_lora = gkv.shape[1]
    w = heads * HEAD_DIM
    r = heads * MLA_ROPE
    tm = _row_tile(m)
    t = cos.shape[0]
    assert tm % t == 0
    per_head = lambda tab: jnp.tile(tab[:, :MLA_ROPE], (tm // t, heads))
    cosq, sinq = per_head(cos), per_head(sin)
    cos, sin = jnp.tile(cos, (tm // t, 1)), jnp.tile(sin, (tm // t, 1))
    row = lambda n: pl.BlockSpec((tm, n), lambda i: (i, 0))
    return pl.pallas_call(
        functools.partial(_mla_in_body, q_lora=q_lora, kv_lora=kv_lora, scale=scale),
        grid=(m // tm,),
        in_specs=[row(d), _full((1, d)), _full(w1.shape), _full((1, q_lora)), _full((1, kv_lora)),
                  _full(w2.shape), _full((tm, LANE)), _full((tm, LANE)), _full((tm, r)), _full((tm, r))],
        out_specs=[row(w), row(r), row(kv_lora), row(MLA_ROPE)],
        out_shape=[jax.ShapeDtypeStruct((m, w), BF16), jax.ShapeDtypeStruct((m, r), BF16),
                   jax.ShapeDtypeStruct((m, kv_lora), F32), jax.ShapeDtypeStruct((m, MLA_ROPE), F32)],
        compiler_params=_params("parallel"),
        name="mla_in",
    )(x, g, w1, gq, gkv, w2, cos, sin, cosq, sinq)


def _mla_in_t_body(x_ref, g_ref, w1, gq_ref, gkv_ref, wqnt, wqat, wqbt, wkn, wvt, wkrt,
                   cos_ref, sin_ref, cost_ref, sint_ref,
                   qnt_out, qrt_out, kn_out, krd_out, vt_out, ckv_out, krt_out, *, q_lora, kv_lora, scale):
    h = (_rms(x_ref[...]) * g_ref[...]).astype(BF16)
    a = _dot(h, w1[...])
    c_q = (_rms(a[:, :q_lora]) * gq_ref[...]).astype(BF16)
    c_kv = _rms(a[:, q_lora:q_lora + kv_lora]) * gkv_ref[...]
    ckv_out[...] = c_kv
    o = q_lora + kv_lora
    krd_out[...] = (a[:, o:o + LANE] * cos_ref[...] + a[:, o + LANE:o + 2 * LANE] * sin_ref[...]).astype(BF16)
    cost = cost_ref[...]
    sint = sint_ref[...]
    kab = _dot_nt(wkrt[...], h)
    krt_out[...] = kab[:MLA_ROPE] * cost[:MLA_ROPE] + kab[MLA_ROPE:] * sint[:MLA_ROPE]
    c_kv = c_kv.astype(BF16)
    kn_out[...] = _dot(c_kv, wkn[...]).astype(BF16)
    vt_out[...] = _dot_nt(wvt[...], c_kv).astype(BF16)
    qnt_out[...] = (_dot_nt(wqnt[...], c_q) * scale).astype(BF16)
    for p in range(qrt_out.shape[0] // LANE):
        rows = slice(p * LANE, (p + 1) * LANE)
        qa = _dot_nt(wqat[rows, :], c_q)
        qb = _dot_nt(wqbt[rows, :], c_q)
        qrt_out[rows, :] = ((qa * cost + qb * sint) * scale).astype(BF16)


def _mla_in_t(x, g, gq, gkv, wts, tables, scale, layer, n_layers, prev):
    w1, wqnt, wqat, wqbt, wkn, wvt, wkrt = wts
    cos, sin, cost, sint = tables
    b, t, d = x.shape
    q_lora = gq.shape[1]
    kv_lora = gkv.shape[1]
    w = wkn.shape[1]
    tm = _row_tile(t)
    rows = lambda bi, i: (bi, i, 0)
    cols = lambda bi, i: (bi, 0, i)
    res, stacked = _stacked_call(
        functools.partial(_mla_in_t_body, q_lora=q_lora, kv_lora=kv_lora, scale=scale),
        grid=(b, t // tm),
        in_specs=[pl.BlockSpec((None, tm, d), rows), _full((1, d)), _full(w1.shape), _full((1, q_lora)),
                  _full((1, kv_lora)), _full(wqnt.shape), _full(wqat.shape), _full(wqbt.shape), _full(wkn.shape),
                  _full(wvt.shape), _full(wkrt.shape),
                  pl.BlockSpec((tm, LANE), lambda bi, i: (i, 0)), pl.BlockSpec((tm, LANE), lambda bi, i: (i, 0)),
                  pl.BlockSpec((LANE, tm), lambda bi, i: (0, i)), pl.BlockSpec((LANE, tm), lambda bi, i: (0, i))],
        args=[x, g, w1, gq, gkv, wqnt, wqat, wqbt, wkn, wvt, wkrt, cos, sin, cost, sint],
        outs=[((b, w, t), BF16, (None, w, tm), cols, None),
              ((b, w, t), BF16, (None, w, tm), cols, None),
              ((b, t, w), BF16, (None, tm, w), rows, None),
              ((b, t, LANE), BF16, (None, tm, LANE), rows, None),
              ((b, w, t), BF16, (None, w, tm), cols, None),
              ((b, t, kv_lora), F32, (None, tm, kv_lora), rows, n_layers),
              ((b, MLA_ROPE, t), F32, (None, MLA_ROPE, tm), cols, n_layers)],
        layer=layer, prev=prev, sem=("parallel", "parallel"), name="mla_in_t")
    return res[:5], stacked


def _memory_kv_body(m_ref, g_ref, w_ref, k_out, v_out):
    h = (_rms(m_ref[...]) * g_ref[...]).astype(BF16)
    n = k_out.shape[-1]
    k_out[...] = _dot(h, w_ref[:, :n])
    v_out[...] = _dot(h, w_ref[:, n:])


def _memory_kv(mem, g_mem, w_kv):
    m, d = mem.shape
    depth, _, n2 = w_kv.shape
    n = n2 // 2
    tm = _row_tile(m)
    out = pl.BlockSpec((None, tm, n), lambda l, i: (l, i, 0))
    return pl.pallas_call(
        _memory_kv_body, grid=(depth, m // tm),
        in_specs=[pl.BlockSpec((tm, d), lambda l, i: (i, 0)),
                  pl.BlockSpec((None, 1, d), lambda l, i: (l, 0, 0)),
                  pl.BlockSpec((None, d, n2), lambda l, i: (l, 0, 0))],
        out_specs=[out, out],
        out_shape=[jax.ShapeDtypeStruct((depth, m, n), F32)] * 2,
        compiler_params=_params("parallel", "parallel"), name="memory_kv",
    )(mem, g_mem, w_kv)


FF_BLOCK = 256


def _layer_tail_body(o_ref, x_ref, wout, gc_ref, wq, mk_ref, mv_ref, wxo, gf_ref, wg, wu, wd, *rest, heads, final):
    if final:
        gl_ref, x_out, y_out, att = rest
    else:
        x_out, att = rest
    bb, tm, d = x_ref.shape
    x = x_ref[...].reshape(bb * tm, d) + _dot(o_ref[...].reshape(bb * tm, o_ref.shape[-1]), wout[...])
    h = (_rms(x) * gc_ref[...]).astype(BF16)
    dh = wq.shape[1] // heads
    q = (_dot(h, wq[...]) * dh ** -0.5).astype(BF16)
    for b in range(bb):
        for hd in range(heads):
            cols = slice(hd * dh, (hd + 1) * dh)
            s = _dot_nt(q[b * tm:(b + 1) * tm, cols], mk_ref[b, :, cols].astype(BF16))
            p = jnp.exp(s - jnp.max(s, axis=-1, keepdims=True))
            o = _dot(p.astype(BF16), mv_ref[b, :, cols].astype(BF16)) / jnp.sum(p, axis=-1, keepdims=True)
            att[b * tm:(b + 1) * tm, cols] = o.astype(BF16)
    x = x + _dot(att[...], wxo[...])
    h = (_rms(x) * gf_ref[...]).astype(BF16)
    for c in range(wg.shape[1] // FF_BLOCK):
        cols = slice(c * FF_BLOCK, (c + 1) * FF_BLOCK)
        gate = _dot(h, wg[:, cols])
        up = _dot(h, wu[:, cols])
        x = x + _dot((gate * jax.nn.sigmoid(gate) * up).astype(BF16), wd[cols, :])
    x_out[...] = x.reshape(bb, tm, d)
    if final:
        y_out[...] = (_rms(x) * gl_ref[...]).reshape(bb, tm, d)


def _layer_tail(o, x, w_out, g_cross, wq, mk, mv, layer, wxo, g_ffn, wg, wu, wd, heads, g_final=None):
    b, t, d = x.shape
    w = o.shape[-1]
    n, xw = mk.shape[2:]
    ff = wg.shape[1]
    assert ff % FF_BLOCK == 0
    tm = min(t, 512)
    bb = max(1, min(b, 64 // tm))
    blk = lambda c: pl.BlockSpec((bb, tm, c), lambda bi, ti: (bi, ti, 0))
    mblk = pl.BlockSpec((None, bb, n, xw), lambda bi, ti: (layer, bi, 0, 0))
    final = g_final is not None
    ins = [blk(w), blk(d), _full((w, d)), _full((1, d)), _full((d, xw)), mblk, mblk, _full((xw, d)),
           _full((1, d)), _full((d, ff)), _full((d, ff)), _full((ff, d))]
    args = [o, x, w_out, g_cross, wq, mk, mv, wxo, g_ffn, wg, wu, wd]
    if final:
        ins.append(_full((1, d)))
        args.append(g_final)
    out = jax.ShapeDtypeStruct((b, t, d), F32)
    return pl.pallas_call(
        functools.partial(_layer_tail_body, heads=heads, final=final),
        grid=(b // bb, t // tm),
        in_specs=ins,
        out_specs=[blk(d), blk(d)] if final else blk(d),
        out_shape=[out, out] if final else out,
        scratch_shapes=[pltpu.VMEM((bb * tm, xw), BF16)],
        compiler_params=_params("parallel", "parallel"), name="layer_tail",
    )(*args)


def _prep_fox(w_in, b_f, w_out, heads):
    width = w_out.shape[0]
    w_in = w_in.astype(BF16)
    wq, wk, wv, wf, wg = jnp.split(w_in, [width, 2 * width, 3 * width, 3 * width + heads], axis=1)
    rows = (wq, wk, wv, wg, jnp.pad(wf, ((0, 0), (0, LANE - heads))),
            jnp.pad(b_f, (0, LANE - heads)).reshape(1, LANE))
    cols = (wq.T, wk.T, wv.T, wk, wg, wf.T, b_f.reshape(heads, 1))
    return {"rows": rows, "cols": cols, "out": w_out.astype(BF16)}


def _prep_mla(w_a, w_qb, w_kvb, w_out, q_lora, kv_lora, heads):
    half = MLA_ROPE // 2
    w_a = w_a.astype(BF16)
    d = w_a.shape[0]
    x1 = w_a[:, q_lora + kv_lora:q_lora + kv_lora + half]
    x2 = w_a[:, q_lora + kv_lora + half:]
    zeros = jnp.zeros((d, LANE - 2 * MLA_ROPE), BF16)
    w1 = jnp.concatenate([w_a[:, :q_lora + kv_lora], x1, x2, x1, x2, zeros, x2, x1, x2, x1, zeros], axis=1)
    wkrt = jnp.concatenate([x1, x2, x2, x1], axis=1).T
    qb = w_qb.astype(BF16).reshape(q_lora, heads, HEAD_DIM + MLA_ROPE)
    wqn = qb[:, :, :HEAD_DIM].reshape(q_lora, heads * HEAD_DIM)
    rope = qb[:, :, HEAD_DIM:]
    swapped = jnp.concatenate([rope[..., half:], rope[..., :half]], axis=-1)

    def pack(r):
        r = r.reshape(q_lora, heads // PAIR, PAIR * MLA_ROPE)
        return jnp.pad(r, ((0, 0), (0, 0), (0, LANE - PAIR * MLA_ROPE))).reshape(q_lora, -1)

    wqa, wqb = pack(rope), pack(swapped)
    kvb = w_kvb.astype(BF16).reshape(kv_lora, heads, 2 * HEAD_DIM)
    wkn = kvb[:, :, :HEAD_DIM].reshape(kv_lora, heads * HEAD_DIM)
    wv = kvb[:, :, HEAD_DIM:].reshape(kv_lora, heads * HEAD_DIM)
    flat = lambda r: r.reshape(q_lora, heads * MLA_ROPE)
    rows = (w1, jnp.concatenate([wqn, flat(rope), flat(swapped)], axis=1), wkn, wv)
    cols = (w1, wqn.T, wqa.T, wqb.T, wkn, wv.T, wkrt)
    return {"rows": rows, "cols": cols, "out": w_out.astype(BF16)}


def _rope_tables(pos):
    half = MLA_ROPE // 2
    inv = ROPE_THETA ** (-jnp.arange(half, dtype=F32) / half)
    ang = pos.astype(F32)[:, None] * inv[None, :]
    cos, sin = jnp.cos(ang), jnp.sin(ang)
    z = jnp.zeros((pos.shape[0], LANE - 2 * MLA_ROPE), F32)
    return (jnp.concatenate([cos, cos, cos, cos, z], axis=1),
            jnp.concatenate([-sin, sin, -sin, sin, z], axis=1))


def _after_mixer(x, o, w_out, i, mem_k, mem_v, wts):
    wxq, wxo = wts["cross"][i]
    wgt, wup, wdn = wts["ffn"][i]
    last = i == len(wts["ffn"]) - 1
    res = _layer_tail(o, x, w_out, wts["g_cross"][i], wxq, mem_k, mem_v, i, wxo, wts["g_ffn"][i], wgt, wup, wdn,
                      wts["x_heads"], wts["g_final"] if last else None)
    return res if last else (res, None)


def _trunk_prompt(x, mem_k, mem_v, wts):
    b, t, d = x.shape
    m = b * t
    depth = len(wts["ffn"])
    n_fox, n_mla = len(wts["fox"]), len(wts["mla"])
    cos, sin = _rope_tables(jnp.arange(t, dtype=jnp.int32))
    tables = (cos, sin, cos.T, sin.T)
    fox_state = mla_state = None
    for i in range(depth):
        j = i // 2
        if i % 2 == 0:
            wf = wts["fox"][j]
            qt, k, gate, fox_state = _fox_in_t(x, wts["g_mix"][i], wf["cols"], j, n_fox, fox_state)
            cum_t = _cumsum_lanes(fox_state[2], j)
            o = _fox_attn_prompt(qt, k, fox_state[1], j, cum_t, gate)
        else:
            wf = wts["mla"][j]
            (qnt, qrt, kn, krd, vt), mla_state = _mla_in_t(
                x, wts["g_mix"][i], wts["g_mla_q"][j], wts["g_mla_kv"][j], wf["cols"], tables,
                wts["mla_scale"] * LOG2E, j, n_mla, mla_state)
            o = _mla_attn_prompt(qnt, qrt, kn, krd, vt)
        x, y = _after_mixer(x, o, wf["out"], i, mem_k, mem_v, wts)
    kt, vt, lft = fox_state
    ckv, krt = mla_state
    heads = lft.shape[2]
    unfold = lambda a: jnp.transpose(a.reshape(n_fox, b, heads, a.shape[2] // heads, t), (0, 1, 4, 2, 3))
    return (y.reshape(b, t, d), unfold(kt), unfold(vt), jnp.swapaxes(lft, 2, 3), ckv, jnp.swapaxes(krt, 2, 3))


def _trunk_sample(x, pos, fox_past, mla_past, mem_k, mem_v, wts):
    b, t, d = x.shape
    m = b * t
    depth = len(wts["ffn"])
    cos, sin = _rope_tables(pos)
    heads = wts["fox_heads"]
    mla_heads = wts["mla_heads"]
    fox_k, fox_v, fox_lf, mla_c, mla_r = [], [], [], [], []
    r3 = lambda a: a.reshape(b, t, a.shape[-1])
    n_fox, _, p, _, hd = fox_past[0].shape
    kt_cache = jnp.transpose(fox_past[0], (0, 1, 3, 4, 2)).reshape(n_fox, b, heads * hd, p)
    vt_cache = jnp.transpose(fox_past[1], (0, 1, 3, 4, 2)).reshape(n_fox, b, heads * hd, p)
    lft_cache = jnp.swapaxes(fox_past[2], 2, 3)
    krt_cache = jnp.swapaxes(mla_past[1], 2, 3)
    total = p + t
    pad = jnp.zeros((b, heads, -(-total // CUM_BLOCK) * CUM_BLOCK - total), F32)
    for i in range(depth):
        j = i // 2
        x2 = x.reshape(m, d)
        if i % 2 == 0:
            wf = wts["fox"][j]
            q, k, v, gate, lf = _fox_in(x2, wts["g_mix"][i], *wf["rows"], heads)
            lf_all = jnp.concatenate([lft_cache[j], jnp.swapaxes(r3(lf), 1, 2), pad], axis=2)
            cum_t = _cumsum_lanes(lf_all)[:, :, :total]
            o = _fox_attn_sample(r3(q), kt_cache, vt_cache, j, r3(k), r3(v), cum_t, r3(gate))
            fox_k.append(k.reshape(b, t, heads, hd))
            fox_v.append(v.reshape(b, t, heads, hd))
            fox_lf.append(lf.reshape(b, t, heads))
        else:
            wf = wts["mla"][j]
            w1, w2, wkn, wv = wf["rows"]
            qn, qr, ckv, kr = _mla_in(x2, wts["g_mix"][i], w1, wts["g_mla_q"][j], wts["g_mla_kv"][j],
                                      w2, cos, sin, mla_heads, wts["mla_scale"])
            qr_rows = jnp.swapaxes(qr.reshape(b, t, mla_heads, MLA_ROPE), 1, 2).reshape(b, mla_heads * t, MLA_ROPE)
            o = _mla_attn_sample(r3(qn), qr_rows, mla_past[0], krt_cache, j, r3(ckv), r3(kr), wkn, wv)
            mla_c.append(r3(ckv))
            mla_r.append(r3(kr))
        x, y = _after_mixer(x, o, wf["out"], i, mem_k, mem_v, wts)
    return (y, jnp.stack(fox_k), jnp.stack(fox_v), jnp.stack(fox_lf),
            jnp.stack(mla_c), jnp.stack(mla_r))


def kernel(x_prompt, x_sample, mem_prompt, cache_fox_k, cache_fox_v, cache_fox_logf, cache_mla_ckv, cache_mla_krope, cache_mem_k, cache_mem_v, g_mix, g_cross, g_mem, g_ffn, g_final, w_fox_in, b_fox_f, w_fox_out, w_mla_a, g_mla_q, g_mla_kv, w_mla_qb, w_mla_kvb, w_mla_out, w_x_q, w_x_kv, w_x_o, w_ffn_gu, w_ffn_down):
    depth, d = g_mix.shape
    fox_heads = b_fox_f.shape[1]
    x_heads = cache_mem_k.shape[3]
    q_lora = g_mla_q.shape[1]
    kv_lora = g_mla_kv.shape[1]
    mla_heads = w_mla_out.shape[1] // HEAD_DIM
    ff = w_ffn_down.shape[1]
    row = lambda g: [g[i].reshape(1, -1) for i in range(g.shape[0])]
    gu = w_ffn_gu.astype(BF16)
    wts = {
        "g_mix": row(g_mix), "g_cross": row(g_cross), "g_ffn": row(g_ffn), "g_final": g_final.reshape(1, d),
        "g_mla_q": row(g_mla_q), "g_mla_kv": row(g_mla_kv),
        "fox": [_prep_fox(w_fox_in[j], b_fox_f[j], w_fox_out[j], fox_heads) for j in range(w_fox_in.shape[0])],
        "mla": [_prep_mla(w_mla_a[j], w_mla_qb[j], w_mla_kvb[j], w_mla_out[j], q_lora, kv_lora, mla_heads)
                for j in range(w_mla_a.shape[0])],
        "cross": [(w_x_q[i].astype(BF16), w_x_o[i].astype(BF16)) for i in range(depth)],
        "ffn": [(gu[i, :, :ff], gu[i, :, ff:], w_ffn_down[i].astype(BF16)) for i in range(depth)],
        "fox_heads": fox_heads, "x_heads": x_heads, "mla_heads": mla_heads,
        "mla_scale": (HEAD_DIM + MLA_ROPE) ** -0.5,
    }
    bp, n_mem, _ = mem_prompt.shape
    mk, mv = _memory_kv(mem_prompt.reshape(bp * n_mem, d), g_mem.reshape(depth, 1, d), w_x_kv.astype(BF16))
    xw = mk.shape[-1]
    mk = mk.reshape(depth, bp, n_mem, xw)
    mv = mv.reshape(depth, bp, n_mem, xw)
    y_p, fk_p, fv_p, fl_p, mc_p, mr_p = _trunk_prompt(x_prompt, mk, mv, wts)
    past_len = cache_fox_k.shape[2]
    pos_s = past_len + jnp.arange(x_sample.shape[1], dtype=jnp.int32)
    bs = x_sample.shape[0]
    y_s, fk_s, fv_s, fl_s, mc_s, mr_s = _trunk_sample(
        x_sample, pos_s, (cache_fox_k, cache_fox_v, cache_fox_logf), (cache_mla_ckv, cache_mla_krope),
        cache_mem_k.reshape(depth, bs, n_mem, xw), cache_mem_v.reshape(depth, bs, n_mem, xw), wts)
    dh = xw // x_heads
    return (y_p, y_s, fk_p, fv_p, fl_p, mc_p, mr_p,
            mk.reshape(depth, bp, n_mem, x_heads, dh), mv.reshape(depth, bp, n_mem, x_heads, dh),
            fk_s, fv_s, fl_s, mc_s, mr_s)
```

```python
import functools

import jax
import jax.numpy as jnp
from jax import lax
from jax.experimental import pallas as pl
from jax.experimental.pallas import tpu as pltpu

EPS = 1e-6
CHUNK = 64
ROPE_THETA = 10000.0
LANE = 128
HEAD_DIM = 64
PAIR = 2
MLA_ROPE = 32
NEG = -1e30
LOG2E = 1.4426950408889634
VMEM_LIMIT = 56 * 1024 * 1024
BF16 = jnp.bfloat16
F32 = jnp.float32


def _dot(a, b):
    return jnp.dot(a, b, preferred_element_type=F32)


def _dot_nt(a, b):
    return lax.dot_general(a, b, (((1,), (1,)), ((), ())), preferred_element_type=F32)


def _rms(x):
    return x * lax.rsqrt(jnp.mean(x * x, axis=-1, keepdims=True) + EPS)


def _params(*sem):
    return pltpu.CompilerParams(dimension_semantics=sem, vmem_limit_bytes=VMEM_LIMIT)


def _row_tile(m, cap=512):
    t = min(m, cap)
    assert m % t == 0
    return t


def _full(shape):
    return pl.BlockSpec(shape, lambda *_: (0,) * len(shape), pipeline_mode=pl.Buffered(1))


def _log_sigmoid(f):
    return jnp.minimum(f, 0.0) - jnp.log1p(jnp.exp(-jnp.abs(f)))


def _split3(x):
    hi = x.astype(BF16)
    r = x - hi.astype(F32)
    mid = r.astype(BF16)
    lo = (r - mid.astype(F32)).astype(BF16)
    return hi, mid, lo


def _layer_map(imap, layer, *idx):
    return (layer,) + tuple(imap(*idx))


def _stacked_call(body, *, grid, in_specs, args, outs, layer, prev, sem, name, scratch=()):
    out_specs, out_shape, stacked = [], [], []
    for k, (shape, dtype, blk, imap, n_layers) in enumerate(outs):
        if n_layers is None:
            out_specs.append(pl.BlockSpec(blk, imap))
            out_shape.append(jax.ShapeDtypeStruct(shape, dtype))
        else:
            out_specs.append(pl.BlockSpec((None,) + tuple(blk), functools.partial(_layer_map, imap, layer)))
            out_shape.append(jax.ShapeDtypeStruct((n_layers,) + tuple(shape), dtype))
            stacked.append(k)
    aliases = {}
    in_specs = list(in_specs)
    args = list(args)
    if prev is not None:
        for k, arr in zip(stacked, prev):
            aliases[len(args)] = k
            in_specs.append(pl.BlockSpec(memory_space=pl.ANY))
            args.append(arr)
    n_alias = len(aliases)

    def wrapped(*refs):
        n_in = len(args) - n_alias
        body(*refs[:n_in], *refs[n_in + n_alias:])

    res = pl.pallas_call(
        wrapped, grid=grid, in_specs=in_specs, out_specs=out_specs, out_shape=out_shape,
        input_output_aliases=aliases, scratch_shapes=list(scratch),
        compiler_params=_params(*sem), name=name,
    )(*args)
    return res, [res[k] for k in stacked]


def _fox_in_body(x_ref, g_ref, wq, wk, wv, wg, wf, bf_ref, q_out, k_out, v_out, gate_out, lf_out, *, scale, heads):
    h = (_rms(x_ref[...]) * g_ref[...]).astype(BF16)
    q_out[...] = (_dot(h, wq[...]) * scale).astype(BF16)
    k_out[...] = _dot(h, wk[...])
    v_out[...] = _dot(h, wv[...])
    gate_out[...] = jax.nn.sigmoid(_dot(h, wg[...]))
    lf_out[...] = _log_sigmoid(_dot(h, wf[...]) + bf_ref[...])[:, :heads]


def _fox_in(x, g, wq, wk, wv, wg, wf, bf, heads):
    m, d = x.shape
    w = wq.shape[1]
    tm = _row_tile(m)
    row = lambda n: pl.BlockSpec((tm, n), lambda i: (i, 0))
    return pl.pallas_call(
        functools.partial(_fox_in_body, scale=HEAD_DIM ** -0.5, heads=heads),
        grid=(m // tm,),
        in_specs=[row(d), _full((1, d)), _full((d, w)), _full((d, w)), _full((d, w)), _full((d, w)),
                  _full((d, LANE)), _full((1, LANE))],
        out_specs=[row(w), row(w), row(w), row(w), row(heads)],
        out_shape=[jax.ShapeDtypeStruct((m, w), BF16), jax.ShapeDtypeStruct((m, w), F32),
                   jax.ShapeDtypeStruct((m, w), F32), jax.ShapeDtypeStruct((m, w), F32),
                   jax.ShapeDtypeStruct((m, heads), F32)],
        compiler_params=_params("parallel"),
        name="fox_in",
    )(x, g, wq, wk, wv, wg, wf, bf)


def _fox_in_t_body(x_ref, g_ref, wqt, wkt, wvt, wk, wg, wft, bf_ref,
                   qt_out, k_out, kt_out, vt_out, gate_out, lft_out, *, scale):
    h = (_rms(x_ref[...]) * g_ref[...]).astype(BF16)
    qt_out[...] = (_dot_nt(wqt[...], h) * scale).astype(BF16)
    k_out[...] = _dot(h, wk[...]).astype(BF16)
    kt_out[...] = _dot_nt(wkt[...], h)
    vt_out[...] = _dot_nt(wvt[...], h)
    gate_out[...] = jax.nn.sigmoid(_dot(h, wg[...]))
    lft_out[...] = _log_sigmoid(_dot_nt(wft[...], h) + bf_ref[...])


def _fox_in_t(x, g, wts, layer, n_layers, prev):
    wqt, wkt, wvt, wk, wg, wft, bf = wts
    b, t, d = x.shape
    w = wk.shape[1]
    heads = wft.shape[0]
    tm = _row_tile(t)
    rows = lambda bi, i: (bi, i, 0)
    cols = lambda bi, i: (bi, 0, i)
    res, stacked = _stacked_call(
        functools.partial(_fox_in_t_body, scale=HEAD_DIM ** -0.5 * LOG2E),
        grid=(b, t // tm),
        in_specs=[pl.BlockSpec((None, tm, d), rows), _full((1, d)), _full((w, d)), _full((w, d)), _full((w, d)),
                  _full((d, w)), _full((d, w)), _full((heads, d)), _full((heads, 1))],
        args=[x, g, wqt, wkt, wvt, wk, wg, wft, bf],
        outs=[((b, w, t), BF16, (None, w, tm), cols, None),
              ((b, t, w), BF16, (None, tm, w), rows, None),
              ((b, w, t), F32, (None, w, tm), cols, n_layers),
              ((b, w, t), F32, (None, w, tm), cols, n_layers),
              ((b, t, w), F32, (None, tm, w), rows, None),
              ((b, heads, t), F32, (None, heads, tm), cols, n_layers)],
        layer=layer, prev=prev, sem=("parallel", "parallel"), name="fox_in_t")
    qt, k, _, _, gate, _ = res
    return qt, k, gate, stacked


CUM_BLOCK = 256


def _cumsum_lanes_body(lf_ref, out_ref):
    h, t = lf_ref.shape
    r = lax.broadcasted_iota(jnp.int32, (CUM_BLOCK, CUM_BLOCK), 0)
    c = lax.broadcasted_iota(jnp.int32, (CUM_BLOCK, CUM_BLOCK), 1)
    tri = jnp.where(r <= c, 1.0, 0.0).astype(BF16)
    carry = jnp.zeros((h, 1), F32)
    for b in range(t // CUM_BLOCK):
        hi, mid, lo = _split3(lf_ref[:, b * CUM_BLOCK:(b + 1) * CUM_BLOCK])
        out = _dot(hi, tri) + _dot(mid, tri) + _dot(lo, tri) + carry
        out_ref[:, b * CUM_BLOCK:(b + 1) * CUM_BLOCK] = out
        carry = out[:, CUM_BLOCK - 1:CUM_BLOCK]


def _cumsum_lanes(lf, layer=None):
    b, h, t = lf.shape[-3:]
    assert t % CUM_BLOCK == 0
    if layer is None:
        spec = pl.BlockSpec((None, h, t), lambda i: (i, 0, 0))
    else:
        spec = pl.BlockSpec((None, None, h, t), lambda i: (layer, i, 0, 0))
    return pl.pallas_call(
        _cumsum_lanes_body, grid=(b,),
        in_specs=[spec],
        out_specs=pl.BlockSpec((None, h, t), lambda i: (i, 0, 0)),
        out_shape=jax.ShapeDtypeStruct((b, h, t), F32),
        compiler_params=_params("parallel"), name="logf_cumsum_t",
    )(lf)


N_PIECE = 3
TRIP_BLOCKS = 4
V_ROWS = HEAD_DIM + 16


def _attn_prompt_body(*refs, fox, tq):
    if fox:
        qt_ref, k_ref, vt_ref, crow_ref, ccol_ref, gate_ref, o_ref, kx, vx = refs
    else:
        qnt_ref, qrt_ref, kn_ref, krd_ref, vt_ref, o_ref, kx, vx = refs
    hp = pl.program_id(1)
    tk = tq
    t = kx.shape[0]
    ones_lo = PAIR * N_PIECE

    tail_row = lax.broadcasted_iota(jnp.int32, (V_ROWS - HEAD_DIM, t), 0)
    for e in range(PAIR):
        vx[e * V_ROWS:e * V_ROWS + HEAD_DIM, :] = vt_ref[e * HEAD_DIM:(e + 1) * HEAD_DIM, :].astype(BF16)
        vx[e * V_ROWS + HEAD_DIM:(e + 1) * V_ROWS, :] = jnp.where(tail_row == 0, 1.0, 0.0).astype(BF16)
    if fox:
        kx[:, :LANE] = k_ref[...]
        heads = ccol_ref.shape[1]
        hrow = lax.broadcasted_iota(jnp.int32, (heads, LANE), 0)
        lcol = lax.broadcasted_iota(jnp.int32, (heads, LANE), 1)
        ext = jnp.zeros((t, LANE), F32)
        for i, piece in enumerate(_split3(ccol_ref[...] * LOG2E)):
            hit = ((hrow == PAIR * hp) & (lcol == i)) | ((hrow == PAIR * hp + 1) & (lcol == N_PIECE + i))
            ext = ext + _dot(piece, jnp.where(hit, 1.0, 0.0).astype(BF16))
        lane = lax.broadcasted_iota(jnp.int32, (t, LANE), 1)
        ext = jnp.where((lane >= ones_lo) & (lane < ones_lo + N_PIECE), 1.0, ext)
        kx[:, LANE:] = ext.astype(BF16)
    else:
        kx[:, :LANE] = kn_ref[...]
        kx[:, LANE:] = krd_ref[...]

    lax.fori_loop(0, t // tq, functools.partial(_attn_query_block, refs, kx, vx, fox=fox, tq=tq), 0)


def _attn_query_block(refs, kx, vx, qi, _, *, fox, tq):
    if fox:
        qt_ref, _, _, crow_ref, _, gate_ref, o_ref = refs[:7]
    else:
        qnt_ref, qrt_ref, _, _, _, o_ref = refs[:6]
    tk = tq
    ones_lo = PAIR * N_PIECE
    cols = pl.ds(pl.multiple_of(qi * tq, tq), tq)

    sub = lax.broadcasted_iota(jnp.int32, (LANE, tq), 0)
    zero = jnp.zeros((), BF16)
    qs = []
    for e in range(PAIR):
        in_head = (sub >= e * HEAD_DIM) & (sub < (e + 1) * HEAD_DIM)
        if fox:
            ext = jnp.where((sub >= N_PIECE * e) & (sub < N_PIECE * (e + 1)), -1.0, 0.0)
            for i, piece in enumerate(_split3(crow_ref[e:e + 1, cols] * LOG2E)):
                ext = jnp.where(sub == ones_lo + i, piece.astype(F32), ext)
            qs.append(jnp.concatenate([jnp.where(in_head, qt_ref[:, cols], zero), ext.astype(BF16)], axis=0))
        else:
            in_rope = (sub >= e * MLA_ROPE) & (sub < (e + 1) * MLA_ROPE)
            qs.append(jnp.concatenate([jnp.where(in_head, qnt_ref[:, cols], zero),
                                       jnp.where(in_rope, qrt_ref[:, cols], zero)], axis=0))

    key = lax.broadcasted_iota(jnp.int32, (tk, tq), 0)
    qry = lax.broadcasted_iota(jnp.int32, (tk, tq), 1)
    visible = (key <= qry) if fox else ((key // CHUNK) <= (qry // CHUNK))

    def blocks(j0, carry, nblk, diagonal_last):
        starts = [pl.multiple_of((j0 + u) * tk, tk) for u in range(nblk)]
        scores = [[_dot(kx[pl.ds(st, tk), :], qs[e]) for e in range(PAIR)] for st in starts]
        carry = list(carry)
        for u, st in enumerate(starts):
            for e in range(PAIR):
                m, acc = carry[2 * e:2 * e + 2]
                s = scores[u][e]
                if diagonal_last and u == nblk - 1:
                    s = jnp.where(visible, s, NEG)
                m_new = jnp.maximum(m, jnp.max(s, axis=0, keepdims=True))
                p = jnp.exp2(s - m_new).astype(BF16)
                v_j = vx[e * V_ROWS:(e + 1) * V_ROWS, pl.ds(st, tk)]
                carry[2 * e:2 * e + 2] = [m_new, jnp.exp2(m - m_new) * acc + _dot(v_j, p)]
        return tuple(carry)

    init = (jnp.full((1, tq), NEG, F32), jnp.zeros((V_ROWS, tq), F32)) * PAIR
    main = qi // TRIP_BLOCKS
    left = qi - main * TRIP_BLOCKS
    carry = lax.fori_loop(0, main, lambda i, c: blocks(TRIP_BLOCKS * i, c, TRIP_BLOCKS, False), init)
    for n in range(TRIP_BLOCKS):
        carry = lax.fori_loop(0, jnp.where(left == n, 1, 0),
                              lambda _, c, n=n: blocks(qi - n, c, n + 1, True), carry)
    o = jnp.concatenate([acc[:HEAD_DIM] / acc[HEAD_DIM:HEAD_DIM + 1] for acc in carry[1::2]], axis=0).T
    if fox:
        o = o * gate_ref[cols, :]
    o_ref[cols, :] = o.astype(BF16)
    return 0


def _fox_attn_prompt(qt, k, vt_all, layer, cum_t, gate, tq=512):
    b, w, t = qt.shape
    heads = cum_t.shape[1]
    tq = min(tq, t)
    npair = w // LANE
    rowblk = pl.BlockSpec((None, t, LANE), lambda bi, hp: (bi, 0, hp))
    return pl.pallas_call(
        functools.partial(_attn_prompt_body, fox=True, tq=tq),
        grid=(b, npair),
        in_specs=[pl.BlockSpec((None, LANE, t), lambda bi, hp: (bi, hp, 0)),
                  rowblk,
                  pl.BlockSpec((None, None, LANE, t), lambda bi, hp: (layer, bi, hp, 0)),
                  pl.BlockSpec((None, None, PAIR, t), lambda bi, hp: (bi, hp, 0, 0)),
                  pl.BlockSpec((None, t, heads), lambda bi, hp: (bi, 0, 0)),
                  rowblk],
        out_specs=rowblk,
        out_shape=jax.ShapeDtypeStruct((b, t, w), BF16),
        scratch_shapes=[pltpu.VMEM((t, 2 * LANE), BF16), pltpu.VMEM((PAIR * V_ROWS, t), BF16)],
        compiler_params=_params("parallel", "parallel"),
        name="fox_attn_prompt",
    )(qt, k, vt_all, cum_t.reshape(b, npair, PAIR, t), jnp.swapaxes(cum_t, 1, 2), gate)


def _mla_attn_prompt(qnt, qrt, kn, krd, vt, tq=512):
    b, w, t = qnt.shape
    tq = min(tq, t)
    npair = w // LANE
    colblk = pl.BlockSpec((None, LANE, t), lambda bi, hp: (bi, hp, 0))
    rowblk = pl.BlockSpec((None, t, LANE), lambda bi, hp: (bi, 0, hp))
    return pl.pallas_call(
        functools.partial(_attn_prompt_body, fox=False, tq=tq),
        grid=(b, npair),
        in_specs=[colblk, colblk, rowblk, pl.BlockSpec((None, t, LANE), lambda bi, hp: (bi, 0, 0)), colblk],
        out_specs=rowblk,
        out_shape=jax.ShapeDtypeStruct((b, t, w), BF16),
        scratch_shapes=[pltpu.VMEM((t, 2 * LANE), BF16), pltpu.VMEM((PAIR * V_ROWS, t), BF16)],
        compiler_params=_params("parallel", "parallel"),
        name="mla_attn_prompt",
    )(qnt, qrt, kn, krd, vt)


def _fox_attn_sample_body(q_ref, kt_ref, vt_ref, kn_ref, vn_ref, cq_ref, ck_ref, gate_ref, o_ref):
    hp = pl.program_id(1)
    t = q_ref.shape[0]
    p = kt_ref.shape[1]
    rows = PAIR * t
    lane = lax.broadcasted_iota(jnp.int32, (rows, LANE), 1)
    row = lax.broadcasted_iota(jnp.int32, (rows, LANE), 0)
    q2 = jnp.concatenate([q_ref[...]] * PAIR, axis=0)
    q2 = jnp.where((lane // HEAD_DIM) == (row // t), q2, jnp.zeros((), BF16))
    hl = lax.broadcasted_iota(jnp.int32, cq_ref.shape, 1)
    cq = jnp.concatenate([jnp.sum(jnp.where(hl == PAIR * hp + e, cq_ref[...], 0.0), axis=1, keepdims=True)
                          for e in range(PAIR)], axis=0)
    first_p = lax.broadcasted_iota(jnp.int32, (rows, p), 0) < t
    rn = lax.broadcasted_iota(jnp.int32, (rows, t), 0)
    cn = lax.broadcasted_iota(jnp.int32, (rows, t), 1)
    s_p = _dot(q2, kt_ref[...].astype(BF16)) + cq - jnp.where(first_p, ck_ref[0:1, :p], ck_ref[1:2, :p])
    s_n = _dot_nt(q2, kn_ref[...].astype(BF16)) + cq - jnp.where(rn < t, ck_ref[0:1, p:], ck_ref[1:2, p:])
    s_n = jnp.where(cn <= lax.rem(rn, t), s_n, NEG)
    m = jnp.maximum(jnp.max(s_p, axis=-1, keepdims=True), jnp.max(s_n, axis=-1, keepdims=True))
    p_p = jnp.exp(s_p - m)
    p_n = jnp.exp(s_n - m)
    l = jnp.sum(p_p, axis=-1, keepdims=True) + jnp.sum(p_n, axis=-1, keepdims=True)
    o = (_dot_nt(p_p.astype(BF16), vt_ref[...].astype(BF16)) + _dot(p_n.astype(BF16), vn_ref[...].astype(BF16))) / l
    o = jnp.where(lax.broadcasted_iota(jnp.int32, (t, LANE), 1) < HEAD_DIM, o[:t], o[t:]) * gate_ref[...]
    o_ref[...] = o.astype(BF16)


def _fox_attn_sample(q, kt_cache, vt_cache, layer, k_new, v_new, cum_t, gate):
    b, t, w = q.shape
    p = kt_cache.shape[-1]
    heads = cum_t.shape[1]
    npair = w // LANE
    new = pl.BlockSpec((None, t, LANE), lambda bi, hp: (bi, 0, hp))
    past = pl.BlockSpec((None, None, LANE, p), lambda bi, hp: (layer, bi, hp, 0))
    return pl.pallas_call(
        _fox_attn_sample_body,
        grid=(b, npair),
        in_specs=[new, past, past, new, new,
                  pl.BlockSpec((None, t, heads), lambda bi, hp: (bi, 0, 0)),
                  pl.BlockSpec((None, None, PAIR, p + t), lambda bi, hp: (bi, hp, 0, 0)),
                  new],
        out_specs=new,
        out_shape=jax.ShapeDtypeStruct((b, t, w), BF16),
        compiler_params=_params("parallel", "parallel"),
        name="fox_attn_sample",
    )(q, kt_cache, vt_cache, k_new, v_new, jnp.swapaxes(cum_t[:, :, p:], 1, 2),
      cum_t.reshape(b, npair, PAIR, p + t), gate)


def _mla_attn_sample_body(qn_ref, qr_ref, cp_ref, krp_ref, cn_ref, krn_ref, wkn_ref, wv_ref, o_ref, *, past_len):
    t, w = qn_ref.shape
    p = cp_ref.shape[0]
    heads = w // HEAD_DIM
    rows = heads * t
    lane = lax.broadcasted_iota(jnp.int32, (rows, w), 1)
    row = lax.broadcasted_iota(jnp.int32, (rows, w), 0)
    own = (lane // HEAD_DIM) == (row // t)
    q_wide = jnp.where(own, jnp.concatenate([qn_ref[...]] * heads, axis=0), jnp.zeros((), BF16))
    q_lat = _dot_nt(q_wide, wkn_ref[...]).astype(BF16)
    qr = qr_ref[...]
    c_p = cp_ref[...].astype(BF16)
    c_n = cn_ref[...].astype(BF16)
    s_p = _dot_nt(q_lat, c_p) + _dot(qr, krp_ref[...].astype(BF16))
    s_n = _dot_nt(q_lat, c_n) + _dot_nt(qr, krn_ref[...].astype(BF16))
    q_chunk_p = (past_len + lax.rem(lax.broadcasted_iota(jnp.int32, (rows, p), 0), t)) // CHUNK
    s_p = jnp.where((lax.broadcasted_iota(jnp.int32, (rows, p), 1) // CHUNK) <= q_chunk_p, s_p, NEG)
    q_chunk_n = (past_len + lax.rem(lax.broadcasted_iota(jnp.int32, (rows, t), 0), t)) // CHUNK
    s_n = jnp.where(((past_len + lax.broadcasted_iota(jnp.int32, (rows, t), 1)) // CHUNK) <= q_chunk_n, s_n, NEG)
    m = jnp.maximum(jnp.max(s_p, axis=-1, keepdims=True), jnp.max(s_n, axis=-1, keepdims=True))
    p_p = jnp.exp(s_p - m)
    p_n = jnp.exp(s_n - m)
    l = jnp.sum(p_p, axis=-1, keepdims=True) + jnp.sum(p_n, axis=-1, keepdims=True)
    o_lat = (_dot(p_p.astype(BF16), c_p) + _dot(p_n.astype(BF16), c_n)) / l
    o_wide = jnp.where(own, _dot(o_lat.astype(BF16), wv_ref[...]), 0.0)
    o = o_wide[:t]
    for h in range(1, heads):
        o = o + o_wide[h * t:(h + 1) * t]
    o_ref[...] = o.astype(BF16)


def _mla_attn_sample(qn, qr_rows, ckv_cache, krt_cache, layer, ckv_new, kr_new, wkn, wv):
    b, t, w = qn.shape
    p, c = ckv_cache.shape[2:]
    rows = qr_rows.shape[1]
    blk = lambda *s: pl.BlockSpec((None,) + s, lambda bi: (bi,) + (0,) * len(s))
    past = lambda *s: pl.BlockSpec((None, None) + s, lambda bi: (layer, bi) + (0,) * len(s))
    return pl.pallas_call(
        functools.partial(_mla_attn_sample_body, past_len=p),
        grid=(b,),
        in_specs=[blk(t, w), blk(rows, MLA_ROPE), past(p, c), past(MLA_ROPE, p), blk(t, c), blk(t, MLA_ROPE),
                  _full(wkn.shape), _full(wv.shape)],
        out_specs=blk(t, w),
        out_shape=jax.ShapeDtypeStruct((b, t, w), BF16),
        compiler_params=_params("parallel"),
        name="mla_attn_sample",
    )(qn, qr_rows, ckv_cache, krt_cache, ckv_new, kr_new, wkn, wv)


def _mla_in_body(x_ref, g_ref, w1, gq_ref, gkv_ref, w2, cos_ref, sin_ref, cosq_ref, sinq_ref,
                 qn_out, qr_out, ckv_out, kr_out, *, q_lora, kv_lora, scale):
    h = (_rms(x_ref[...]) * g_ref[...]).astype(BF16)
    a = _dot(h, w1[...])
    c_q = (_rms(a[:, :q_lora]) * gq_ref[...]).astype(BF16)
    ckv_out[...] = _rms(a[:, q_lora:q_lora + kv_lora]) * gkv_ref[...]
    o = q_lora + kv_lora
    krd = a[:, o:o + LANE] * cos_ref[...] + a[:, o + LANE:o + 2 * LANE] * sin_ref[...]
    kr_out[...] = krd[:, :MLA_ROPE]
    w = qn_out.shape[1]
    r = qr_out.shape[1]
    qn_out[...] = (_dot(c_q, w2[:, :w]) * scale).astype(BF16)
    qr = _dot(c_q, w2[:, w:w + r]) * cosq_ref[...] + _dot(c_q, w2[:, w + r:w + 2 * r]) * sinq_ref[...]
    qr_out[...] = (qr * scale).astype(BF16)


def _mla_in(x, g, w1, gq, gkv, w2, cos, sin, heads, scale):
    m, d = x.shape
    q_lora = gq.shape[1]
    kv_lora = gkv.shape[1]
    w = heads * HEAD_DIM
    r = heads * MLA_ROPE
    tm = _row_tile(m)
    t = cos.shape[0]
    assert tm % t == 0
    per_head = lambda tab: jnp.tile(tab[:, :MLA_ROPE], (tm // t, heads))
    cosq, sinq = per_head(cos), per_head(sin)
    cos, sin = jnp.tile(cos, (tm // t, 1)), jnp.tile(sin, (tm // t, 1))
    row = lambda n: pl.BlockSpec((tm, n), lambda i: (i, 0))
    return pl.pallas_call(
        functools.partial(_mla_in_body, q_lora=q_lora, kv_lora=kv_lora, scale=scale),
        grid=(m // tm,),
        in_specs=[row(d), _full((1, d)), _full(w1.shape), _full((1, q_lora)), _full((1, kv_lora)),
                  _full(w2.shape), _full((tm, LANE)), _full((tm, LANE)), _full((tm, r)), _full((tm, r))],
        out_specs=[row(w), row(r), row(kv_lora), row(MLA_ROPE)],
        out_shape=[jax.ShapeDtypeStruct((m, w), BF16), jax.ShapeDtypeStruct((m, r), BF16),
                   jax.ShapeDtypeStruct((m, kv_lora), F32), jax.ShapeDtypeStruct((m, MLA_ROPE), F32)],
        compiler_params=_params("parallel"),
        name="mla_in",
    )(x, g, w1, gq, gkv, w2, cos, sin, cosq, sinq)


def _mla_in_t_body(x_ref, g_ref, w1, gq_ref, gkv_ref, wqnt, wqat, wqbt, wkn, wvt, wkrt,
                   cos_ref, sin_ref, cost_ref, sint_ref,
                   qnt_out, qrt_out, kn_out, krd_out, vt_out, ckv_out, krt_out, *, q_lora, kv_lora, scale):
    h = (_rms(x_ref[...]) * g_ref[...]).astype(BF16)
    a = _dot(h, w1[...])
    c_q = (_rms(a[:, :q_lora]) * gq_ref[...]).astype(BF16)
    c_kv = _rms(a[:, q_lora:q_lora + kv_lora]) * gkv_ref[...]
    ckv_out[...] = c_kv
    o = q_lora + kv_lora
    krd_out[...] = (a[:, o:o + LANE] * cos_ref[...] + a[:, o + LANE:o + 2 * LANE] * sin_ref[...]).astype(BF16)
    cost = cost_ref[...]
    sint = sint_ref[...]
    kab = _dot_nt(wkrt[...], h)
    krt_out[...] = kab[:MLA_ROPE] * cost[:MLA_ROPE] + kab[MLA_ROPE:] * sint[:MLA_ROPE]
    c_kv = c_kv.astype(BF16)
    kn_out[...] = _dot(c_kv, wkn[...]).astype(BF16)
    vt_out[...] = _dot_nt(wvt[...], c_kv).astype(BF16)
    qnt_out[...] = (_dot_nt(wqnt[...], c_q) * scale).astype(BF16)
    for p in range(qrt_out.shape[0] // LANE):
        rows = slice(p * LANE, (p + 1) * LANE)
        qa = _dot_nt(wqat[rows, :], c_q)
        qb = _dot_nt(wqbt[rows, :], c_q)
        qrt_out[rows, :] = ((qa * cost + qb * sint) * scale).astype(BF16)


def _mla_in_t(x, g, gq, gkv, wts, tables, scale, layer, n_layers, prev):
    w1, wqnt, wqat, wqbt, wkn, wvt, wkrt = wts
    cos, sin, cost, sint = tables
    b, t, d = x.shape
    q_lora = gq.shape[1]
    kv_lora = gkv.shape[1]
    w = wkn.shape[1]
    tm = _row_tile(t)
    rows = lambda bi, i: (bi, i, 0)
    cols = lambda bi, i: (bi, 0, i)
    res, stacked = _stacked_call(
        functools.partial(_mla_in_t_body, q_lora=q_lora, kv_lora=kv_lora, scale=scale),
        grid=(b, t // tm),
        in_specs=[pl.BlockSpec((None, tm, d), rows), _full((1, d)), _full(w1.shape), _full((1, q_lora)),
                  _full((1, kv_lora)), _full(wqnt.shape), _full(wqat.shape), _full(wqbt.shape), _full(wkn.shape),
                  _full(wvt.shape), _full(wkrt.shape),
                  pl.BlockSpec((tm, LANE), lambda bi, i: (i, 0)), pl.BlockSpec((tm, LANE), lambda bi, i: (i, 0)),
                  pl.BlockSpec((LANE, tm), lambda bi, i: (0, i)), pl.BlockSpec((LANE, tm), lambda bi, i: (0, i))],
        args=[x, g, w1, gq, gkv, wqnt, wqat, wqbt, wkn, wvt, wkrt, cos, sin, cost, sint],
        outs=[((b, w, t), BF16, (None, w, tm), cols, None),
              ((b, w, t), BF16, (None, w, tm), cols, None),
              ((b, t, w), BF16, (None, tm, w), rows, None),
              ((b, t, LANE), BF16, (None, tm, LANE), rows, None),
              ((b, w, t), BF16, (None, w, tm), cols, None),
              ((b, t, kv_lora), F32, (None, tm, kv_lora), rows, n_layers),
              ((b, MLA_ROPE, t), F32, (None, MLA_ROPE, tm), cols, n_layers)],
        layer=layer, prev=prev, sem=("parallel", "parallel"), name="mla_in_t")
    return res[:5], stacked


def _memory_kv_body(m_ref, g_ref, w_ref, k_out, v_out):
    h = (_rms(m_ref[...]) * g_ref[...]).astype(BF16)
    n = k_out.shape[-1]
    k_out[...] = _dot(h, w_ref[:, :n])
    v_out[...] = _dot(h, w_ref[:, n:])


def _memory_kv(mem, g_mem, w_kv):
    m, d = mem.shape
    depth, _, n2 = w_kv.shape
    n = n2 // 2
    tm = _row_tile(m)
    out = pl.BlockSpec((None, tm, n), lambda l, i: (l, i, 0))
    return pl.pallas_call(
        _memory_kv_body, grid=(depth, m // tm),
        in_specs=[pl.BlockSpec((tm, d), lambda l, i: (i, 0)),
                  pl.BlockSpec((None, 1, d), lambda l, i: (l, 0, 0)),
                  pl.BlockSpec((None, d, n2), lambda l, i: (l, 0, 0))],
        out_specs=[out, out],
        out_shape=[jax.ShapeDtypeStruct((depth, m, n), F32)] * 2,
        compiler_params=_params("parallel", "parallel"), name="memory_kv",
    )(mem, g_mem, w_kv)


FF_BLOCK = 256


def _layer_tail_body(o_ref, x_ref, wout, gc_ref, wq, mk_ref, mv_ref, wxo, gf_ref, wg, wu, wd, *rest, heads, final):
    if final:
        gl_ref, x_out, y_out, att = rest
    else:
        x_out, att = rest
    bb, tm, d = x_ref.shape
    x = x_ref[...].reshape(bb * tm, d) + _dot(o_ref[...].reshape(bb * tm, o_ref.shape[-1]), wout[...])
    h = (_rms(x) * gc_ref[...]).astype(BF16)
    dh = wq.shape[1] // heads
    q = (_dot(h, wq[...]) * dh ** -0.5).astype(BF16)
    for b in range(bb):
        for hd in range(heads):
            cols = slice(hd * dh, (hd + 1) * dh)
            q_h = q[b * tm:(b + 1) * tm, cols]
            if mk_ref.shape[-1] == wq.shape[1]:
                k_parts, v_parts = [mk_ref[b, :, cols]], [mv_ref[b, :, cols]]
            else:
                parts = dh // LANE
                pick = lambda ref, c: ref[b, pl.ds(c * heads + hd, ref.shape[1] // (heads * parts),
                                                   stride=heads * parts), :]
                k_parts = [pick(mk_ref, c) for c in range(parts)]
                v_parts = [pick(mv_ref, c) for c in range(parts)]
            width = dh // len(k_parts)
            s = sum(_dot_nt(q_h[:, c * width:(c + 1) * width], kp.astype(BF16)) for c, kp in enumerate(k_parts))
            p = jnp.exp(s - jnp.max(s, axis=-1, keepdims=True))
            pb = p.astype(BF16)
            o = jnp.concatenate([_dot(pb, vp.astype(BF16)) for vp in v_parts], axis=1)
            o = o / jnp.sum(p, axis=-1, keepdims=True)
            att[b * tm:(b + 1) * tm, cols] = o.astype(BF16)
    x = x + _dot(att[...], wxo[...])
    h = (_rms(x) * gf_ref[...]).astype(BF16)
    for c in range(wg.shape[1] // FF_BLOCK):
        cols = slice(c * FF_BLOCK, (c + 1) * FF_BLOCK)
        gate = _dot(h, wg[:, cols])
        up = _dot(h, wu[:, cols])
        x = x + _dot((gate * jax.nn.sigmoid(gate) * up).astype(BF16), wd[cols, :])
    x_out[...] = x.reshape(bb, tm, d)
    if final:
        y_out[...] = (_rms(x) * gl_ref[...]).reshape(bb, tm, d)


def _layer_tail(o, x, w_out, g_cross, wq, mk, mv, layer, wxo, g_ffn, wg, wu, wd, heads, g_final=None):
    b, t, d = x.shape
    w = o.shape[-1]
    xw = wq.shape[1]
    ff = wg.shape[1]
    assert ff % FF_BLOCK == 0
    tm = min(t, 512)
    bb = max(1, min(b, 64 // tm))
    blk = lambda c: pl.BlockSpec((bb, tm, c), lambda bi, ti: (bi, ti, 0))
    mblk = pl.BlockSpec((None, bb) + mk.shape[2:], lambda bi, ti: (layer, bi, 0, 0))
    final = g_final is not None
    ins = [blk(w), blk(d), _full((w, d)), _full((1, d)), _full((d, xw)), mblk, mblk, _full((xw, d)),
           _full((1, d)), _full((d, ff)), _full((d, ff)), _full((ff, d))]
    args = [o, x, w_out, g_cross, wq, mk, mv, wxo, g_ffn, wg, wu, wd]
    if final:
        ins.append(_full((1, d)))
        args.append(g_final)
    out = jax.ShapeDtypeStruct((b, t, d), F32)
    return pl.pallas_call(
        functools.partial(_layer_tail_body, heads=heads, final=final),
        grid=(b // bb, t // tm),
        in_specs=ins,
        out_specs=[blk(d), blk(d)] if final else blk(d),
        out_shape=[out, out] if final else out,
        scratch_shapes=[pltpu.VMEM((bb * tm, xw), BF16)],
        compiler_params=_params("parallel", "parallel"), name="layer_tail",
    )(*args)


def _prep_fox(w_in, b_f, w_out, heads):
    width = w_out.shape[0]
    w_in = w_in.astype(BF16)
    wq, wk, wv, wf, wg = jnp.split(w_in, [width, 2 * width, 3 * width, 3 * width + heads], axis=1)
    rows = (wq, wk, wv, wg, jnp.pad(wf, ((0, 0), (0, LANE - heads))),
            jnp.pad(b_f, (0, LANE - heads)).reshape(1, LANE))
    cols = (wq.T, wk.T, wv.T, wk, wg, wf.T, b_f.reshape(heads, 1))
    return {"rows": rows, "cols": cols, "out": w_out.astype(BF16)}


def _prep_mla(w_a, w_qb, w_kvb, w_out, q_lora, kv_lora, heads):
    half = MLA_ROPE // 2
    w_a = w_a.astype(BF16)
    d = w_a.shape[0]
    x1 = w_a[:, q_lora + kv_lora:q_lora + kv_lora + half]
    x2 = w_a[:, q_lora + kv_lora + half:]
    zeros = jnp.zeros((d, LANE - 2 * MLA_ROPE), BF16)
    w1 = jnp.concatenate([w_a[:, :q_lora + kv_lora], x1, x2, x1, x2, zeros, x2, x1, x2, x1, zeros], axis=1)
    wkrt = jnp.concatenate([x1, x2, x2, x1], axis=1).T
    qb = w_qb.astype(BF16).reshape(q_lora, heads, HEAD_DIM + MLA_ROPE)
    wqn = qb[:, :, :HEAD_DIM].reshape(q_lora, heads * HEAD_DIM)
    rope = qb[:, :, HEAD_DIM:]
    swapped = jnp.concatenate([rope[..., half:], rope[..., :half]], axis=-1)

    def pack(r):
        r = r.reshape(q_lora, heads // PAIR, PAIR * MLA_ROPE)
        return jnp.pad(r, ((0, 0), (0, 0), (0, LANE - PAIR * MLA_ROPE))).reshape(q_lora, -1)

    wqa, wqb = pack(rope), pack(swapped)
    kvb = w_kvb.astype(BF16).reshape(kv_lora, heads, 2 * HEAD_DIM)
    wkn = kvb[:, :, :HEAD_DIM].reshape(kv_lora, heads * HEAD_DIM)
    wv = kvb[:, :, HEAD_DIM:].reshape(kv_lora, heads * HEAD_DIM)
    flat = lambda r: r.reshape(q_lora, heads * MLA_ROPE)
    rows = (w1, jnp.concatenate([wqn, flat(rope), flat(swapped)], axis=1), wkn, wv)
    cols = (w1, wqn.T, wqa.T, wqb.T, wkn, wv.T, wkrt)
    return {"rows": rows, "cols": cols, "out": w_out.astype(BF16)}


def _rope_tables(pos):
    half = MLA_ROPE // 2
    inv = ROPE_THETA ** (-jnp.arange(half, dtype=F32) / half)
    ang = pos.astype(F32)[:, None] * inv[None, :]
    cos, sin = jnp.cos(ang), jnp.sin(ang)
    z = jnp.zeros((pos.shape[0], LANE - 2 * MLA_ROPE), F32)
    return (jnp.concatenate([cos, cos, cos, cos, z], axis=1),
            jnp.concatenate([-sin, sin, -sin, sin, z], axis=1))


def _after_mixer(x, o, w_out, i, mem_k, mem_v, wts):
    wxq, wxo = wts["cross"][i]
    wgt, wup, wdn = wts["ffn"][i]
    last = i == len(wts["ffn"]) - 1
    res = _layer_tail(o, x, w_out, wts["g_cross"][i], wxq, mem_k, mem_v, i, wxo, wts["g_ffn"][i], wgt, wup, wdn,
                      wts["x_heads"], wts["g_final"] if last else None)
    return res if last else (res, None)


def _trunk_prompt(x, mem_k, mem_v, wts):
    b, t, d = x.shape
    m = b * t
    depth = len(wts["ffn"])
    n_fox, n_mla = len(wts["fox"]), len(wts["mla"])
    cos, sin = _rope_tables(jnp.arange(t, dtype=jnp.int32))
    tables = (cos, sin, cos.T, sin.T)
    fox_state = mla_state = None
    for i in range(depth):
        j = i // 2
        if i % 2 == 0:
            wf = wts["fox"][j]
            qt, k, gate, fox_state = _fox_in_t(x, wts["g_mix"][i], wf["cols"], j, n_fox, fox_state)
            cum_t = _cumsum_lanes(fox_state[2], j)
            o = _fox_attn_prompt(qt, k, fox_state[1], j, cum_t, gate)
        else:
            wf = wts["mla"][j]
            (qnt, qrt, kn, krd, vt), mla_state = _mla_in_t(
                x, wts["g_mix"][i], wts["g_mla_q"][j], wts["g_mla_kv"][j], wf["cols"], tables,
                wts["mla_scale"] * LOG2E, j, n_mla, mla_state)
            o = _mla_attn_prompt(qnt, qrt, kn, krd, vt)
        x, y = _after_mixer(x, o, wf["out"], i, mem_k, mem_v, wts)
    kt, vt, lft = fox_state
    ckv, krt = mla_state
    heads = lft.shape[2]
    unfold = lambda a: jnp.transpose(a.reshape(n_fox, b, heads, a.shape[2] // heads, t), (0, 1, 4, 2, 3))
    return (y.reshape(b, t, d), unfold(kt), unfold(vt), jnp.swapaxes(lft, 2, 3), ckv, jnp.swapaxes(krt, 2, 3))


def _trunk_sample(x, pos, fox_past, mla_past, mem_k, mem_v, wts):
    b, t, d = x.shape
    m = b * t
    depth = len(wts["ffn"])
    cos, sin = _rope_tables(pos)
    heads = wts["fox_heads"]
    mla_heads = wts["mla_heads"]
    fox_k, fox_v, fox_lf, mla_c, mla_r = [], [], [], [], []
    r3 = lambda a: a.reshape(b, t, a.shape[-1])
    n_fox, _, p, _, hd = fox_past[0].shape
    kt_cache = jnp.transpose(fox_past[0], (0, 1, 3, 4, 2)).reshape(n_fox, b, heads * hd, p)
    vt_cache = jnp.transpose(fox_past[1], (0, 1, 3, 4, 2)).reshape(n_fox, b, heads * hd, p)
    lft_cache = jnp.swapaxes(fox_past[2], 2, 3)
    krt_cache = jnp.swapaxes(mla_past[1], 2, 3)
    total = p + t
    pad = jnp.zeros((b, heads, -(-total // CUM_BLOCK) * CUM_BLOCK - total), F32)
    for i in range(depth):
        j = i // 2
        x2 = x.reshape(m, d)
        if i % 2 == 0:
            wf = wts["fox"][j]
            q, k, v, gate, lf = _fox_in(x2, wts["g_mix"][i], *wf["rows"], heads)
            lf_all = jnp.concatenate([lft_cache[j], jnp.swapaxes(r3(lf), 1, 2), pad], axis=2)
            cum_t = _cumsum_lanes(lf_all)[:, :, :total]
            o = _fox_attn_sample(r3(q), kt_cache, vt_cache, j, r3(k), r3(v), cum_t, r3(gate))
            fox_k.append(k.reshape(b, t, heads, hd))
            fox_v.append(v.reshape(b, t, heads, hd))
            fox_lf.append(lf.reshape(b, t, heads))
        else:
            wf = wts["mla"][j]
            w1, w2, wkn, wv = wf["rows"]
            qn, qr, ckv, kr = _mla_in(x2, wts["g_mix"][i], w1, wts["g_mla_q"][j], wts["g_mla_kv"][j],
                                      w2, cos, sin, mla_heads, wts["mla_scale"])
            qr_rows = jnp.swapaxes(qr.reshape(b, t, mla_heads, MLA_ROPE), 1, 2).reshape(b, mla_heads * t, MLA_ROPE)
            o = _mla_attn_sample(r3(qn), qr_rows, mla_past[0], krt_cache, j, r3(ckv), r3(kr), wkn, wv)
            mla_c.append(r3(ckv))
            mla_r.append(r3(kr))
        x, y = _after_mixer(x, o, wf["out"], i, mem_k, mem_v, wts)
    return (y, jnp.stack(fox_k), jnp.stack(fox_v), jnp.stack(fox_lf),
            jnp.stack(mla_c), jnp.stack(mla_r))


def kernel(x_prompt, x_sample, mem_prompt, cache_fox_k, cache_fox_v, cache_fox_logf, cache_mla_ckv, cache_mla_krope, cache_mem_k, cache_mem_v, g_mix, g_cross, g_mem, g_ffn, g_final, w_fox_in, b_fox_f, w_fox_out, w_mla_a, g_mla_q, g_mla_kv, w_mla_qb, w_mla_kvb, w_mla_out, w_x_q, w_x_kv, w_x_o, w_ffn_gu, w_ffn_down):
    depth, d = g_mix.shape
    fox_heads = b_fox_f.shape[1]
    x_heads = cache_mem_k.shape[3]
    q_lora = g_mla_q.shape[1]
    kv_lora = g_mla_kv.shape[1]
    mla_heads = w_mla_out.shape[1] // HEAD_DIM
    ff = w_ffn_down.shape[1]
    row = lambda g: [g[i].reshape(1, -1) for i in range(g.shape[0])]
    gu = w_ffn_gu.astype(BF16)
    wts = {
        "g_mix": row(g_mix), "g_cross": row(g_cross), "g_ffn": row(g_ffn), "g_final": g_final.reshape(1, d),
        "g_mla_q": row(g_mla_q), "g_mla_kv": row(g_mla_kv),
        "fox": [_prep_fox(w_fox_in[j], b_fox_f[j], w_fox_out[j], fox_heads) for j in range(w_fox_in.shape[0])],
        "mla": [_prep_mla(w_mla_a[j], w_mla_qb[j], w_mla_kvb[j], w_mla_out[j], q_lora, kv_lora, mla_heads)
                for j in range(w_mla_a.shape[0])],
        "cross": [(w_x_q[i].astype(BF16), w_x_o[i].astype(BF16)) for i in range(depth)],
        "ffn": [(gu[i, :, :ff], gu[i, :, ff:], w_ffn_down[i].astype(BF16)) for i in range(depth)],
        "fox_heads": fox_heads, "x_heads": x_heads, "mla_heads": mla_heads,
        "mla_scale": (HEAD_DIM + MLA_ROPE) ** -0.5,
    }
    bp, n_mem, _ = mem_prompt.shape
    mk, mv = _memory_kv(mem_prompt.reshape(bp * n_mem, d), g_mem.reshape(depth, 1, d), w_x_kv.astype(BF16))
    xw = mk.shape[-1]
    mk = mk.reshape(depth, bp, n_mem, xw)
    mv = mv.reshape(depth, bp, n_mem, xw)
    y_p, fk_p, fv_p, fl_p, mc_p, mr_p = _trunk_prompt(x_prompt, mk, mv, wts)
    past_len = cache_fox_k.shape[2]
    pos_s = past_len + jnp.arange(x_sample.shape[1], dtype=jnp.int32)
    bs = x_sample.shape[0]
    dh = xw // x_heads

    def in_memory_order(c):
        c = c.reshape(depth, bs, n_mem, x_heads, dh // LANE, LANE)
        return jnp.transpose(c, (0, 1, 2, 4, 3, 5)).reshape(depth, bs, n_mem * xw // LANE, LANE)

    y_s, fk_s, fv_s, fl_s, mc_s, mr_s = _trunk_sample(
        x_sample, pos_s, (cache_fox_k, cache_fox_v, cache_fox_logf), (cache_mla_ckv, cache_mla_krope),
        in_memory_order(cache_mem_k), in_memory_order(cache_mem_v), wts)
    return (y_p, y_s, fk_p, fv_p, fl_p, mc_p, mr_p,
            mk.reshape(depth, bp, n_mem, x_heads, dh), mv.reshape(depth, bp, n_mem, x_heads, dh),
            fk_s, fv_s, fl_s, mc_s, mr_s)
```

```python
import functools

import jax
import jax.numpy as jnp
from jax import lax
from jax.experimental import pallas as pl
from jax.experimental.pallas import tpu as pltpu

EPS = 1e-6
CHUNK = 64
ROPE_THETA = 10000.0
LANE = 128
HEAD_DIM = 64
PAIR = 2
MLA_ROPE = 32
NEG = -1e30
LOG2E = 1.4426950408889634
VMEM_LIMIT = 56 * 1024 * 1024
BF16 = jnp.bfloat16
F32 = jnp.float32


def _dot(a, b):
    return jnp.dot(a, b, preferred_element_type=F32)


def _dot_nt(a, b):
    return lax.dot_general(a, b, (((1,), (1,)), ((), ())), preferred_element_type=F32)


def _rms(x):
    return x * lax.rsqrt(jnp.mean(x * x, axis=-1, keepdims=True) + EPS)


def _params(*sem):
    return pltpu.CompilerParams(dimension_semantics=sem, vmem_limit_bytes=VMEM_LIMIT)


def _row_tile(m, cap=512):
    t = min(m, cap)
    assert m % t == 0
    return t


def _full(shape):
    return pl.BlockSpec(shape, lambda *_: (0,) * len(shape), pipeline_mode=pl.Buffered(1))


def _log_sigmoid(f):
    return jnp.minimum(f, 0.0) - jnp.log1p(jnp.exp(-jnp.abs(f)))


def _split3(x):
    hi = x.astype(BF16)
    r = x - hi.astype(F32)
    mid = r.astype(BF16)
    lo = (r - mid.astype(F32)).astype(BF16)
    return hi, mid, lo


def _layer_map(imap, layer, *idx):
    return (layer,) + tuple(imap(*idx))


def _stacked_call(body, *, grid, in_specs, args, outs, layer, prev, sem, name, scratch=()):
    out_specs, out_shape, stacked = [], [], []
    for k, (shape, dtype, blk, imap, n_layers) in enumerate(outs):
        if n_layers is None:
            out_specs.append(pl.BlockSpec(blk, imap))
            out_shape.append(jax.ShapeDtypeStruct(shape, dtype))
        else:
            out_specs.append(pl.BlockSpec((None,) + tuple(blk), functools.partial(_layer_map, imap, layer)))
            out_shape.append(jax.ShapeDtypeStruct((n_layers,) + tuple(shape), dtype))
            stacked.append(k)
    aliases = {}
    in_specs = list(in_specs)
    args = list(args)
    if prev is not None:
        for k, arr in zip(stacked, prev):
            aliases[len(args)] = k
            in_specs.append(pl.BlockSpec(memory_space=pl.ANY))
            args.append(arr)
    n_alias = len(aliases)

    def wrapped(*refs):
        n_in = len(args) - n_alias
        body(*refs[:n_in], *refs[n_in + n_alias:])

    res = pl.pallas_call(
        wrapped, grid=grid, in_specs=in_specs, out_specs=out_specs, out_shape=out_shape,
        input_output_aliases=aliases, scratch_shapes=list(scratch),
        compiler_params=_params(*sem), name=name,
    )(*args)
    return res, [res[k] for k in stacked]


def _fox_in_body(x_ref, g_ref, wq, wk, wv, wg, wf, bf_ref, q_out, k_out, v_out, gate_out, lf_out, *, scale, heads):
    h = (_rms(x_ref[...]) * g_ref[...]).astype(BF16)
    q_out[...] = (_dot(h, wq[...]) * scale).astype(BF16)
    k_out[...] = _dot(h, wk[...])
    v_out[...] = _dot(h, wv[...])
    gate_out[...] = jax.nn.sigmoid(_dot(h, wg[...]))
    lf_out[...] = _log_sigmoid(_dot(h, wf[...]) + bf_ref[...])[:, :heads]


def _fox_in(x, g, wq, wk, wv, wg, wf, bf, heads):
    m, d = x.shape
    w = wq.shape[1]
    tm = _row_tile(m)
    row = lambda n: pl.BlockSpec((tm, n), lambda i: (i, 0))
    return pl.pallas_call(
        functools.partial(_fox_in_body, scale=HEAD_DIM ** -0.5, heads=heads),
        grid=(m // tm,),
        in_specs=[row(d), _full((1, d)), _full((d, w)), _full((d, w)), _full((d, w)), _full((d, w)),
                  _full((d, LANE)), _full((1, LANE))],
        out_specs=[row(w), row(w), row(w), row(w), row(heads)],
        out_shape=[jax.ShapeDtypeStruct((m, w), BF16), jax.ShapeDtypeStruct((m, w), F32),
                   jax.ShapeDtypeStruct((m, w), F32), jax.ShapeDtypeStruct((m, w), F32),
                   jax.ShapeDtypeStruct((m, heads), F32)],
        compiler_params=_params("parallel"),
        name="fox_in",
    )(x, g, wq, wk, wv, wg, wf, bf)


def _fox_in_t_body(x_ref, g_ref, wqkvt, wg, wft, bf_ref,
                   qt_out, k_out, kt_out, vt_out, gate_out, lft_out, *, scale):
    h = (_rms(x_ref[...]) * g_ref[...]).astype(BF16)
    w = kt_out.shape[0]
    qkvt = _dot_nt(wqkvt[...], h)
    qt_out[...] = (qkvt[:w] * scale).astype(BF16)
    kt = qkvt[w:2 * w]
    kt_out[...] = kt
    k_out[...] = kt.T.astype(BF16)
    vt_out[...] = qkvt[2 * w:]
    gate_out[...] = jax.nn.sigmoid(_dot(h, wg[...]))
    lft_out[...] = _log_sigmoid(_dot_nt(wft[...], h) + bf_ref[...])


def _fox_in_t(x, g, wts, layer, n_layers, prev):
    wqkvt, wg, wft, bf = wts
    b, t, d = x.shape
    w = wg.shape[1]
    heads = wft.shape[0]
    tm = _row_tile(t)
    rows = lambda bi, i: (bi, i, 0)
    cols = lambda bi, i: (bi, 0, i)
    res, stacked = _stacked_call(
        functools.partial(_fox_in_t_body, scale=HEAD_DIM ** -0.5 * LOG2E),
        grid=(b, t // tm),
        in_specs=[pl.BlockSpec((None, tm, d), rows), _full((1, d)), _full((3 * w, d)),
                  _full((d, w)), _full((heads, d)), _full((heads, 1))],
        args=[x, g, wqkvt, wg, wft, bf],
        outs=[((b, w, t), BF16, (None, w, tm), cols, None),
              ((b, t, w), BF16, (None, tm, w), rows, None),
              ((b, w, t), F32, (None, w, tm), cols, n_layers),
              ((b, w, t), F32, (None, w, tm), cols, n_layers),
              ((b, t, w), F32, (None, tm, w), rows, None),
              ((b, heads, t), F32, (None, heads, tm), cols, n_layers)],
        layer=layer, prev=prev, sem=("parallel", "parallel"), name="fox_in_t")
    qt, k, _, _, gate, _ = res
    return qt, k, gate, stacked


CUM_BLOCK = 256


def _cumsum_lanes_body(lf_ref, out_ref):
    h, t = lf_ref.shape
    r = lax.broadcasted_iota(jnp.int32, (CUM_BLOCK, CUM_BLOCK), 0)
    c = lax.broadcasted_iota(jnp.int32, (CUM_BLOCK, CUM_BLOCK), 1)
    tri = jnp.where(r <= c, 1.0, 0.0).astype(BF16)
    carry = jnp.zeros((h, 1), F32)
    for b in range(t // CUM_BLOCK):
        hi, mid, lo = _split3(lf_ref[:, b * CUM_BLOCK:(b + 1) * CUM_BLOCK])
        out = _dot(hi, tri) + _dot(mid, tri) + _dot(lo, tri) + carry
        out_ref[:, b * CUM_BLOCK:(b + 1) * CUM_BLOCK] = out
        carry = out[:, CUM_BLOCK - 1:CUM_BLOCK]


def _cumsum_lanes(lf, layer=None):
    b, h, t = lf.shape[-3:]
    assert t % CUM_BLOCK == 0
    if layer is None:
        spec = pl.BlockSpec((None, h, t), lambda i: (i, 0, 0))
    else:
        spec = pl.BlockSpec((None, None, h, t), lambda i: (layer, i, 0, 0))
    return pl.pallas_call(
        _cumsum_lanes_body, grid=(b,),
        in_specs=[spec],
        out_specs=pl.BlockSpec((None, h, t), lambda i: (i, 0, 0)),
        out_shape=jax.ShapeDtypeStruct((b, h, t), F32),
        compiler_params=_params("parallel"), name="logf_cumsum_t",
    )(lf)


N_PIECE = 3
TRIP_BLOCKS = 4
V_ROWS = HEAD_DIM + 16


def _attn_prompt_body(*refs, fox, tq):
    if fox:
        qt_ref, k_ref, vt_ref, crow_ref, ccol_ref, gate_ref, o_ref, kx, vx = refs
    else:
        qnt_ref, qrt_ref, kn_ref, krd_ref, vt_ref, o_ref, kx, vx = refs
    hp = pl.program_id(1)
    tk = tq
    t = kx.shape[0]
    ones_lo = PAIR * N_PIECE

    tail_row = lax.broadcasted_iota(jnp.int32, (V_ROWS - HEAD_DIM, t), 0)
    for e in range(PAIR):
        vx[e * V_ROWS:e * V_ROWS + HEAD_DIM, :] = vt_ref[e * HEAD_DIM:(e + 1) * HEAD_DIM, :].astype(BF16)
        vx[e * V_ROWS + HEAD_DIM:(e + 1) * V_ROWS, :] = jnp.where(tail_row == 0, 1.0, 0.0).astype(BF16)
    if fox:
        kx[:, :LANE] = k_ref[...]
        heads = ccol_ref.shape[1]
        hrow = lax.broadcasted_iota(jnp.int32, (heads, LANE), 0)
        lcol = lax.broadcasted_iota(jnp.int32, (heads, LANE), 1)
        ext = jnp.zeros((t, LANE), F32)
        for i, piece in enumerate(_split3(ccol_ref[...] * LOG2E)):
            hit = ((hrow == PAIR * hp) & (lcol == i)) | ((hrow == PAIR * hp + 1) & (lcol == N_PIECE + i))
            ext = ext + _dot(piece, jnp.where(hit, 1.0, 0.0).astype(BF16))
        lane = lax.broadcasted_iota(jnp.int32, (t, LANE), 1)
        ext = jnp.where((lane >= ones_lo) & (lane < ones_lo + N_PIECE), 1.0, ext)
        kx[:, LANE:] = ext.astype(BF16)
    else:
        kx[:, :LANE] = kn_ref[...]
        kx[:, LANE:] = krd_ref[...]

    lax.fori_loop(0, t // tq, functools.partial(_attn_query_block, refs, kx, vx, fox=fox, tq=tq), 0)


def _attn_query_block(refs, kx, vx, qi, _, *, fox, tq):
    if fox:
        qt_ref, _, _, crow_ref, _, gate_ref, o_ref = refs[:7]
    else:
        qnt_ref, qrt_ref, _, _, _, o_ref = refs[:6]
    tk = tq
    ones_lo = PAIR * N_PIECE
    cols = pl.ds(pl.multiple_of(qi * tq, tq), tq)

    sub = lax.broadcasted_iota(jnp.int32, (LANE, tq), 0)
    zero = jnp.zeros((), BF16)
    qs = []
    for e in range(PAIR):
        in_head = (sub >= e * HEAD_DIM) & (sub < (e + 1) * HEAD_DIM)
        if fox:
            ext = jnp.where((sub >= N_PIECE * e) & (sub < N_PIECE * (e + 1)), -1.0, 0.0)
            for i, piece in enumerate(_split3(crow_ref[e:e + 1, cols] * LOG2E)):
                ext = jnp.where(sub == ones_lo + i, piece.astype(F32), ext)
            qs.append(jnp.concatenate([jnp.where(in_head, qt_ref[:, cols], zero), ext.astype(BF16)], axis=0))
        else:
            in_rope = (sub >= e * MLA_ROPE) & (sub < (e + 1) * MLA_ROPE)
            qs.append(jnp.concatenate([jnp.where(in_head, qnt_ref[:, cols], zero),
                                       jnp.where(in_rope, qrt_ref[:, cols], zero)], axis=0))

    key = lax.broadcasted_iota(jnp.int32, (tk, tq), 0)
    qry = lax.broadcasted_iota(jnp.int32, (tk, tq), 1)
    visible = (key <= qry) if fox else ((key // CHUNK) <= (qry // CHUNK))

    def blocks(j0, carry, nblk, diagonal_last):
        starts = [pl.multiple_of((j0 + u) * tk, tk) for u in range(nblk)]
        scores = [[_dot(kx[pl.ds(st, tk), :], qs[e]) for e in range(PAIR)] for st in starts]
        carry = list(carry)
        for u, st in enumerate(starts):
            for e in range(PAIR):
                m, acc = carry[2 * e:2 * e + 2]
                s = scores[u][e]
                if diagonal_last and u == nblk - 1:
                    s = jnp.where(visible, s, NEG)
                m_new = jnp.maximum(m, jnp.max(s, axis=0, keepdims=True))
                p = jnp.exp2(s - m_new).astype(BF16)
                v_j = vx[e * V_ROWS:(e + 1) * V_ROWS, pl.ds(st, tk)]
                carry[2 * e:2 * e + 2] = [m_new, jnp.exp2(m - m_new) * acc + _dot(v_j, p)]
        return tuple(carry)

    init = (jnp.full((1, tq), NEG, F32), jnp.zeros((V_ROWS, tq), F32)) * PAIR
    n_q = kx.shape[0] // tq
    carry, left = init, qi
    if n_q > TRIP_BLOCKS:
        main = qi // TRIP_BLOCKS
        left = qi - main * TRIP_BLOCKS
        carry = lax.fori_loop(0, main, lambda i, c: blocks(TRIP_BLOCKS * i, c, TRIP_BLOCKS, False), init)
    for n in range(min(TRIP_BLOCKS, n_q)):
        carry = lax.fori_loop(0, jnp.where(left == n, 1, 0),
                              lambda _, c, n=n: blocks(qi - n, c, n + 1, True), carry)
    o = jnp.concatenate([acc[:HEAD_DIM] / acc[HEAD_DIM:HEAD_DIM + 1] for acc in carry[1::2]], axis=0).T
    if fox:
        o = o * gate_ref[cols, :]
    o_ref[cols, :] = o.astype(BF16)
    return 0


def _fox_attn_prompt(qt, k, vt_all, layer, cum_t, gate, tq=512):
    b, w, t = qt.shape
    heads = cum_t.shape[1]
    tq = min(tq, t)
    npair = w // LANE
    rowblk = pl.BlockSpec((None, t, LANE), lambda bi, hp: (bi, 0, hp))
    return pl.pallas_call(
        functools.partial(_attn_prompt_body, fox=True, tq=tq),
        grid=(b, npair),
        in_specs=[pl.BlockSpec((None, LANE, t), lambda bi, hp: (bi, hp, 0)),
                  rowblk,
                  pl.BlockSpec((None, None, LANE, t), lambda bi, hp: (layer, bi, hp, 0)),
                  pl.BlockSpec((None, None, PAIR, t), lambda bi, hp: (bi, hp, 0, 0)),
                  pl.BlockSpec((None, t, heads), lambda bi, hp: (bi, 0, 0)),
                  rowblk],
        out_specs=rowblk,
        out_shape=jax.ShapeDtypeStruct((b, t, w), BF16),
        scratch_shapes=[pltpu.VMEM((t, 2 * LANE), BF16), pltpu.VMEM((PAIR * V_ROWS, t), BF16)],
        compiler_params=_params("parallel", "parallel"),
        name="fox_attn_prompt",
    )(qt, k, vt_all, cum_t.reshape(b, npair, PAIR, t), jnp.swapaxes(cum_t, 1, 2), gate)


def _mla_attn_prompt(qnt, qrt, kn, krd, vt, tq=512):
    b, w, t = qnt.shape
    tq = min(tq, t)
    npair = w // LANE
    colblk = pl.BlockSpec((None, LANE, t), lambda bi, hp: (bi, hp, 0))
    rowblk = pl.BlockSpec((None, t, LANE), lambda bi, hp: (bi, 0, hp))
    return pl.pallas_call(
        functools.partial(_attn_prompt_body, fox=False, tq=tq),
        grid=(b, npair),
        in_specs=[colblk, colblk, rowblk, pl.BlockSpec((None, t, LANE), lambda bi, hp: (bi, 0, 0)), colblk],
        out_specs=rowblk,
        out_shape=jax.ShapeDtypeStruct((b, t, w), BF16),
        scratch_shapes=[pltpu.VMEM((t, 2 * LANE), BF16), pltpu.VMEM((PAIR * V_ROWS, t), BF16)],
        compiler_params=_params("parallel", "parallel"),
        name="mla_attn_prompt",
    )(qnt, qrt, kn, krd, vt)


SAMPLE_PAIRS = 2


def _fox_attn_sample_body(q_ref, kt_ref, vt_ref, kn_ref, vn_ref, cq_ref, ck_ref, gate_ref, o_ref):
    t = q_ref.shape[0]
    p = kt_ref.shape[1]
    rows = PAIR * t
    lane = lax.broadcasted_iota(jnp.int32, (rows, LANE), 1)
    row = lax.broadcasted_iota(jnp.int32, (rows, LANE), 0)
    hl = lax.broadcasted_iota(jnp.int32, cq_ref.shape, 1)
    first_p = lax.broadcasted_iota(jnp.int32, (rows, p), 0) < t
    rn = lax.broadcasted_iota(jnp.int32, (rows, t), 0)
    cn = lax.broadcasted_iota(jnp.int32, (rows, t), 1)
    for pp in range(SAMPLE_PAIRS):
        hp = pl.program_id(1) * SAMPLE_PAIRS + pp
        lanes = slice(pp * LANE, (pp + 1) * LANE)
        q2 = jnp.concatenate([q_ref[:, lanes]] * PAIR, axis=0)
        q2 = jnp.where((lane // HEAD_DIM) == (row // t), q2, jnp.zeros((), BF16))
        cq = jnp.concatenate([jnp.sum(jnp.where(hl == PAIR * hp + e, cq_ref[...], 0.0), axis=1, keepdims=True)
                              for e in range(PAIR)], axis=0)
        ck = ck_ref[pp]
        s_p = _dot(q2, kt_ref[lanes, :].astype(BF16)) + cq - jnp.where(first_p, ck[0:1, :p], ck[1:2, :p])
        s_n = _dot_nt(q2, kn_ref[:, lanes].astype(BF16)) + cq - jnp.where(rn < t, ck[0:1, p:], ck[1:2, p:])
        s_n = jnp.where(cn <= lax.rem(rn, t), s_n, NEG)
        m = jnp.maximum(jnp.max(s_p, axis=-1, keepdims=True), jnp.max(s_n, axis=-1, keepdims=True))
        p_p = jnp.exp(s_p - m)
        p_n = jnp.exp(s_n - m)
        l = jnp.sum(p_p, axis=-1, keepdims=True) + jnp.sum(p_n, axis=-1, keepdims=True)
        o = (_dot_nt(p_p.astype(BF16), vt_ref[lanes, :].astype(BF16))
             + _dot(p_n.astype(BF16), vn_ref[:, lanes].astype(BF16))) / l
        o = jnp.where(lax.broadcasted_iota(jnp.int32, (t, LANE), 1) < HEAD_DIM, o[:t], o[t:]) * gate_ref[:, lanes]
        o_ref[:, lanes] = o.astype(BF16)


def _fox_attn_sample(q, kt_cache, vt_cache, layer, k_new, v_new, cum_t, gate):
    b, t, w = q.shape
    p = kt_cache.shape[-1]
    heads = cum_t.shape[1]
    npair = w // LANE
    wide = SAMPLE_PAIRS * LANE
    new = pl.BlockSpec((None, t, wide), lambda bi, hp: (bi, 0, hp))
    past = pl.BlockSpec((None, None, wide, p), lambda bi, hp: (layer, bi, hp, 0))
    return pl.pallas_call(
        _fox_attn_sample_body,
        grid=(b, npair // SAMPLE_PAIRS),
        in_specs=[new, past, past, new, new,
                  pl.BlockSpec((None, t, heads), lambda bi, hp: (bi, 0, 0)),
                  pl.BlockSpec((None, SAMPLE_PAIRS, PAIR, p + t), lambda bi, hp: (bi, hp, 0, 0)),
                  new],
        out_specs=new,
        out_shape=jax.ShapeDtypeStruct((b, t, w), BF16),
        compiler_params=_params("parallel", "parallel"),
        name="fox_attn_sample",
    )(q, kt_cache, vt_cache, k_new, v_new, jnp.swapaxes(cum_t[:, :, p:], 1, 2),
      cum_t.reshape(b, npair, PAIR, p + t), gate)


def _mla_attn_sample_body(qn_ref, qr_ref, cp_ref, krp_ref, cn_ref, krn_ref, wkn_ref, wv_ref, o_ref, *, past_len):
    t, w = qn_ref.shape
    p = cp_ref.shape[0]
    heads = w // HEAD_DIM
    rows = heads * t
    lane = lax.broadcasted_iota(jnp.int32, (rows, w), 1)
    row = lax.broadcasted_iota(jnp.int32, (rows, w), 0)
    own = (lane // HEAD_DIM) == (row // t)
    q_wide = jnp.where(own, jnp.concatenate([qn_ref[...]] * heads, axis=0), jnp.zeros((), BF16))
    q_lat = _dot_nt(q_wide, wkn_ref[...]).astype(BF16)
    qr = qr_ref[...]
    c_p = cp_ref[...].astype(BF16)
    c_n = cn_ref[...].astype(BF16)
    s_p = _dot_nt(q_lat, c_p) + _dot(qr, krp_ref[...].astype(BF16))
    s_n = _dot_nt(q_lat, c_n) + _dot_nt(qr, krn_ref[...].astype(BF16))
    q_chunk_p = (past_len + lax.rem(lax.broadcasted_iota(jnp.int32, (rows, p), 0), t)) // CHUNK
    s_p = jnp.where((lax.broadcasted_iota(jnp.int32, (rows, p), 1) // CHUNK) <= q_chunk_p, s_p, NEG)
    q_chunk_n = (past_len + lax.rem(lax.broadcasted_iota(jnp.int32, (rows, t), 0), t)) // CHUNK
    s_n = jnp.where(((past_len + lax.broadcasted_iota(jnp.int32, (rows, t), 1)) // CHUNK) <= q_chunk_n, s_n, NEG)
    m = jnp.maximum(jnp.max(s_p, axis=-1, keepdims=True), jnp.max(s_n, axis=-1, keepdims=True))
    p_p = jnp.exp(s_p - m)
    p_n = jnp.exp(s_n - m)
    l = jnp.sum(p_p, axis=-1, keepdims=True) + jnp.sum(p_n, axis=-1, keepdims=True)
    o_lat = (_dot(p_p.astype(BF16), c_p) + _dot(p_n.astype(BF16), c_n)) / l
    o_wide = jnp.where(own, _dot(o_lat.astype(BF16), wv_ref[...]), 0.0)
    o = o_wide[:t]
    for h in range(1, heads):
        o = o + o_wide[h * t:(h + 1) * t]
    o_ref[...] = o.astype(BF16)


def _mla_attn_sample(qn, qr_rows, ckv_cache, krt_cache, layer, ckv_new, kr_new, wkn, wv):
    b, t, w = qn.shape
    p, c = ckv_cache.shape[2:]
    rows = qr_rows.shape[1]
    blk = lambda *s: pl.BlockSpec((None,) + s, lambda bi: (bi,) + (0,) * len(s))
    past = lambda *s: pl.BlockSpec((None, None) + s, lambda bi: (layer, bi) + (0,) * len(s))
    return pl.pallas_call(
        functools.partial(_mla_attn_sample_body, past_len=p),
        grid=(b,),
        in_specs=[blk(t, w), blk(rows, MLA_ROPE), past(p, c), past(MLA_ROPE, p), blk(t, c), blk(t, MLA_ROPE),
                  _full(wkn.shape), _full(wv.shape)],
        out_specs=blk(t, w),
        out_shape=jax.ShapeDtypeStruct((b, t, w), BF16),
        compiler_params=_params("parallel"),
        name="mla_attn_sample",
    )(qn, qr_rows, ckv_cache, krt_cache, ckv_new, kr_new, wkn, wv)


def _mla_in_body(x_ref, g_ref, w1, gq_ref, gkv_ref, w2, cos_ref, sin_ref, cosq_ref, sinq_ref,
                 qn_out, qr_out, ckv_out, kr_out, *, q_lora, kv_lora, scale):
    h = (_rms(x_ref[...]) * g_ref[...]).astype(BF16)
    a = _dot(h, w1[...])
    c_q = (_rms(a[:, :q_lora]) * gq_ref[...]).astype(BF16)
    ckv_out[...] = _rms(a[:, q_lora:q_lora + kv_lora]) * gkv_ref[...]
    o = q_lora + kv_lora
    krd = a[:, o:o + LANE] * cos_ref[...] + a[:, o + LANE:o + 2 * LANE] * sin_ref[...]
    kr_out[...] = krd[:, :MLA_ROPE]
    w = qn_out.shape[1]
    r = qr_out.shape[1]
    qn_out[...] = (_dot(c_q, w2[:, :w]) * scale).astype(BF16)
    qr = _dot(c_q, w2[:, w:w + r]) * cosq_ref[...] + _dot(c_q, w2[:, w + r:w + 2 * r]) * sinq_ref[...]
    qr_out[...] = (qr * scale).astype(BF16)


def _mla_in(x, g, w1, gq, gkv, w2, cos, sin, heads, scale):
    m, d = x.shape
    q_lora = gq.shape[1]
    kv_lora = gkv.shape[1]
    w = heads * HEAD_DIM
    r = heads * MLA_ROPE
    tm = _row_tile(m)
    t = cos.shape[0]
    assert tm % t == 0
    per_head = lambda tab: jnp.tile(tab[:, :MLA_ROPE], (tm // t, heads))
    cosq, sinq = per_head(cos), per_head(sin)
    cos, sin = jnp.tile(cos, (tm // t, 1)), jnp.tile(sin, (tm // t, 1))
    row = lambda n: pl.BlockSpec((tm, n), lambda i: (i, 0))
    return pl.pallas_call(
        functools.partial(_mla_in_body, q_lora=q_lora, kv_lora=kv_lora, scale=scale),
        grid=(m // tm,),
        in_specs=[row(d), _full((1, d)), _full(w1.shape), _full((1, q_lora)), _full((1, kv_lora)),
                  _full(w2.shape), _full((tm, LANE)), _full((tm, LANE)), _full((tm, r)), _full((tm, r))],
        out_specs=[row(w), row(r), row(kv_lora), row(MLA_ROPE)],
        out_shape=[jax.ShapeDtypeStruct((m, w), BF16), jax.ShapeDtypeStruct((m, r), BF16),
                   jax.ShapeDtypeStruct((m, kv_lora), F32), jax.ShapeDtypeStruct((m, MLA_ROPE), F32)],
        compiler_params=_params("parallel"),
        name="mla_in",
    )(x, g, w1, gq, gkv, w2, cos, sin, cosq, sinq)


def _mla_in_t_body(x_ref, g_ref, w1, gq_ref, gkv_ref, wq3t, wkn, wvt, wkrt,
                   cos_ref, sin_ref, cost_ref, sint_ref,
                   qnt_out, qrt_out, kn_out, krd_out, vt_out, ckv_out, krt_out, *, q_lora, kv_lora, scale):
    h = (_rms(x_ref[...]) * g_ref[...]).astype(BF16)
    a = _dot(h, w1[...])
    c_q = (_rms(a[:, :q_lora]) * gq_ref[...]).astype(BF16)
    c_kv = _rms(a[:, q_lora:q_lora + kv_lora]) * gkv_ref[...]
    ckv_out[...] = c_kv
    o = q_lora + kv_lora
    krd_out[...] = (a[:, o:o + LANE] * cos_ref[...] + a[:, o + LANE:o + 2 * LANE] * sin_ref[...]).astype(BF16)
    cost = cost_ref[...]
    sint = sint_ref[...]
    kab = _dot_nt(wkrt[...], h)
    krt_out[...] = kab[:MLA_ROPE] * cost[:MLA_ROPE] + kab[MLA_ROPE:] * sint[:MLA_ROPE]
    c_kv = c_kv.astype(BF16)
    kn_out[...] = _dot(c_kv, wkn[...]).astype(BF16)
    vt_out[...] = _dot_nt(wvt[...], c_kv).astype(BF16)
    w = qnt_out.shape[0]
    q3 = _dot_nt(wq3t[...], c_q)
    qnt_out[...] = (q3[:w] * scale).astype(BF16)
    for p in range(w // LANE):
        qa = q3[w + p * LANE:w + (p + 1) * LANE]
        qb = q3[2 * w + p * LANE:2 * w + (p + 1) * LANE]
        qrt_out[p * LANE:(p + 1) * LANE, :] = ((qa * cost + qb * sint) * scale).astype(BF16)


def _mla_in_t(x, g, gq, gkv, wts, tables, scale, layer, n_layers, prev):
    w1, wq3t, wkn, wvt, wkrt = wts
    cos, sin, cost, sint = tables
    b, t, d = x.shape
    q_lora = gq.shape[1]
    kv_lora = gkv.shape[1]
    w = wkn.shape[1]
    tm = _row_tile(t)
    rows = lambda bi, i: (bi, i, 0)
    cols = lambda bi, i: (bi, 0, i)
    res, stacked = _stacked_call(
        functools.partial(_mla_in_t_body, q_lora=q_lora, kv_lora=kv_lora, scale=scale),
        grid=(b, t // tm),
        in_specs=[pl.BlockSpec((None, tm, d), rows), _full((1, d)), _full(w1.shape), _full((1, q_lora)),
                  _full((1, kv_lora)), _full(wq3t.shape), _full(wkn.shape), _full(wvt.shape), _full(wkrt.shape),
                  pl.BlockSpec((tm, LANE), lambda bi, i: (i, 0)), pl.BlockSpec((tm, LANE), lambda bi, i: (i, 0)),
                  pl.BlockSpec((LANE, tm), lambda bi, i: (0, i)), pl.BlockSpec((LANE, tm), lambda bi, i: (0, i))],
        args=[x, g, w1, gq, gkv, wq3t, wkn, wvt, wkrt, cos, sin, cost, sint],
        outs=[((b, w, t), BF16, (None, w, tm), cols, None),
              ((b, w, t), BF16, (None, w, tm), cols, None),
              ((b, t, w), BF16, (None, tm, w), rows, None),
              ((b, t, LANE), BF16, (None, tm, LANE), rows, None),
              ((b, w, t), BF16, (None, w, tm), cols, None),
              ((b, t, kv_lora), F32, (None, tm, kv_lora), rows, n_layers),
              ((b, MLA_ROPE, t), F32, (None, MLA_ROPE, tm), cols, n_layers)],
        layer=layer, prev=prev, sem=("parallel", "parallel"), name="mla_in_t")
    return res[:5], stacked


def _memory_kv_body(m_ref, g_ref, w_ref, k_out, v_out):
    h = (_rms(m_ref[...]) * g_ref[...]).astype(BF16)
    n = k_out.shape[-1]
    k_out[...] = _dot(h, w_ref[:, :n])
    v_out[...] = _dot(h, w_ref[:, n:])


def _memory_kv(mem, g_mem, w_kv):
    m, d = mem.shape
    depth, _, n2 = w_kv.shape
    n = n2 // 2
    tm = _row_tile(m)
    out = pl.BlockSpec((None, tm, n), lambda l, i: (l, i, 0))
    return pl.pallas_call(
        _memory_kv_body, grid=(depth, m // tm),
        in_specs=[pl.BlockSpec((tm, d), lambda l, i: (i, 0)),
                  pl.BlockSpec((None, 1, d), lambda l, i: (l, 0, 0)),
                  pl.BlockSpec((None, d, n2), lambda l, i: (l, 0, 0))],
        out_specs=[out, out],
        out_shape=[jax.ShapeDtypeStruct((depth, m, n), F32)] * 2,
        compiler_params=_params("parallel", "parallel"), name="memory_kv",
    )(mem, g_mem, w_kv)


FF_BLOCK = 256
SHORT_SEQ_ROWS = 128


def _layer_tail_body(o_ref, x_ref, wout, gc_ref, wq, mk_ref, mv_ref, wxo, gf_ref, wg, wu, wd, *rest, heads, final):
    if final:
        gl_ref, x_out, y_out, att = rest
    else:
        x_out, att = rest
    bb, tm, d = x_ref.shape
    x = x_ref[...].reshape(bb * tm, d) + _dot(o_ref[...].reshape(bb * tm, o_ref.shape[-1]), wout[...])
    h = (_rms(x) * gc_ref[...]).astype(BF16)
    dh = wq.shape[1] // heads
    q = (_dot(h, wq[...]) * dh ** -0.5).astype(BF16)
    for b in range(bb):
        for hd in range(heads):
            cols = slice(hd * dh, (hd + 1) * dh)
            q_h = q[b * tm:(b + 1) * tm, cols]
            if mk_ref.shape[-1] == wq.shape[1]:
                k_parts, v_parts = [mk_ref[b, :, cols]], [mv_ref[b, :, cols]]
            else:
                parts = dh // LANE
                pick = lambda ref, c: ref[b, pl.ds(c * heads + hd, ref.shape[1] // (heads * parts),
                                                   stride=heads * parts), :]
                k_parts = [pick(mk_ref, c) for c in range(parts)]
                v_parts = [pick(mv_ref, c) for c in range(parts)]
            width = dh // len(k_parts)
            s = sum(_dot_nt(q_h[:, c * width:(c + 1) * width], kp.astype(BF16)) for c, kp in enumerate(k_parts))
            p = jnp.exp(s - jnp.max(s, axis=-1, keepdims=True))
            pb = p.astype(BF16)
            o = jnp.concatenate([_dot(pb, vp.astype(BF16)) for vp in v_parts], axis=1)
            o = o / jnp.sum(p, axis=-1, keepdims=True)
            att[b * tm:(b + 1) * tm, cols] = o.astype(BF16)
    x = x + _dot(att[...], wxo[...])
    h = (_rms(x) * gf_ref[...]).astype(BF16)
    for c in range(wg.shape[1] // FF_BLOCK):
        cols = slice(c * FF_BLOCK, (c + 1) * FF_BLOCK)
        gate = _dot(h, wg[:, cols])
        up = _dot(h, wu[:, cols])
        x = x + _dot((gate * jax.nn.sigmoid(gate) * up).astype(BF16), wd[cols, :])
    x_out[...] = x.reshape(bb, tm, d)
    if final:
        y_out[...] = (_rms(x) * gl_ref[...]).reshape(bb, tm, d)


def _layer_tail(o, x, w_out, g_cross, wq, mk, mv, layer, wxo, g_ffn, wg, wu, wd, heads, g_final=None):
    b, t, d = x.shape
    w = o.shape[-1]
    xw = wq.shape[1]
    ff = wg.shape[1]
    assert ff % FF_BLOCK == 0
    tm = min(t, 512)
    bb = max(1, min(b, SHORT_SEQ_ROWS // tm))
    blk = lambda c: pl.BlockSpec((bb, tm, c), lambda bi, ti: (bi, ti, 0))
    mblk = pl.BlockSpec((None, bb) + mk.shape[2:], lambda bi, ti: (layer, bi, 0, 0),
                        pipeline_mode=pl.Buffered(1) if bb > 1 else None)
    final = g_final is not None
    ins = [blk(w), blk(d), _full((w, d)), _full((1, d)), _full((d, xw)), mblk, mblk, _full((xw, d)),
           _full((1, d)), _full((d, ff)), _full((d, ff)), _full((ff, d))]
    args = [o, x, w_out, g_cross, wq, mk, mv, wxo, g_ffn, wg, wu, wd]
    if final:
        ins.append(_full((1, d)))
        args.append(g_final)
    out = jax.ShapeDtypeStruct((b, t, d), F32)
    return pl.pallas_call(
        functools.partial(_layer_tail_body, heads=heads, final=final),
        grid=(b // bb, t // tm),
        in_specs=ins,
        out_specs=[blk(d), blk(d)] if final else blk(d),
        out_shape=[out, out] if final else out,
        scratch_shapes=[pltpu.VMEM((bb * tm, xw), BF16)],
        compiler_params=_params("parallel", "parallel"), name="layer_tail",
    )(*args)


def _prep_fox(w_in, b_f, w_out, heads):
    width = w_out.shape[0]
    w_in = w_in.astype(BF16)
    wq, wk, wv, wf, wg = jnp.split(w_in, [width, 2 * width, 3 * width, 3 * width + heads], axis=1)
    rows = (wq, wk, wv, wg, jnp.pad(wf, ((0, 0), (0, LANE - heads))),
            jnp.pad(b_f, (0, LANE - heads)).reshape(1, LANE))
    cols = (w_in[:, :3 * width].T, wg, wf.T, b_f.reshape(heads, 1))
    return {"rows": rows, "cols": cols, "out": w_out.astype(BF16)}


def _prep_mla(w_a, w_qb, w_kvb, w_out, q_lora, kv_lora, heads):
    half = MLA_ROPE // 2
    w_a = w_a.astype(BF16)
    d = w_a.shape[0]
    x1 = w_a[:, q_lora + kv_lora:q_lora + kv_lora + half]
    x2 = w_a[:, q_lora + kv_lora + half:]
    zeros = jnp.zeros((d, LANE - 2 * MLA_ROPE), BF16)
    w1 = jnp.concatenate([w_a[:, :q_lora + kv_lora], x1, x2, x1, x2, zeros, x2, x1, x2, x1, zeros], axis=1)
    wkrt = jnp.concatenate([x1, x2, x2, x1], axis=1).T
    qb = w_qb.astype(BF16).reshape(q_lora, heads, HEAD_DIM + MLA_ROPE)
    wqn = qb[:, :, :HEAD_DIM].reshape(q_lora, heads * HEAD_DIM)
    rope = qb[:, :, HEAD_DIM:]
    swapped = jnp.concatenate([rope[..., half:], rope[..., :half]], axis=-1)

    def pack(r):
        r = r.reshape(q_lora, heads // PAIR, PAIR * MLA_ROPE)
        return jnp.pad(r, ((0, 0), (0, 0), (0, LANE - PAIR * MLA_ROPE))).reshape(q_lora, -1)

    wqa, wqb = pack(rope), pack(swapped)
    kvb = w_kvb.astype(BF16).reshape(kv_lora, heads, 2 * HEAD_DIM)
    wkn = kvb[:, :, :HEAD_DIM].reshape(kv_lora, heads * HEAD_DIM)
    wv = kvb[:, :, HEAD_DIM:].reshape(kv_lora, heads * HEAD_DIM)
    flat = lambda r: r.reshape(q_lora, heads * MLA_ROPE)
    rows = (w1, jnp.concatenate([wqn, flat(rope), flat(swapped)], axis=1), wkn, wv)
    cols = (w1, jnp.concatenate([wqn, wqa, wqb], axis=1).T, wkn, wv.T, wkrt)
    return {"rows": rows, "cols": cols, "out": w_out.astype(BF16)}


def _rope_tables(pos):
    half = MLA_ROPE // 2
    inv = ROPE_THETA ** (-jnp.arange(half, dtype=F32) / half)
    ang = pos.astype(F32)[:, None] * inv[None, :]
    cos, sin = jnp.cos(ang), jnp.sin(ang)
    z = jnp.zeros((pos.shape[0], LANE - 2 * MLA_ROPE), F32)
    return (jnp.concatenate([cos, cos, cos, cos, z], axis=1),
            jnp.concatenate([-sin, sin, -sin, sin, z], axis=1))


def _after_mixer(x, o, w_out, i, mem_k, mem_v, wts):
    wxq, wxo = wts["cross"][i]
    wgt, wup, wdn = wts["ffn"][i]
    last = i == len(wts["ffn"]) - 1
    res = _layer_tail(o, x, w_out, wts["g_cross"][i], wxq, mem_k, mem_v, i, wxo, wts["g_ffn"][i], wgt, wup, wdn,
                      wts["x_heads"], wts["g_final"] if last else None)
    return res if last else (res, None)


def _trunk_prompt(x, mem_k, mem_v, wts):
    b, t, d = x.shape
    m = b * t
    depth = len(wts["ffn"])
    n_fox, n_mla = len(wts["fox"]), len(wts["mla"])
    cos, sin = _rope_tables(jnp.arange(t, dtype=jnp.int32))
    tables = (cos, sin, cos.T, sin.T)
    fox_state = mla_state = None
    for i in range(depth):
        j = i // 2
        if i % 2 == 0:
            wf = wts["fox"][j]
            qt, k, gate, fox_state = _fox_in_t(x, wts["g_mix"][i], wf["cols"], j, n_fox, fox_state)
            cum_t = _cumsum_lanes(fox_state[2], j)
            o = _fox_attn_prompt(qt, k, fox_state[1], j, cum_t, gate)
        else:
            wf = wts["mla"][j]
            (qnt, qrt, kn, krd, vt), mla_state = _mla_in_t(
                x, wts["g_mix"][i], wts["g_mla_q"][j], wts["g_mla_kv"][j], wf["cols"], tables,
                wts["mla_scale"] * LOG2E, j, n_mla, mla_state)
            o = _mla_attn_prompt(qnt, qrt, kn, krd, vt)
        x, y = _after_mixer(x, o, wf["out"], i, mem_k, mem_v, wts)
    kt, vt, lft = fox_state
    ckv, krt = mla_state
    heads = lft.shape[2]
    unfold = lambda a: jnp.transpose(a.reshape(n_fox, b, heads, a.shape[2] // heads, t), (0, 1, 4, 2, 3))
    return (y.reshape(b, t, d), unfold(kt), unfold(vt), jnp.swapaxes(lft, 2, 3), ckv, jnp.swapaxes(krt, 2, 3))


def _trunk_sample(x, pos, fox_past, mla_past, mem_k, mem_v, wts):
    b, t, d = x.shape
    m = b * t
    depth = len(wts["ffn"])
    cos, sin = _rope_tables(pos)
    heads = wts["fox_heads"]
    mla_heads = wts["mla_heads"]
    fox_k, fox_v, fox_lf, mla_c, mla_r = [], [], [], [], []
    r3 = lambda a: a.reshape(b, t, a.shape[-1])
    n_fox, _, p, _, hd = fox_past[0].shape
    kt_cache = jnp.transpose(fox_past[0], (0, 1, 3, 4, 2)).reshape(n_fox, b, heads * hd, p)
    vt_cache = jnp.transpose(fox_past[1], (0, 1, 3, 4, 2)).reshape(n_fox, b, heads * hd, p)
    lft_cache = jnp.swapaxes(fox_past[2], 2, 3)
    krt_cache = jnp.swapaxes(mla_past[1], 2, 3)
    total = p + t
    pad = jnp.zeros((b, heads, -(-total // CUM_BLOCK) * CUM_BLOCK - total), F32)
    for i in range(depth):
        j = i // 2
        x2 = x.reshape(m, d)
        if i % 2 == 0:
            wf = wts["fox"][j]
            q, k, v, gate, lf = _fox_in(x2, wts["g_mix"][i], *wf["rows"], heads)
            lf_all = jnp.concatenate([lft_cache[j], jnp.swapaxes(r3(lf), 1, 2), pad], axis=2)
            cum_t = _cumsum_lanes(lf_all)[:, :, :total]
            o = _fox_attn_sample(r3(q), kt_cache, vt_cache, j, r3(k), r3(v), cum_t, r3(gate))
            fox_k.append(k.reshape(b, t, heads, hd))
            fox_v.append(v.reshape(b, t, heads, hd))
            fox_lf.append(lf.reshape(b, t, heads))
        else:
            wf = wts["mla"][j]
            w1, w2, wkn, wv = wf["rows"]
            qn, qr, ckv, kr = _mla_in(x2, wts["g_mix"][i], w1, wts["g_mla_q"][j], wts["g_mla_kv"][j],
                                      w2, cos, sin, mla_heads, wts["mla_scale"])
            qr_rows = jnp.swapaxes(qr.reshape(b, t, mla_heads, MLA_ROPE), 1, 2).reshape(b, mla_heads * t, MLA_ROPE)
            o = _mla_attn_sample(r3(qn), qr_rows, mla_past[0], krt_cache, j, r3(ckv), r3(kr), wkn, wv)
            mla_c.append(r3(ckv))
            mla_r.append(r3(kr))
        x, y = _after_mixer(x, o, wf["out"], i, mem_k, mem_v, wts)
    return (y, jnp.stack(fox_k), jnp.stack(fox_v), jnp.stack(fox_lf),
            jnp.stack(mla_c), jnp.stack(mla_r))


def kernel(x_prompt, x_sample, mem_prompt, cache_fox_k, cache_fox_v, cache_fox_logf, cache_mla_ckv, cache_mla_krope, cache_mem_k, cache_mem_v, g_mix, g_cross, g_mem, g_ffn, g_final, w_fox_in, b_fox_f, w_fox_out, w_mla_a, g_mla_q, g_mla_kv, w_mla_qb, w_mla_kvb, w_mla_out, w_x_q, w_x_kv, w_x_o, w_ffn_gu, w_ffn_down):
    depth, d = g_mix.shape
    fox_heads = b_fox_f.shape[1]
    x_heads = cache_mem_k.shape[3]
    q_lora = g_mla_q.shape[1]
    kv_lora = g_mla_kv.shape[1]
    mla_heads = w_mla_out.shape[1] // HEAD_DIM
    ff = w_ffn_down.shape[1]
    row = lambda g: [g[i].reshape(1, -1) for i in range(g.shape[0])]
    gu = w_ffn_gu.astype(BF16)
    wts = {
        "g_mix": row(g_mix), "g_cross": row(g_cross), "g_ffn": row(g_ffn), "g_final": g_final.reshape(1, d),
        "g_mla_q": row(g_mla_q), "g_mla_kv": row(g_mla_kv),
        "fox": [_prep_fox(w_fox_in[j], b_fox_f[j], w_fox_out[j], fox_heads) for j in range(w_fox_in.shape[0])],
        "mla": [_prep_mla(w_mla_a[j], w_mla_qb[j], w_mla_kvb[j], w_mla_out[j], q_lora, kv_lora, mla_heads)
                for j in range(w_mla_a.shape[0])],
        "cross": [(w_x_q[i].astype(BF16), w_x_o[i].astype(BF16)) for i in range(depth)],
        "ffn": [(gu[i, :, :ff], gu[i, :, ff:], w_ffn_down[i].astype(BF16)) for i in range(depth)],
        "fox_heads": fox_heads, "x_heads": x_heads, "mla_heads": mla_heads,
        "mla_scale": (HEAD_DIM + MLA_ROPE) ** -0.5,
    }
    bp, n_mem, _ = mem_prompt.shape
    mk, mv = _memory_kv(mem_prompt.reshape(bp * n_mem, d), g_mem.reshape(depth, 1, d), w_x_kv.astype(BF16))
    xw = mk.shape[-1]
    mk = mk.reshape(depth, bp, n_mem, xw)
    mv = mv.reshape(depth, bp, n_mem, xw)
    y_p, fk_p, fv_p, fl_p, mc_p, mr_p = _trunk_prompt(x_prompt, mk, mv, wts)
    past_len = cache_fox_k.shape[2]
    pos_s = past_len + jnp.arange(x_sample.shape[1], dtype=jnp.int32)
    bs = x_sample.shape[0]
    dh = xw // x_heads

    def in_memory_order(c):
        c = c.reshape(depth, bs, n_mem, x_heads, dh // LANE, LANE)
        return jnp.transpose(c, (0, 1, 2, 4, 3, 5)).reshape(depth, bs, n_mem * xw // LANE, LANE)

    y_s, fk_s, fv_s, fl_s, mc_s, mr_s = _trunk_sample(
        x_sample, pos_s, (cache_fox_k, cache_fox_v, cache_fox_logf), (cache_mla_ckv, cache_mla_krope),
        in_memory_order(cache_mem_k), in_memory_order(cache_mem_v), wts)
    return (y_p, y_s, fk_p, fv_p, fl_p, mc_p, mr_p,
            mk.reshape(depth, bp, n_mem, x_heads, dh), mv.reshape(depth, bp, n_mem, x_heads, dh),
            fk_s, fv_s, fl_s, mc_s, mr_s)
```

```python
import functools

import jax
import jax.numpy as jnp
from jax import lax
from jax.experimental import pallas as pl
from jax.experimental.pallas import tpu as pltpu

EPS = 1e-6
CHUNK = 64
assert CHUNK & (CHUNK - 1) == 0
ROPE_THETA = 10000.0
LANE = 128
HEAD_DIM = 64
PAIR = 2
MLA_ROPE = 32
NEG = -1e30
LOG2E = 1.4426950408889634
VMEM_LIMIT = 56 * 1024 * 1024
BF16 = jnp.bfloat16
F32 = jnp.float32


def _dot(a, b):
    return jnp.dot(a, b, preferred_element_type=F32)


def _dot_nt(a, b):
    return lax.dot_general(a, b, (((1,), (1,)), ((), ())), preferred_element_type=F32)


def _rms(x):
    return x * lax.rsqrt(jnp.mean(x * x, axis=-1, keepdims=True) + EPS)


def _params(*sem):
    return pltpu.CompilerParams(dimension_semantics=sem, vmem_limit_bytes=VMEM_LIMIT)


def _row_tile(m, cap=512):
    t = min(m, cap)
    assert m % t == 0
    return t


def _full(shape):
    return pl.BlockSpec(shape, lambda *_: (0,) * len(shape), pipeline_mode=pl.Buffered(1))


def _log_sigmoid(f):
    return jnp.minimum(f, 0.0) - jnp.log1p(jnp.exp(-jnp.abs(f)))


def _split3(x):
    hi = x.astype(BF16)
    r = x - hi.astype(F32)
    mid = r.astype(BF16)
    lo = (r - mid.astype(F32)).astype(BF16)
    return hi, mid, lo


def _layer_map(imap, layer, *idx):
    return (layer,) + tuple(imap(*idx))


def _stacked_call(body, *, grid, in_specs, args, outs, layer, prev, sem, name, scratch=()):
    out_specs, out_shape, stacked = [], [], []
    for k, (shape, dtype, blk, imap, n_layers) in enumerate(outs):
        if n_layers is None:
            out_specs.append(pl.BlockSpec(blk, imap))
            out_shape.append(jax.ShapeDtypeStruct(shape, dtype))
        else:
            out_specs.append(pl.BlockSpec((None,) + tuple(blk), functools.partial(_layer_map, imap, layer)))
            out_shape.append(jax.ShapeDtypeStruct((n_layers,) + tuple(shape), dtype))
            stacked.append(k)
    aliases = {}
    in_specs = list(in_specs)
    args = list(args)
    if prev is not None:
        for k, arr in zip(stacked, prev):
            aliases[len(args)] = k
            in_specs.append(pl.BlockSpec(memory_space=pl.ANY))
            args.append(arr)
    n_alias = len(aliases)

    def wrapped(*refs):
        n_in = len(args) - n_alias
        body(*refs[:n_in], *refs[n_in + n_alias:])

    res = pl.pallas_call(
        wrapped, grid=grid, in_specs=in_specs, out_specs=out_specs, out_shape=out_shape,
        input_output_aliases=aliases, scratch_shapes=list(scratch),
        compiler_params=_params(*sem), name=name,
    )(*args)
    return res, [res[k] for k in stacked]


def _fox_in_body(x_ref, g_ref, wq, wk, wv, wg, wf, bf_ref, q_out, k_out, v_out, gate_out, lf_out, *, scale, heads):
    h = (_rms(x_ref[...]) * g_ref[...]).astype(BF16)
    q_out[...] = (_dot(h, wq[...]) * scale).astype(BF16)
    k_out[...] = _dot(h, wk[...])
    v_out[...] = _dot(h, wv[...])
    gate_out[...] = jax.nn.sigmoid(_dot(h, wg[...]))
    lf_out[...] = _log_sigmoid(_dot(h, wf[...]) + bf_ref[...])[:, :heads]


def _fox_in(x, g, wq, wk, wv, wg, wf, bf, heads):
    m, d = x.shape
    w = wq.shape[1]
    tm = _row_tile(m)
    row = lambda n: pl.BlockSpec((tm, n), lambda i: (i, 0))
    return pl.pallas_call(
        functools.partial(_fox_in_body, scale=HEAD_DIM ** -0.5, heads=heads),
        grid=(m // tm,),
        in_specs=[row(d), _full((1, d)), _full((d, w)), _full((d, w)), _full((d, w)), _full((d, w)),
                  _full((d, LANE)), _full((1, LANE))],
        out_specs=[row(w), row(w), row(w), row(w), row(heads)],
        out_shape=[jax.ShapeDtypeStruct((m, w), BF16), jax.ShapeDtypeStruct((m, w), F32),
                   jax.ShapeDtypeStruct((m, w), F32), jax.ShapeDtypeStruct((m, w), F32),
                   jax.ShapeDtypeStruct((m, heads), F32)],
        compiler_params=_params("parallel"),
        name="fox_in",
    )(x, g, wq, wk, wv, wg, wf, bf)


def _fox_in_t_body(x_ref, g_ref, wqkvt, wg, wft, bf_ref,
                   qt_out, k_out, kt_out, vt_out, gate_out, lft_out, *, scale):
    h = (_rms(x_ref[...]) * g_ref[...]).astype(BF16)
    w = kt_out.shape[0]
    qkvt = _dot_nt(wqkvt[...], h)
    qt_out[...] = (qkvt[:w] * scale).astype(BF16)
    kt = qkvt[w:2 * w]
    kt_out[...] = kt
    k_out[...] = kt.T.astype(BF16)
    vt_out[...] = qkvt[2 * w:]
    gate_out[...] = jax.nn.sigmoid(_dot(h, wg[...]))
    lft_out[...] = _log_sigmoid(_dot_nt(wft[...], h) + bf_ref[...])


def _fox_in_t(x, g, wts, layer, n_layers, prev):
    wqkvt, wg, wft, bf = wts
    b, t, d = x.shape
    w = wg.shape[1]
    heads = wft.shape[0]
    tm = _row_tile(t)
    rows = lambda bi, i: (bi, i, 0)
    cols = lambda bi, i: (bi, 0, i)
    res, stacked = _stacked_call(
        functools.partial(_fox_in_t_body, scale=HEAD_DIM ** -0.5 * LOG2E),
        grid=(b, t // tm),
        in_specs=[pl.BlockSpec((None, tm, d), rows), _full((1, d)), _full((3 * w, d)),
                  _full((d, w)), _full((heads, d)), _full((heads, 1))],
        args=[x, g, wqkvt, wg, wft, bf],
        outs=[((b, w, t), BF16, (None, w, tm), cols, None),
              ((b, t, w), BF16, (None, tm, w), rows, None),
              ((b, w, t), F32, (None, w, tm), cols, n_layers),
              ((b, w, t), F32, (None, w, tm), cols, n_layers),
              ((b, t, w), F32, (None, tm, w), rows, None),
              ((b, heads, t), F32, (None, heads, tm), cols, n_layers)],
        layer=layer, prev=prev, sem=("parallel", "parallel"), name="fox_in_t")
    qt, k, _, _, gate, _ = res
    return qt, k, gate, stacked


CUM_BLOCK = 256


def _cumsum_lanes_body(lf_ref, out_ref):
    h, t = lf_ref.shape
    r = lax.broadcasted_iota(jnp.int32, (CUM_BLOCK, CUM_BLOCK), 0)
    c = lax.broadcasted_iota(jnp.int32, (CUM_BLOCK, CUM_BLOCK), 1)
    tri = jnp.where(r <= c, 1.0, 0.0).astype(BF16)
    carry = jnp.zeros((h, 1), F32)
    for b in range(t // CUM_BLOCK):
        hi, mid, lo = _split3(lf_ref[:, b * CUM_BLOCK:(b + 1) * CUM_BLOCK])
        out = _dot(hi, tri) + _dot(mid, tri) + _dot(lo, tri) + carry
        out_ref[:, b * CUM_BLOCK:(b + 1) * CUM_BLOCK] = out
        carry = out[:, CUM_BLOCK - 1:CUM_BLOCK]


def _cumsum_lanes(lf, layer=None):
    b, h, t = lf.shape[-3:]
    assert t % CUM_BLOCK == 0
    if layer is None:
        spec = pl.BlockSpec((None, h, t), lambda i: (i, 0, 0))
    else:
        spec = pl.BlockSpec((None, None, h, t), lambda i: (layer, i, 0, 0))
    return pl.pallas_call(
        _cumsum_lanes_body, grid=(b,),
        in_specs=[spec],
        out_specs=pl.BlockSpec((None, h, t), lambda i: (i, 0, 0)),
        out_shape=jax.ShapeDtypeStruct((b, h, t), F32),
        compiler_params=_params("parallel"), name="logf_cumsum_t",
    )(lf)


N_PIECE = 3
TRIP_BLOCKS = 4
V_ROWS = HEAD_DIM + 16
BIAS_ROWS = 16


def _attn_prompt_body(*refs, fox, tq):
    if fox:
        qt_ref, k_ref, vt_ref, crow_ref, ccol_ref, gate_ref, o_ref, kx, vx = refs
    else:
        qnt_ref, qrt_ref, kn_ref, krd_ref, vt_ref, o_ref, kx, vx = refs
    hp = pl.program_id(1)
    tk = tq
    t = kx.shape[0]
    ones_lo = PAIR * N_PIECE

    tail_row = lax.broadcasted_iota(jnp.int32, (V_ROWS - HEAD_DIM, t), 0)
    for e in range(PAIR):
        vx[e * V_ROWS:e * V_ROWS + HEAD_DIM, :] = vt_ref[e * HEAD_DIM:(e + 1) * HEAD_DIM, :].astype(BF16)
        vx[e * V_ROWS + HEAD_DIM:(e + 1) * V_ROWS, :] = jnp.where(tail_row == 0, 1.0, 0.0).astype(BF16)
    if fox:
        kx[:, :LANE] = k_ref[...]
        heads = ccol_ref.shape[1]
        hrow = lax.broadcasted_iota(jnp.int32, (heads, LANE), 0)
        lcol = lax.broadcasted_iota(jnp.int32, (heads, LANE), 1)
        ext = jnp.zeros((t, LANE), F32)
        for i, piece in enumerate(_split3(ccol_ref[...] * LOG2E)):
            hit = ((hrow == PAIR * hp) & (lcol == i)) | ((hrow == PAIR * hp + 1) & (lcol == N_PIECE + i))
            ext = ext + _dot(piece, jnp.where(hit, 1.0, 0.0).astype(BF16))
        lane = lax.broadcasted_iota(jnp.int32, (t, LANE), 1)
        ext = jnp.where((lane >= ones_lo) & (lane < ones_lo + N_PIECE), 1.0, ext)
        kx[:, LANE:] = ext.astype(BF16)
    else:
        kx[:, :LANE] = kn_ref[...]
        kx[:, LANE:] = krd_ref[...]

    lax.fori_loop(0, t // tq, functools.partial(_attn_query_block, refs, kx, vx, fox=fox, tq=tq), 0)


def _attn_query_block(refs, kx, vx, qi, _, *, fox, tq):
    if fox:
        qt_ref, _, _, crow_ref, _, gate_ref, o_ref = refs[:7]
    else:
        qnt_ref, qrt_ref, _, _, _, o_ref = refs[:6]
    tk = tq
    ones_lo = PAIR * N_PIECE
    cols = pl.ds(pl.multiple_of(qi * tq, tq), tq)

    sub = lax.broadcasted_iota(jnp.int32, (LANE, tq), 0)
    zero = jnp.zeros((), BF16)
    qs = []
    for e in range(PAIR):
        in_head = (sub >= e * HEAD_DIM) & (sub < (e + 1) * HEAD_DIM)
        if fox:
            sub16 = lax.broadcasted_iota(jnp.int32, (BIAS_ROWS, tq), 0)
            ext = jnp.where((sub16 >= N_PIECE * e) & (sub16 < N_PIECE * (e + 1)), -1.0, 0.0)
            for i, piece in enumerate(_split3(crow_ref[e:e + 1, cols] * LOG2E)):
                ext = jnp.where(sub16 == ones_lo + i, piece.astype(F32), ext)
            qs.append(jnp.concatenate([jnp.where(in_head, qt_ref[:, cols], zero), ext.astype(BF16),
                                       jnp.zeros((LANE - BIAS_ROWS, tq), BF16)], axis=0))
        else:
            in_rope = (sub >= e * MLA_ROPE) & (sub < (e + 1) * MLA_ROPE)
            qs.append(jnp.concatenate([jnp.where(in_head, qnt_ref[:, cols], zero),
                                       jnp.where(in_rope, qrt_ref[:, cols], zero)], axis=0))

    def blocks(j0, carry, nblk, diagonal_last):
        starts = [(j0 + u) * tk for u in range(nblk)]
        starts = [st if isinstance(st, int) else pl.multiple_of(st, tk) for st in starts]
        scores = [[_dot(kx[pl.ds(st, tk), :], qs[e]) for e in range(PAIR)] for st in starts]
        carry = list(carry)
        for u, st in enumerate(starts):
            for e in range(PAIR):
                m, acc = carry[2 * e:2 * e + 2]
                s = scores[u][e]
                if diagonal_last and u == nblk - 1:
                    key = lax.broadcasted_iota(jnp.int32, (tk, tq), 0)
                    qry = lax.broadcasted_iota(jnp.int32, (tk, tq), 1)
                    visible = (key <= qry) if fox else (key <= (qry | (CHUNK - 1)))
                    s = jnp.where(visible, s, NEG)
                m_new = jnp.maximum(m, jnp.max(s, axis=0, keepdims=True))
                p = jnp.exp2(s - m_new).astype(BF16)
                v_j = vx[e * V_ROWS:(e + 1) * V_ROWS, pl.ds(st, tk)]
                carry[2 * e:2 * e + 2] = [m_new, jnp.exp2(m - m_new) * acc + _dot(v_j, p)]
        return tuple(carry)

    init = (jnp.full((1, tq), NEG, F32), jnp.zeros((V_ROWS, tq), F32)) * PAIR
    n_q = kx.shape[0] // tq
    carry, left = init, qi
    if TRIP_BLOCKS < n_q <= 2 * TRIP_BLOCKS:
        full = jnp.where(qi >= TRIP_BLOCKS, 1, 0)
        left = qi - full * TRIP_BLOCKS
        carry = lax.fori_loop(0, full, lambda _, c: blocks(0, c, TRIP_BLOCKS, False), init)
    elif n_q > TRIP_BLOCKS:
        main = qi // TRIP_BLOCKS
        left = qi - main * TRIP_BLOCKS
        carry = lax.fori_loop(0, main, lambda i, c: blocks(TRIP_BLOCKS * i, c, TRIP_BLOCKS, False), init)
    for n in range(min(TRIP_BLOCKS, n_q)):
        carry = lax.fori_loop(0, jnp.where(left == n, 1, 0),
                              lambda _, c, n=n: blocks(qi - n, c, n + 1, True), carry)
    o = jnp.concatenate([acc[:HEAD_DIM] * (1.0 / acc[HEAD_DIM:HEAD_DIM + 1]) for acc in carry[1::2]], axis=0).T
    if fox:
        o = o * gate_ref[cols, :]
    o_ref[cols, :] = o.astype(BF16)
    return 0


def _fox_attn_prompt(qt, k, vt_all, layer, cum_t, gate, tq=512):
    b, w, t = qt.shape
    heads = cum_t.shape[1]
    tq = min(tq, t)
    npair = w // LANE
    rowblk = pl.BlockSpec((None, t, LANE), lambda bi, hp: (bi, 0, hp))
    return pl.pallas_call(
        functools.partial(_attn_prompt_body, fox=True, tq=tq),
        grid=(b, npair),
        in_specs=[pl.BlockSpec((None, LANE, t), lambda bi, hp: (bi, hp, 0)),
                  rowblk,
                  pl.BlockSpec((None, None, LANE, t), lambda bi, hp: (layer, bi, hp, 0)),
                  pl.BlockSpec((None, None, PAIR, t), lambda bi, hp: (bi, hp, 0, 0)),
                  pl.BlockSpec((None, t, heads), lambda bi, hp: (bi, 0, 0)),
                  rowblk],
        out_specs=rowblk,
        out_shape=jax.ShapeDtypeStruct((b, t, w), BF16),
        scratch_shapes=[pltpu.VMEM((t, 2 * LANE), BF16), pltpu.VMEM((PAIR * V_ROWS, t), BF16)],
        compiler_params=_params("parallel", "parallel"),
        name="fox_attn_prompt",
    )(qt, k, vt_all, cum_t.reshape(b, npair, PAIR, t), jnp.swapaxes(cum_t, 1, 2), gate)


def _mla_attn_prompt(qnt, qrt, kn, krd, vt, tq=512):
    b, w, t = qnt.shape
    tq = min(tq, t)
    npair = w // LANE
    colblk = pl.BlockSpec((None, LANE, t), lambda bi, hp: (bi, hp, 0))
    rowblk = pl.BlockSpec((None, t, LANE), lambda bi, hp: (bi, 0, hp))
    return pl.pallas_call(
        functools.partial(_attn_prompt_body, fox=False, tq=tq),
        grid=(b, npair),
        in_specs=[colblk, colblk, rowblk, pl.BlockSpec((None, t, LANE), lambda bi, hp: (bi, 0, 0)), colblk],
        out_specs=rowblk,
        out_shape=jax.ShapeDtypeStruct((b, t, w), BF16),
        scratch_shapes=[pltpu.VMEM((t, 2 * LANE), BF16), pltpu.VMEM((PAIR * V_ROWS, t), BF16)],
        compiler_params=_params("parallel", "parallel"),
        name="mla_attn_prompt",
    )(qnt, qrt, kn, krd, vt)


SAMPLE_PAIRS = 2


def _fox_attn_sample_body(q_ref, kt_ref, vt_ref, kn_ref, vn_ref, cq_ref, ck_ref, gate_ref, o_ref):
    t = q_ref.shape[0]
    p = kt_ref.shape[1]
    rows = PAIR * t
    lane = lax.broadcasted_iota(jnp.int32, (rows, LANE), 1)
    row = lax.broadcasted_iota(jnp.int32, (rows, LANE), 0)
    hl = lax.broadcasted_iota(jnp.int32, cq_ref.shape, 1)
    first_p = lax.broadcasted_iota(jnp.int32, (rows, p), 0) < t
    rn = lax.broadcasted_iota(jnp.int32, (rows, t), 0)
    cn = lax.broadcasted_iota(jnp.int32, (rows, t), 1)
    for pp in range(SAMPLE_PAIRS):
        hp = pl.program_id(1) * SAMPLE_PAIRS + pp
        lanes = slice(pp * LANE, (pp + 1) * LANE)
        q2 = jnp.concatenate([q_ref[:, lanes]] * PAIR, axis=0)
        q2 = jnp.where((lane // HEAD_DIM) == (row // t), q2, jnp.zeros((), BF16))
        cq = jnp.concatenate([jnp.sum(jnp.where(hl == PAIR * hp + e, cq_ref[...], 0.0), axis=1, keepdims=True)
                              for e in range(PAIR)], axis=0)
        ck = ck_ref[pp]
        s_p = _dot(q2, kt_ref[lanes, :].astype(BF16)) + cq - jnp.where(first_p, ck[0:1, :p], ck[1:2, :p])
        s_n = _dot_nt(q2, kn_ref[:, lanes].astype(BF16)) + cq - jnp.where(rn < t, ck[0:1, p:], ck[1:2, p:])
        s_n = jnp.where(cn <= lax.rem(rn, t), s_n, NEG)
        m = jnp.maximum(jnp.max(s_p, axis=-1, keepdims=True), jnp.max(s_n, axis=-1, keepdims=True))
        p_p = jnp.exp(s_p - m)
        p_n = jnp.exp(s_n - m)
        l = jnp.sum(p_p, axis=-1, keepdims=True) + jnp.sum(p_n, axis=-1, keepdims=True)
        o = (_dot_nt(p_p.astype(BF16), vt_ref[lanes, :].astype(BF16))
             + _dot(p_n.astype(BF16), vn_ref[:, lanes].astype(BF16))) / l
        o = jnp.where(lax.broadcasted_iota(jnp.int32, (t, LANE), 1) < HEAD_DIM, o[:t], o[t:]) * gate_ref[:, lanes]
        o_ref[:, lanes] = o.astype(BF16)


def _fox_attn_sample(q, kt_cache, vt_cache, layer, k_new, v_new, cum_t, gate):
    b, t, w = q.shape
    p = kt_cache.shape[-1]
    heads = cum_t.shape[1]
    npair = w // LANE
    wide = SAMPLE_PAIRS * LANE
    new = pl.BlockSpec((None, t, wide), lambda bi, hp: (bi, 0, hp))
    past = pl.BlockSpec((None, None, wide, p), lambda bi, hp: (layer, bi, hp, 0))
    return pl.pallas_call(
        _fox_attn_sample_body,
        grid=(b, npair // SAMPLE_PAIRS),
        in_specs=[new, past, past, new, new,
                  pl.BlockSpec((None, t, heads), lambda bi, hp: (bi, 0, 0)),
                  pl.BlockSpec((None, SAMPLE_PAIRS, PAIR, p + t), lambda bi, hp: (bi, hp, 0, 0)),
                  new],
        out_specs=new,
        out_shape=jax.ShapeDtypeStruct((b, t, w), BF16),
        compiler_params=_params("parallel", "parallel"),
        name="fox_attn_sample",
    )(q, kt_cache, vt_cache, k_new, v_new, jnp.swapaxes(cum_t[:, :, p:], 1, 2),
      cum_t.reshape(b, npair, PAIR, p + t), gate)


def _mla_attn_sample_body(qn_ref, qr_ref, cp_ref, krp_ref, cn_ref, krn_ref, wkn_ref, wv_ref, o_ref, *, past_len):
    t, w = qn_ref.shape
    p = cp_ref.shape[0]
    heads = w // HEAD_DIM
    rows = heads * t
    lane = lax.broadcasted_iota(jnp.int32, (rows, w), 1)
    row = lax.broadcasted_iota(jnp.int32, (rows, w), 0)
    own = (lane // HEAD_DIM) == (row // t)
    q_wide = jnp.where(own, jnp.concatenate([qn_ref[...]] * heads, axis=0), jnp.zeros((), BF16))
    q_lat = _dot_nt(q_wide, wkn_ref[...]).astype(BF16)
    qr = qr_ref[...]
    c_p = cp_ref[...].astype(BF16)
    c_n = cn_ref[...].astype(BF16)
    s_p = _dot_nt(q_lat, c_p) + _dot(qr, krp_ref[...].astype(BF16))
    s_n = _dot_nt(q_lat, c_n) + _dot_nt(qr, krn_ref[...].astype(BF16))
    q_chunk_p = (past_len + lax.rem(lax.broadcasted_iota(jnp.int32, (rows, p), 0), t)) // CHUNK
    s_p = jnp.where((lax.broadcasted_iota(jnp.int32, (rows, p), 1) // CHUNK) <= q_chunk_p, s_p, NEG)
    q_chunk_n = (past_len + lax.rem(lax.broadcasted_iota(jnp.int32, (rows, t), 0), t)) // CHUNK
    s_n = jnp.where(((past_len + lax.broadcasted_iota(jnp.int32, (rows, t), 1)) // CHUNK) <= q_chunk_n, s_n, NEG)
    m = jnp.maximum(jnp.max(s_p, axis=-1, keepdims=True), jnp.max(s_n, axis=-1, keepdims=True))
    p_p = jnp.exp(s_p - m)
    p_n = jnp.exp(s_n - m)
    l = jnp.sum(p_p, axis=-1, keepdims=True) + jnp.sum(p_n, axis=-1, keepdims=True)
    o_lat = (_dot(p_p.astype(BF16), c_p) + _dot(p_n.astype(BF16), c_n)) / l
    o_wide = jnp.where(own, _dot(o_lat.astype(BF16), wv_ref[...]), 0.0)
    o = o_wide[:t]
    for h in range(1, heads):
        o = o + o_wide[h * t:(h + 1) * t]
    o_ref[...] = o.astype(BF16)


def _mla_attn_sample(qn, qr_rows, ckv_cache, krt_cache, layer, ckv_new, kr_new, wkn, wv):
    b, t, w = qn.shape
    p, c = ckv_cache.shape[2:]
    rows = qr_rows.shape[1]
    blk = lambda *s: pl.BlockSpec((None,) + s, lambda bi: (bi,) + (0,) * len(s))
    past = lambda *s: pl.BlockSpec((None, None) + s, lambda bi: (layer, bi) + (0,) * len(s))
    return pl.pallas_call(
        functools.partial(_mla_attn_sample_body, past_len=p),
        grid=(b,),
        in_specs=[blk(t, w), blk(rows, MLA_ROPE), past(p, c), past(MLA_ROPE, p), blk(t, c), blk(t, MLA_ROPE),
                  _full(wkn.shape), _full(wv.shape)],
        out_specs=blk(t, w),
        out_shape=jax.ShapeDtypeStruct((b, t, w), BF16),
        compiler_params=_params("parallel"),
        name="mla_attn_sample",
    )(qn, qr_rows, ckv_cache, krt_cache, ckv_new, kr_new, wkn, wv)


def _mla_in_body(x_ref, g_ref, w1, gq_ref, gkv_ref, w2, cos_ref, sin_ref, cosq_ref, sinq_ref,
                 qn_out, qr_out, ckv_out, kr_out, *, q_lora, kv_lora, scale):
    h = (_rms(x_ref[...]) * g_ref[...]).astype(BF16)
    a = _dot(h, w1[...])
    c_q = (_rms(a[:, :q_lora]) * gq_ref[...]).astype(BF16)
    ckv_out[...] = _rms(a[:, q_lora:q_lora + kv_lora]) * gkv_ref[...]
    o = q_lora + kv_lora
    krd = a[:, o:o + LANE] * cos_ref[...] + a[:, o + LANE:o + 2 * LANE] * sin_ref[...]
    kr_out[...] = krd[:, :MLA_ROPE]
    w = qn_out.shape[1]
    r = qr_out.shape[1]
    qn_out[...] = (_dot(c_q, w2[:, :w]) * scale).astype(BF16)
    qr = _dot(c_q, w2[:, w:w + r]) * cosq_ref[...] + _dot(c_q, w2[:, w + r:w + 2 * r]) * sinq_ref[...]
    qr_out[...] = (qr * scale).astype(BF16)


def _mla_in(x, g, w1, gq, gkv, w2, cos, sin, heads, scale):
    m, d = x.shape
    q_lora = gq.shape[1]
    kv_lora = gkv.shape[1]
    w = heads * HEAD_DIM
    r = heads * MLA_ROPE
    tm = _row_tile(m)
    t = cos.shape[0]
    assert tm % t == 0
    per_head = lambda tab: jnp.tile(tab[:, :MLA_ROPE], (tm // t, heads))
    cosq, sinq = per_head(cos), per_head(sin)
    cos, sin = jnp.tile(cos, (tm // t, 1)), jnp.tile(sin, (tm // t, 1))
    row = lambda n: pl.BlockSpec((tm, n), lambda i: (i, 0))
    return pl.pallas_call(
        functools.partial(_mla_in_body, q_lora=q_lora, kv_lora=kv_lora, scale=scale),
        grid=(m // tm,),
        in_specs=[row(d), _full((1, d)), _full(w1.shape), _full((1, q_lora)), _full((1, kv_lora)),
                  _full(w2.shape), _full((tm, LANE)), _full((tm, LANE)), _full((tm, r)), _full((tm, r))],
        out_specs=[row(w), row(r), row(kv_lora), row(MLA_ROPE)],
        out_shape=[jax.ShapeDtypeStruct((m, w), BF16), jax.ShapeDtypeStruct((m, r), BF16),
                   jax.ShapeDtypeStruct((m, kv_lora), F32), jax.ShapeDtypeStruct((m, MLA_ROPE), F32)],
        compiler_params=_params("parallel"),
        name="mla_in",
    )(x, g, w1, gq, gkv, w2, cos, sin, cosq, sinq)


def _mla_in_t_body(x_ref, g_ref, w1, gq_ref, gkv_ref, wq3t, wkn, wvt, wkrt,
                   cos_ref, sin_ref, cost_ref, sint_ref,
                   qnt_out, qrt_out, kn_out, krd_out, vt_out, ckv_out, krt_out, *, q_lora, kv_lora, scale):
    h = (_rms(x_ref[...]) * g_ref[...]).astype(BF16)
    a = _dot(h, w1[...])
    c_q = (_rms(a[:, :q_lora]) * gq_ref[...]).astype(BF16)
    c_kv = _rms(a[:, q_lora:q_lora + kv_lora]) * gkv_ref[...]
    ckv_out[...] = c_kv
    o = q_lora + kv_lora
    krd_out[...] = (a[:, o:o + LANE] * cos_ref[...] + a[:, o + LANE:o + 2 * LANE] * sin_ref[...]).astype(BF16)
    cost = cost_ref[...]
    sint = sint_ref[...]
    kab = _dot_nt(wkrt[...], h)
    krt_out[...] = kab[:MLA_ROPE] * cost[:MLA_ROPE] + kab[MLA_ROPE:] * sint[:MLA_ROPE]
    c_kv = c_kv.astype(BF16)
    kn_out[...] = _dot(c_kv, wkn[...]).astype(BF16)
    vt_out[...] = _dot_nt(wvt[...], c_kv).astype(BF16)
    w = qnt_out.shape[0]
    q3 = _dot_nt(wq3t[...], c_q)
    qnt_out[...] = (q3[:w] * scale).astype(BF16)
    for p in range(w // LANE):
        qa = q3[w + p * LANE:w + (p + 1) * LANE]
        qb = q3[2 * w + p * LANE:2 * w + (p + 1) * LANE]
        qrt_out[p * LANE:(p + 1) * LANE, :] = ((qa * cost + qb * sint) * scale).astype(BF16)


def _mla_in_t(x, g, gq, gkv, wts, tables, scale, layer, n_layers, prev):
    w1, wq3t, wkn, wvt, wkrt = wts
    cos, sin, cost, sint = tables
    b, t, d = x.shape
    q_lora = gq.shape[1]
    kv_lora = gkv.shape[1]
    w = wkn.shape[1]
    tm = _row_tile(t)
    rows = lambda bi, i: (bi, i, 0)
    cols = lambda bi, i: (bi, 0, i)
    res, stacked = _stacked_call(
        functools.partial(_mla_in_t_body, q_lora=q_lora, kv_lora=kv_lora, scale=scale),
        grid=(b, t // tm),
        in_specs=[pl.BlockSpec((None, tm, d), rows), _full((1, d)), _full(w1.shape), _full((1, q_lora)),
                  _full((1, kv_lora)), _full(wq3t.shape), _full(wkn.shape), _full(wvt.shape), _full(wkrt.shape),
                  pl.BlockSpec((tm, LANE), lambda bi, i: (i, 0)), pl.BlockSpec((tm, LANE), lambda bi, i: (i, 0)),
                  pl.BlockSpec((LANE, tm), lambda bi, i: (0, i)), pl.BlockSpec((LANE, tm), lambda bi, i: (0, i))],
        args=[x, g, w1, gq, gkv, wq3t, wkn, wvt, wkrt, cos, sin, cost, sint],
        outs=[((b, w, t), BF16, (None, w, tm), cols, None),
              ((b, w, t), BF16, (None, w, tm), cols, None),
              ((b, t, w), BF16, (None, tm, w), rows, None),
              ((b, t, LANE), BF16, (None, tm, LANE), rows, None),
              ((b, w, t), BF16, (None, w, tm), cols, None),
              ((b, t, kv_lora), F32, (None, tm, kv_lora), rows, n_layers),
              ((b, MLA_ROPE, t), F32, (None, MLA_ROPE, tm), cols, n_layers)],
        layer=layer, prev=prev, sem=("parallel", "parallel"), name="mla_in_t")
    return res[:5], stacked


def _memory_kv_body(m_ref, g_ref, w_ref, k_out, v_out):
    h = (_rms(m_ref[...]) * g_ref[...]).astype(BF16)
    n = k_out.shape[-1]
    k_out[...] = _dot(h, w_ref[:, :n])
    v_out[...] = _dot(h, w_ref[:, n:])


def _memory_kv(mem, g_mem, w_kv):
    m, d = mem.shape
    depth, _, n2 = w_kv.shape
    n = n2 // 2
    tm = _row_tile(m)
    out = pl.BlockSpec((None, tm, n), lambda l, i: (l, i, 0))
    return pl.pallas_call(
        _memory_kv_body, grid=(depth, m // tm),
        in_specs=[pl.BlockSpec((tm, d), lambda l, i: (i, 0)),
                  pl.BlockSpec((None, 1, d), lambda l, i: (l, 0, 0)),
                  pl.BlockSpec((None, d, n2), lambda l, i: (l, 0, 0))],
        out_specs=[out, out],
        out_shape=[jax.ShapeDtypeStruct((depth, m, n), F32)] * 2,
        compiler_params=_params("parallel", "parallel"), name="memory_kv",
    )(mem, g_mem, w_kv)


FF_BLOCK = 256
SHORT_SEQ_ROWS = 128


def _layer_tail_body(o_ref, x_ref, wout, gc_ref, wq, mk_ref, mv_ref, wxo, gf_ref, wg, wu, wd, *rest, heads, final):
    if final:
        gl_ref, x_out, y_out, att = rest
    else:
        x_out, att = rest
    bb, tm, d = x_ref.shape
    x = x_ref[...].reshape(bb * tm, d) + _dot(o_ref[...].reshape(bb * tm, o_ref.shape[-1]), wout[...])
    h = (_rms(x) * gc_ref[...]).astype(BF16)
    dh = wq.shape[1] // heads
    q = (_dot(h, wq[...]) * dh ** -0.5).astype(BF16)
    for b in range(bb):
        for hd in range(heads):
            cols = slice(hd * dh, (hd + 1) * dh)
            q_h = q[b * tm:(b + 1) * tm, cols]
            if mk_ref.shape[-1] == wq.shape[1]:
                k_parts, v_parts = [mk_ref[b, :, cols]], [mv_ref[b, :, cols]]
            else:
                parts = dh // LANE
                pick = lambda ref, c: ref[b, pl.ds(c * heads + hd, ref.shape[1] // (heads * parts),
                                                   stride=heads * parts), :]
                k_parts = [pick(mk_ref, c) for c in range(parts)]
                v_parts = [pick(mv_ref, c) for c in range(parts)]
            width = dh // len(k_parts)
            s = sum(_dot_nt(q_h[:, c * width:(c + 1) * width], kp.astype(BF16)) for c, kp in enumerate(k_parts))
            p = jnp.exp(s - jnp.max(s, axis=-1, keepdims=True))
            pb = p.astype(BF16)
            o = jnp.concatenate([_dot(pb, vp.astype(BF16)) for vp in v_parts], axis=1)
            o = o / jnp.sum(p, axis=-1, keepdims=True)
            att[b * tm:(b + 1) * tm, cols] = o.astype(BF16)
    x = x + _dot(att[...], wxo[...])
    h = (_rms(x) * gf_ref[...]).astype(BF16)
    for c in range(wg.shape[1] // FF_BLOCK):
        cols = slice(c * FF_BLOCK, (c + 1) * FF_BLOCK)
        gate = _dot(h, wg[:, cols])
        up = _dot(h, wu[:, cols])
        x = x + _dot((gate * jax.nn.sigmoid(gate) * up).astype(BF16), wd[cols, :])
    x_out[...] = x.reshape(bb, tm, d)
    if final:
        y_out[...] = (_rms(x) * gl_ref[...]).reshape(bb, tm, d)


def _layer_tail(o, x, w_out, g_cross, wq, mk, mv, layer, wxo, g_ffn, wg, wu, wd, heads, g_final=None):
    b, t, d = x.shape
    w = o.shape[-1]
    xw = wq.shape[1]
    ff = wg.shape[1]
    assert ff % FF_BLOCK == 0
    tm = min(t, 512)
    bb = max(1, min(b, SHORT_SEQ_ROWS // tm))
    blk = lambda c: pl.BlockSpec((bb, tm, c), lambda bi, ti: (bi, ti, 0))
    mblk = pl.BlockSpec((None, bb) + mk.shape[2:], lambda bi, ti: (layer, bi, 0, 0),
                        pipeline_mode=pl.Buffered(1) if bb > 1 else None)
    final = g_final is not None
    ins = [blk(w), blk(d), _full((w, d)), _full((1, d)), _full((d, xw)), mblk, mblk, _full((xw, d)),
           _full((1, d)), _full((d, ff)), _full((d, ff)), _full((ff, d))]
    args = [o, x, w_out, g_cross, wq, mk, mv, wxo, g_ffn, wg, wu, wd]
    if final:
        ins.append(_full((1, d)))
        args.append(g_final)
    out = jax.ShapeDtypeStruct((b, t, d), F32)
    return pl.pallas_call(
        functools.partial(_layer_tail_body, heads=heads, final=final),
        grid=(b // bb, t // tm),
        in_specs=ins,
        out_specs=[blk(d), blk(d)] if final else blk(d),
        out_shape=[out, out] if final else out,
        scratch_shapes=[pltpu.VMEM((bb * tm, xw), BF16)],
        compiler_params=_params("parallel", "parallel"), name="layer_tail",
    )(*args)


def _prep_fox(w_in, b_f, w_out, heads):
    width = w_out.shape[0]
    w_in = w_in.astype(BF16)
    wq, wk, wv, wf, wg = jnp.split(w_in, [width, 2 * width, 3 * width, 3 * width + heads], axis=1)
    rows = (wq, wk, wv, wg, jnp.pad(wf, ((0, 0), (0, LANE - heads))),
            jnp.pad(b_f, (0, LANE - heads)).reshape(1, LANE))
    cols = (w_in[:, :3 * width].T, wg, wf.T, b_f.reshape(heads, 1))
    return {"rows": rows, "cols": cols, "out": w_out.astype(BF16)}


def _prep_mla(w_a, w_qb, w_kvb, w_out, q_lora, kv_lora, heads):
    half = MLA_ROPE // 2
    w_a = w_a.astype(BF16)
    d = w_a.shape[0]
    x1 = w_a[:, q_lora + kv_lora:q_lora + kv_lora + half]
    x2 = w_a[:, q_lora + kv_lora + half:]
    zeros = jnp.zeros((d, LANE - 2 * MLA_ROPE), BF16)
    w1 = jnp.concatenate([w_a[:, :q_lora + kv_lora], x1, x2, x1, x2, zeros, x2, x1, x2, x1, zeros], axis=1)
    wkrt = jnp.concatenate([x1, x2, x2, x1], axis=1).T
    qb = w_qb.astype(BF16).reshape(q_lora, heads, HEAD_DIM + MLA_ROPE)
    wqn = qb[:, :, :HEAD_DIM].reshape(q_lora, heads * HEAD_DIM)
    rope = qb[:, :, HEAD_DIM:]
    swapped = jnp.concatenate([rope[..., half:], rope[..., :half]], axis=-1)

    def pack(r):
        r = r.reshape(q_lora, heads // PAIR, PAIR * MLA_ROPE)
        return jnp.pad(r, ((0, 0), (0, 0), (0, LANE - PAIR * MLA_ROPE))).reshape(q_lora, -1)

    wqa, wqb = pack(rope), pack(swapped)
    kvb = w_kvb.astype(BF16).reshape(kv_lora, heads, 2 * HEAD_DIM)
    wkn = kvb[:, :, :HEAD_DIM].reshape(kv_lora, heads * HEAD_DIM)
    wv = kvb[:, :, HEAD_DIM:].reshape(kv_lora, heads * HEAD_DIM)
    flat = lambda r: r.reshape(q_lora, heads * MLA_ROPE)
    rows = (w1, jnp.concatenate([wqn, flat(rope), flat(swapped)], axis=1), wkn, wv)
    cols = (w1, jnp.concatenate([wqn, wqa, wqb], axis=1).T, wkn, wv.T, wkrt)
    return {"rows": rows, "cols": cols, "out": w_out.astype(BF16)}


def _rope_tables(pos):
    half = MLA_ROPE // 2
    inv = ROPE_THETA ** (-jnp.arange(half, dtype=F32) / half)
    ang = pos.astype(F32)[:, None] * inv[None, :]
    cos, sin = jnp.cos(ang), jnp.sin(ang)
    z = jnp.zeros((pos.shape[0], LANE - 2 * MLA_ROPE), F32)
    return (jnp.concatenate([cos, cos, cos, cos, z], axis=1),
            jnp.concatenate([-sin, sin, -sin, sin, z], axis=1))


def _after_mixer(x, o, w_out, i, mem_k, mem_v, wts):
    wxq, wxo = wts["cross"][i]
    wgt, wup, wdn = wts["ffn"][i]
    last = i == len(wts["ffn"]) - 1
    res = _layer_tail(o, x, w_out, wts["g_cross"][i], wxq, mem_k, mem_v, i, wxo, wts["g_ffn"][i], wgt, wup, wdn,
                      wts["x_heads"], wts["g_final"] if last else None)
    return res if last else (res, None)


def _trunk_prompt(x, mem_k, mem_v, wts):
    b, t, d = x.shape
    m = b * t
    depth = len(wts["ffn"])
    n_fox, n_mla = len(wts["fox"]), len(wts["mla"])
    cos, sin = _rope_tables(jnp.arange(t, dtype=jnp.int32))
    tables = (cos, sin, cos.T, sin.T)
    fox_state = mla_state = None
    for i in range(depth):
        j = i // 2
        if i % 2 == 0:
            wf = wts["fox"][j]
            qt, k, gate, fox_state = _fox_in_t(x, wts["g_mix"][i], wf["cols"], j, n_fox, fox_state)
            cum_t = _cumsum_lanes(fox_state[2], j)
            o = _fox_attn_prompt(qt, k, fox_state[1], j, cum_t, gate)
        else:
            wf = wts["mla"][j]
            (qnt, qrt, kn, krd, vt), mla_state = _mla_in_t(
                x, wts["g_mix"][i], wts["g_mla_q"][j], wts["g_mla_kv"][j], wf["cols"], tables,
                wts["mla_scale"] * LOG2E, j, n_mla, mla_state)
            o = _mla_attn_prompt(qnt, qrt, kn, krd, vt)
        x, y = _after_mixer(x, o, wf["out"], i, mem_k, mem_v, wts)
    kt, vt, lft = fox_state
    ckv, krt = mla_state
    heads = lft.shape[2]
    unfold = lambda a: jnp.transpose(a.reshape(n_fox, b, heads, a.shape[2] // heads, t), (0, 1, 4, 2, 3))
    return (y.reshape(b, t, d), unfold(kt), unfold(vt), jnp.swapaxes(lft, 2, 3), ckv, jnp.swapaxes(krt, 2, 3))


def _trunk_sample(x, pos, fox_past, mla_past, mem_k, mem_v, wts):
    b, t, d = x.shape
    m = b * t
    depth = len(wts["ffn"])
    cos, sin = _rope_tables(pos)
    heads = wts["fox_heads"]
    mla_heads = wts["mla_heads"]
    fox_k, fox_v, fox_lf, mla_c, mla_r = [], [], [], [], []
    r3 = lambda a: a.reshape(b, t, a.shape[-1])
    n_fox, _, p, _, hd = fox_past[0].shape
    kt_cache = jnp.transpose(fox_past[0], (0, 1, 3, 4, 2)).reshape(n_fox, b, heads * hd, p)
    vt_cache = jnp.transpose(fox_past[1], (0, 1, 3, 4, 2)).reshape(n_fox, b, heads * hd, p)
    lft_cache = jnp.swapaxes(fox_past[2], 2, 3)
    krt_cache = jnp.swapaxes(mla_past[1], 2, 3)
    total = p + t
    pad = jnp.zeros((b, heads, -(-total // CUM_BLOCK) * CUM_BLOCK - total), F32)
    for i in range(depth):
        j = i // 2
        x2 = x.reshape(m, d)
        if i % 2 == 0:
            wf = wts["fox"][j]
            q, k, v, gate, lf = _fox_in(x2, wts["g_mix"][i], *wf["rows"], heads)
            lf_all = jnp.concatenate([lft_cache[j], jnp.swapaxes(r3(lf), 1, 2), pad], axis=2)
            cum_t = _cumsum_lanes(lf_all)[:, :, :total]
            o = _fox_attn_sample(r3(q), kt_cache, vt_cache, j, r3(k), r3(v), cum_t, r3(gate))
            fox_k.append(k.reshape(b, t, heads, hd))
            fox_v.append(v.reshape(b, t, heads, hd))
            fox_lf.append(lf.reshape(b, t, heads))
        else:
            wf = wts["mla"][j]
            w1, w2, wkn, wv = wf["rows"]
            qn, qr, ckv, kr = _mla_in(x2, wts["g_mix"][i], w1, wts["g_mla_q"][j], wts["g_mla_kv"][j],
                                      w2, cos, sin, mla_heads, wts["mla_scale"])
            qr_rows = jnp.swapaxes(qr.reshape(b, t, mla_heads, MLA_ROPE), 1, 2).reshape(b, mla_heads * t, MLA_ROPE)
            o = _mla_attn_sample(r3(qn), qr_rows, mla_past[0], krt_cache, j, r3(ckv), r3(kr), wkn, wv)
            mla_c.append(r3(ckv))
            mla_r.append(r3(kr))
        x, y = _after_mixer(x, o, wf["out"], i, mem_k, mem_v, wts)
    return (y, jnp.stack(fox_k), jnp.stack(fox_v), jnp.stack(fox_lf),
            jnp.stack(mla_c), jnp.stack(mla_r))


def kernel(x_prompt, x_sample, mem_prompt, cache_fox_k, cache_fox_v, cache_fox_logf, cache_mla_ckv, cache_mla_krope, cache_mem_k, cache_mem_v, g_mix, g_cross, g_mem, g_ffn, g_final, w_fox_in, b_fox_f, w_fox_out, w_mla_a, g_mla_q, g_mla_kv, w_mla_qb, w_mla_kvb, w_mla_out, w_x_q, w_x_kv, w_x_o, w_ffn_gu, w_ffn_down):
    depth, d = g_mix.shape
    fox_heads = b_fox_f.shape[1]
    x_heads = cache_mem_k.shape[3]
    q_lora = g_mla_q.shape[1]
    kv_lora = g_mla_kv.shape[1]
    mla_heads = w_mla_out.shape[1] // HEAD_DIM
    ff = w_ffn_down.shape[1]
    row = lambda g: [g[i].reshape(1, -1) for i in range(g.shape[0])]
    gu = w_ffn_gu.astype(BF16)
    wts = {
        "g_mix": row(g_mix), "g_cross": row(g_cross), "g_ffn": row(g_ffn), "g_final": g_final.reshape(1, d),
        "g_mla_q": row(g_mla_q), "g_mla_kv": row(g_mla_kv),
        "fox": [_prep_fox(w_fox_in[j], b_fox_f[j], w_fox_out[j], fox_heads) for j in range(w_fox_in.shape[0])],
        "mla": [_prep_mla(w_mla_a[j], w_mla_qb[j], w_mla_kvb[j], w_mla_out[j], q_lora, kv_lora, mla_heads)
                for j in range(w_mla_a.shape[0])],
        "cross": [(w_x_q[i].astype(BF16), w_x_o[i].astype(BF16)) for i in range(depth)],
        "ffn": [(gu[i, :, :ff], gu[i, :, ff:], w_ffn_down[i].astype(BF16)) for i in range(depth)],
        "fox_heads": fox_heads, "x_heads": x_heads, "mla_heads": mla_heads,
        "mla_scale": (HEAD_DIM + MLA_ROPE) ** -0.5,
    }
    bp, n_mem, _ = mem_prompt.shape
    mk, mv = _memory_kv(mem_prompt.reshape(bp * n_mem, d), g_mem.reshape(depth, 1, d), w_x_kv.astype(BF16))
    xw = mk.shape[-1]
    mk = mk.reshape(depth, bp, n_mem, xw)
    mv = mv.reshape(depth, bp, n_mem, xw)
    y_p, fk_p, fv_p, fl_p, mc_p, mr_p = _trunk_prompt(x_prompt, mk, mv, wts)
    past_len = cache_fox_k.shape[2]
    pos_s = past_len + jnp.arange(x_sample.shape[1], dtype=jnp.int32)
    bs = x_sample.shape[0]
    dh = xw // x_heads

    def in_memory_order(c):
        c = c.reshape(depth, bs, n_mem, x_heads, dh // LANE, LANE)
        return jnp.transpose(c, (0, 1, 2, 4, 3, 5)).reshape(depth, bs, n_mem * xw // LANE, LANE)

    y_s, fk_s, fv_s, fl_s, mc_s, mr_s = _trunk_sample(
        x_sample, pos_s, (cache_fox_k, cache_fox_v, cache_fox_logf), (cache_mla_ckv, cache_mla_krope),
        in_memory_order(cache_mem_k), in_memory_order(cache_mem_v), wts)
    return (y_p, y_s, fk_p, fv_p, fl_p, mc_p, mr_p,
            mk.reshape(depth, bp, n_mem, x_heads, dh), mv.reshape(depth, bp, n_mem, x_heads, dh),
            fk_s, fv_s, fl_s, mc_s, mr_s)
```

```python
import functools

import jax
import jax.numpy as jnp
from jax import lax
from jax.experimental import pallas as pl
from jax.experimental.pallas import tpu as pltpu

EPS = 1e-6
CHUNK = 64
assert CHUNK & (CHUNK - 1) == 0
ROPE_THETA = 10000.0
LANE = 128
HEAD_DIM = 64
PAIR = 2
MLA_ROPE = 32
NEG = -1e30
LOG2E = 1.4426950408889634
VMEM_LIMIT = 56 * 1024 * 1024
BF16 = jnp.bfloat16
F32 = jnp.float32


def _dot(a, b):
    return jnp.dot(a, b, preferred_element_type=F32)


def _dot_nt(a, b):
    return lax.dot_general(a, b, (((1,), (1,)), ((), ())), preferred_element_type=F32)


def _rms(x):
    return x * lax.rsqrt(jnp.mean(x * x, axis=-1, keepdims=True) + EPS)


def _params(*sem):
    return pltpu.CompilerParams(dimension_semantics=sem, vmem_limit_bytes=VMEM_LIMIT)


def _row_tile(m, cap=512):
    t = min(m, cap)
    assert m % t == 0
    return t


def _full(shape):
    return pl.BlockSpec(shape, lambda *_: (0,) * len(shape), pipeline_mode=pl.Buffered(1))


def _log_sigmoid(f):
    return jnp.minimum(f, 0.0) - jnp.log1p(jnp.exp(-jnp.abs(f)))


def _split3(x):
    hi = x.astype(BF16)
    r = x - hi.astype(F32)
    mid = r.astype(BF16)
    lo = (r - mid.astype(F32)).astype(BF16)
    return hi, mid, lo


def _layer_map(imap, layer, *idx):
    return (layer,) + tuple(imap(*idx))


def _stacked_call(body, *, grid, in_specs, args, outs, layer, prev, sem, name, scratch=()):
    out_specs, out_shape, stacked = [], [], []
    for k, (shape, dtype, blk, imap, n_layers) in enumerate(outs):
        if n_layers is None:
            out_specs.append(pl.BlockSpec(blk, imap))
            out_shape.append(jax.ShapeDtypeStruct(shape, dtype))
        else:
            out_specs.append(pl.BlockSpec((None,) + tuple(blk), functools.partial(_layer_map, imap, layer)))
            out_shape.append(jax.ShapeDtypeStruct((n_layers,) + tuple(shape), dtype))
            stacked.append(k)
    aliases = {}
    in_specs = list(in_specs)
    args = list(args)
    if prev is not None:
        for k, arr in zip(stacked, prev):
            aliases[len(args)] = k
            in_specs.append(pl.BlockSpec(memory_space=pl.ANY))
            args.append(arr)
    n_alias = len(aliases)

    def wrapped(*refs):
        n_in = len(args) - n_alias
        body(*refs[:n_in], *refs[n_in + n_alias:])

    res = pl.pallas_call(
        wrapped, grid=grid, in_specs=in_specs, out_specs=out_specs, out_shape=out_shape,
        input_output_aliases=aliases, scratch_shapes=list(scratch),
        compiler_params=_params(*sem), name=name,
    )(*args)
    return res, [res[k] for k in stacked]


def _fox_in_body(x_ref, g_ref, wq, wk, wv, wg, wf, bf_ref, q_out, k_out, v_out, gate_out, lf_out, *, scale, heads):
    h = (_rms(x_ref[...]) * g_ref[...]).astype(BF16)
    q_out[...] = (_dot(h, wq[...]) * scale).astype(BF16)
    k_out[...] = _dot(h, wk[...])
    v_out[...] = _dot(h, wv[...])
    gate_out[...] = jax.nn.sigmoid(_dot(h, wg[...]))
    lf_out[...] = _log_sigmoid(_dot(h, wf[...]) + bf_ref[...])[:, :heads]


def _fox_in(x, g, wq, wk, wv, wg, wf, bf, heads):
    m, d = x.shape
    w = wq.shape[1]
    tm = _row_tile(m)
    row = lambda n: pl.BlockSpec((tm, n), lambda i: (i, 0))
    return pl.pallas_call(
        functools.partial(_fox_in_body, scale=HEAD_DIM ** -0.5, heads=heads),
        grid=(m // tm,),
        in_specs=[row(d), _full((1, d)), _full((d, w)), _full((d, w)), _full((d, w)), _full((d, w)),
                  _full((d, LANE)), _full((1, LANE))],
        out_specs=[row(w), row(w), row(w), row(w), row(heads)],
        out_shape=[jax.ShapeDtypeStruct((m, w), BF16), jax.ShapeDtypeStruct((m, w), F32),
                   jax.ShapeDtypeStruct((m, w), F32), jax.ShapeDtypeStruct((m, w), F32),
                   jax.ShapeDtypeStruct((m, heads), F32)],
        compiler_params=_params("parallel"),
        name="fox_in",
    )(x, g, wq, wk, wv, wg, wf, bf)


def _fox_in_t_body(x_ref, g_ref, wqkvt, wg, wft, bf_ref,
                   qt_out, k_out, kt_out, vt_out, gate_out, lft_out, *, scale):
    h = (_rms(x_ref[...]) * g_ref[...]).astype(BF16)
    w = kt_out.shape[0]
    qkvt = _dot_nt(wqkvt[...], h)
    qt_out[...] = (qkvt[:w] * scale).astype(BF16)
    kt = qkvt[w:2 * w]
    kt_out[...] = kt
    k_out[...] = kt.T.astype(BF16)
    vt_out[...] = qkvt[2 * w:]
    gate_out[...] = jax.nn.sigmoid(_dot(h, wg[...]))
    lft_out[...] = _log_sigmoid(_dot_nt(wft[...], h) + bf_ref[...])


def _fox_in_t(x, g, wts, layer, n_layers, prev):
    wqkvt, wg, wft, bf = wts
    b, t, d = x.shape
    w = wg.shape[1]
    heads = wft.shape[0]
    tm = _row_tile(t)
    rows = lambda bi, i: (bi, i, 0)
    cols = lambda bi, i: (bi, 0, i)
    res, stacked = _stacked_call(
        functools.partial(_fox_in_t_body, scale=HEAD_DIM ** -0.5 * LOG2E),
        grid=(b, t // tm),
        in_specs=[pl.BlockSpec((None, tm, d), rows), _full((1, d)), _full((3 * w, d)),
                  _full((d, w)), _full((heads, d)), _full((heads, 1))],
        args=[x, g, wqkvt, wg, wft, bf],
        outs=[((b, w, t), BF16, (None, w, tm), cols, None),
              ((b, t, w), BF16, (None, tm, w), rows, None),
              ((b, w, t), F32, (None, w, tm), cols, n_layers),
              ((b, w, t), F32, (None, w, tm), cols, n_layers),
              ((b, t, w), F32, (None, tm, w), rows, None),
              ((b, heads, t), F32, (None, heads, tm), cols, n_layers)],
        layer=layer, prev=prev, sem=("parallel", "parallel"), name="fox_in_t")
    qt, k, _, _, gate, _ = res
    return qt, k, gate, stacked


CUM_BLOCK = 256


def _cumsum_lanes_body(lf_ref, out_ref):
    h, t = lf_ref.shape
    r = lax.broadcasted_iota(jnp.int32, (CUM_BLOCK, CUM_BLOCK), 0)
    c = lax.broadcasted_iota(jnp.int32, (CUM_BLOCK, CUM_BLOCK), 1)
    tri = jnp.where(r <= c, 1.0, 0.0).astype(BF16)
    carry = jnp.zeros((h, 1), F32)
    for b in range(t // CUM_BLOCK):
        hi, mid, lo = _split3(lf_ref[:, b * CUM_BLOCK:(b + 1) * CUM_BLOCK])
        out = _dot(hi, tri) + _dot(mid, tri) + _dot(lo, tri) + carry
        out_ref[:, b * CUM_BLOCK:(b + 1) * CUM_BLOCK] = out
        carry = out[:, CUM_BLOCK - 1:CUM_BLOCK]


def _cumsum_lanes(lf, layer=None):
    b, h, t = lf.shape[-3:]
    assert t % CUM_BLOCK == 0
    if layer is None:
        spec = pl.BlockSpec((None, h, t), lambda i: (i, 0, 0))
    else:
        spec = pl.BlockSpec((None, None, h, t), lambda i: (layer, i, 0, 0))
    return pl.pallas_call(
        _cumsum_lanes_body, grid=(b,),
        in_specs=[spec],
        out_specs=pl.BlockSpec((None, h, t), lambda i: (i, 0, 0)),
        out_shape=jax.ShapeDtypeStruct((b, h, t), F32),
        compiler_params=_params("parallel"), name="logf_cumsum_t",
    )(lf)


N_PIECE = 3
TRIP_BLOCKS = 4
V_ROWS = HEAD_DIM + 16
BIAS_ROWS = 16


def _attn_prompt_body(*refs, fox, tq):
    if fox:
        qt_ref, k_ref, vt_ref, crow_ref, ccol_ref, gate_ref, o_ref, kx, vx = refs
    else:
        qnt_ref, qrt_ref, kn_ref, krd_ref, vt_ref, o_ref, kx, vx = refs
    hp = pl.program_id(1)
    tk = tq
    t = kx.shape[0]
    ones_lo = PAIR * N_PIECE

    tail_row = lax.broadcasted_iota(jnp.int32, (V_ROWS - HEAD_DIM, t), 0)
    for e in range(PAIR):
        vx[e * V_ROWS:e * V_ROWS + HEAD_DIM, :] = vt_ref[e * HEAD_DIM:(e + 1) * HEAD_DIM, :].astype(BF16)
        vx[e * V_ROWS + HEAD_DIM:(e + 1) * V_ROWS, :] = jnp.where(tail_row == 0, 1.0, 0.0).astype(BF16)
    if fox:
        kx[:, :LANE] = k_ref[...]
        heads = ccol_ref.shape[1]
        hrow = lax.broadcasted_iota(jnp.int32, (heads, LANE), 0)
        lcol = lax.broadcasted_iota(jnp.int32, (heads, LANE), 1)
        ext = jnp.zeros((t, LANE), F32)
        for i, piece in enumerate(_split3(ccol_ref[...] * LOG2E)):
            hit = ((hrow == PAIR * hp) & (lcol == i)) | ((hrow == PAIR * hp + 1) & (lcol == N_PIECE + i))
            ext = ext + _dot(piece, jnp.where(hit, 1.0, 0.0).astype(BF16))
        lane = lax.broadcasted_iota(jnp.int32, (t, LANE), 1)
        ext = jnp.where((lane >= ones_lo) & (lane < ones_lo + N_PIECE), 1.0, ext)
        kx[:, LANE:] = ext.astype(BF16)
    else:
        kx[:, :LANE] = kn_ref[...]
        kx[:, LANE:] = krd_ref[...]

    lax.fori_loop(0, t // tq, functools.partial(_attn_query_block, refs, kx, vx, fox=fox, tq=tq), 0)


def _attn_query_block(refs, kx, vx, qi, _, *, fox, tq):
    if fox:
        qt_ref, _, _, crow_ref, _, gate_ref, o_ref = refs[:7]
    else:
        qnt_ref, qrt_ref, _, _, _, o_ref = refs[:6]
    tk = tq
    ones_lo = PAIR * N_PIECE
    cols = pl.ds(pl.multiple_of(qi * tq, tq), tq)

    sub = lax.broadcasted_iota(jnp.int32, (LANE, tq), 0)
    zero = jnp.zeros((), BF16)
    qs = []
    for e in range(PAIR):
        in_head = (sub >= e * HEAD_DIM) & (sub < (e + 1) * HEAD_DIM)
        if fox:
            sub16 = lax.broadcasted_iota(jnp.int32, (BIAS_ROWS, tq), 0)
            ext = jnp.where((sub16 >= N_PIECE * e) & (sub16 < N_PIECE * (e + 1)), -1.0, 0.0)
            for i, piece in enumerate(_split3(crow_ref[e:e + 1, cols] * LOG2E)):
                ext = jnp.where(sub16 == ones_lo + i, piece.astype(F32), ext)
            qs.append(jnp.concatenate([jnp.where(in_head, qt_ref[:, cols], zero), ext.astype(BF16),
                                       jnp.zeros((LANE - BIAS_ROWS, tq), BF16)], axis=0))
        else:
            in_rope = (sub >= e * MLA_ROPE) & (sub < (e + 1) * MLA_ROPE)
            qs.append(jnp.concatenate([jnp.where(in_head, qnt_ref[:, cols], zero),
                                       jnp.where(in_rope, qrt_ref[:, cols], zero)], axis=0))

    def blocks(j0, carry, nblk, diagonal_last):
        starts = [(j0 + u) * tk for u in range(nblk)]
        starts = [st if isinstance(st, int) else pl.multiple_of(st, tk) for st in starts]
        scores = [[_dot(kx[pl.ds(st, tk), :], qs[e]) for e in range(PAIR)] for st in starts]
        carry = list(carry)
        for e in range(PAIR):
            for u, st in enumerate(starts):
                m, acc = carry[2 * e:2 * e + 2]
                s = scores[u][e]
                if diagonal_last and u == nblk - 1:
                    key = lax.broadcasted_iota(jnp.int32, (tk, tq), 0)
                    qry = lax.broadcasted_iota(jnp.int32, (tk, tq), 1)
                    visible = (key <= qry) if fox else (key <= (qry | (CHUNK - 1)))
                    s = jnp.where(visible, s, NEG)
                m_new = jnp.maximum(m, jnp.max(s, axis=0, keepdims=True))
                p = jnp.exp2(s - m_new).astype(BF16)
                v_j = vx[e * V_ROWS:(e + 1) * V_ROWS, pl.ds(st, tk)]
                carry[2 * e:2 * e + 2] = [m_new, jnp.exp2(m - m_new) * acc + _dot(v_j, p)]
        return tuple(carry)

    init = (jnp.full((1, tq), NEG, F32), jnp.zeros((V_ROWS, tq), F32)) * PAIR
    n_q = kx.shape[0] // tq
    carry, left = init, qi
    if TRIP_BLOCKS < n_q <= 2 * TRIP_BLOCKS:
        full = jnp.where(qi >= TRIP_BLOCKS, 1, 0)
        left = qi - full * TRIP_BLOCKS
        carry = lax.fori_loop(0, full, lambda _, c: blocks(0, c, TRIP_BLOCKS, False), init)
    elif n_q > TRIP_BLOCKS:
        main = qi // TRIP_BLOCKS
        left = qi - main * TRIP_BLOCKS
        carry = lax.fori_loop(0, main, lambda i, c: blocks(TRIP_BLOCKS * i, c, TRIP_BLOCKS, False), init)
    for n in range(min(TRIP_BLOCKS, n_q)):
        carry = lax.fori_loop(0, jnp.where(left == n, 1, 0),
                              lambda _, c, n=n: blocks(qi - n, c, n + 1, True), carry)
    o = jnp.concatenate([acc[:HEAD_DIM] * (1.0 / acc[HEAD_DIM:HEAD_DIM + 1]) for acc in carry[1::2]], axis=0).T
    if fox:
        o = o * gate_ref[cols, :]
    o_ref[cols, :] = o.astype(BF16)
    return 0


def _fox_attn_prompt(qt, k, vt_all, layer, cum_t, gate, tq=512):
    b, w, t = qt.shape
    heads = cum_t.shape[1]
    tq = min(tq, t)
    npair = w // LANE
    rowblk = pl.BlockSpec((None, t, LANE), lambda bi, hp: (bi, 0, hp))
    return pl.pallas_call(
        functools.partial(_attn_prompt_body, fox=True, tq=tq),
        grid=(b, npair),
        in_specs=[pl.BlockSpec((None, LANE, t), lambda bi, hp: (bi, hp, 0)),
                  rowblk,
                  pl.BlockSpec((None, None, LANE, t), lambda bi, hp: (layer, bi, hp, 0)),
                  pl.BlockSpec((None, None, PAIR, t), lambda bi, hp: (bi, hp, 0, 0)),
                  pl.BlockSpec((None, t, heads), lambda bi, hp: (bi, 0, 0)),
                  rowblk],
        out_specs=rowblk,
        out_shape=jax.ShapeDtypeStruct((b, t, w), BF16),
        scratch_shapes=[pltpu.VMEM((t, 2 * LANE), BF16), pltpu.VMEM((PAIR * V_ROWS, t), BF16)],
        compiler_params=_params("parallel", "parallel"),
        name="fox_attn_prompt",
    )(qt, k, vt_all, cum_t.reshape(b, npair, PAIR, t), jnp.swapaxes(cum_t, 1, 2), gate)


def _mla_attn_prompt(qnt, qrt, kn, krd, vt, tq=512):
    b, w, t = qnt.shape
    tq = min(tq, t)
    npair = w // LANE
    colblk = pl.BlockSpec((None, LANE, t), lambda bi, hp: (bi, hp, 0))
    rowblk = pl.BlockSpec((None, t, LANE), lambda bi, hp: (bi, 0, hp))
    return pl.pallas_call(
        functools.partial(_attn_prompt_body, fox=False, tq=tq),
        grid=(b, npair),
        in_specs=[colblk, colblk, rowblk, pl.BlockSpec((None, t, LANE), lambda bi, hp: (bi, 0, 0)), colblk],
        out_specs=rowblk,
        out_shape=jax.ShapeDtypeStruct((b, t, w), BF16),
        scratch_shapes=[pltpu.VMEM((t, 2 * LANE), BF16), pltpu.VMEM((PAIR * V_ROWS, t), BF16)],
        compiler_params=_params("parallel", "parallel"),
        name="mla_attn_prompt",
    )(qnt, qrt, kn, krd, vt)


SAMPLE_PAIRS = 4


def _fox_attn_sample_body(q_ref, kt_ref, vt_ref, kn_ref, vn_ref, cq_ref, ck_ref, gate_ref, o_ref):
    t = q_ref.shape[0]
    p = kt_ref.shape[1]
    rows = PAIR * t
    lane = lax.broadcasted_iota(jnp.int32, (rows, LANE), 1)
    row = lax.broadcasted_iota(jnp.int32, (rows, LANE), 0)
    hl = lax.broadcasted_iota(jnp.int32, cq_ref.shape, 1)
    first_p = lax.broadcasted_iota(jnp.int32, (rows, p), 0) < t
    rn = lax.broadcasted_iota(jnp.int32, (rows, t), 0)
    cn = lax.broadcasted_iota(jnp.int32, (rows, t), 1)
    for pp in range(SAMPLE_PAIRS):
        hp = pl.program_id(1) * SAMPLE_PAIRS + pp
        lanes = slice(pp * LANE, (pp + 1) * LANE)
        q2 = jnp.concatenate([q_ref[:, lanes]] * PAIR, axis=0)
        q2 = jnp.where((lane // HEAD_DIM) == (row // t), q2, jnp.zeros((), BF16))
        cq = jnp.concatenate([jnp.sum(jnp.where(hl == PAIR * hp + e, cq_ref[...], 0.0), axis=1, keepdims=True)
                              for e in range(PAIR)], axis=0)
        ck = ck_ref[pp]
        s_p = _dot(q2, kt_ref[lanes, :].astype(BF16)) + cq - jnp.where(first_p, ck[0:1, :p], ck[1:2, :p])
        s_n = _dot_nt(q2, kn_ref[:, lanes].astype(BF16)) + cq - jnp.where(rn < t, ck[0:1, p:], ck[1:2, p:])
        s_n = jnp.where(cn <= lax.rem(rn, t), s_n, NEG)
        m = jnp.maximum(jnp.max(s_p, axis=-1, keepdims=True), jnp.max(s_n, axis=-1, keepdims=True))
        p_p = jnp.exp(s_p - m)
        p_n = jnp.exp(s_n - m)
        l = jnp.sum(p_p, axis=-1, keepdims=True) + jnp.sum(p_n, axis=-1, keepdims=True)
        o = (_dot_nt(p_p.astype(BF16), vt_ref[lanes, :].astype(BF16))
             + _dot(p_n.astype(BF16), vn_ref[:, lanes].astype(BF16))) / l
        o = jnp.where(lax.broadcasted_iota(jnp.int32, (t, LANE), 1) < HEAD_DIM, o[:t], o[t:]) * gate_ref[:, lanes]
        o_ref[:, lanes] = o.astype(BF16)


def _fox_attn_sample(q, kt_cache, vt_cache, layer, k_new, v_new, cum_t, gate):
    b, t, w = q.shape
    p = kt_cache.shape[-1]
    heads = cum_t.shape[1]
    npair = w // LANE
    wide = SAMPLE_PAIRS * LANE
    new = pl.BlockSpec((None, t, wide), lambda bi, hp: (bi, 0, hp))
    past = pl.BlockSpec((None, None, wide, p), lambda bi, hp: (layer, bi, hp, 0))
    return pl.pallas_call(
        _fox_attn_sample_body,
        grid=(b, npair // SAMPLE_PAIRS),
        in_specs=[new, past, past, new, new,
                  pl.BlockSpec((None, t, heads), lambda bi, hp: (bi, 0, 0)),
                  pl.BlockSpec((None, SAMPLE_PAIRS, PAIR, p + t), lambda bi, hp: (bi, hp, 0, 0)),
                  new],
        out_specs=new,
        out_shape=jax.ShapeDtypeStruct((b, t, w), BF16),
        compiler_params=_params("parallel", "parallel"),
        name="fox_attn_sample",
    )(q, kt_cache, vt_cache, k_new, v_new, jnp.swapaxes(cum_t[:, :, p:], 1, 2),
      cum_t.reshape(b, npair, PAIR, p + t), gate)


def _mla_attn_sample_body(qn_ref, qr_ref, cp_ref, krp_ref, cn_ref, krn_ref, wkn_ref, wv_ref, o_ref, *, past_len):
    t, w = qn_ref.shape
    p = cp_ref.shape[0]
    heads = w // HEAD_DIM
    rows = heads * t
    lane = lax.broadcasted_iota(jnp.int32, (rows, w), 1)
    row = lax.broadcasted_iota(jnp.int32, (rows, w), 0)
    own = (lane // HEAD_DIM) == (row // t)
    q_wide = jnp.where(own, jnp.concatenate([qn_ref[...]] * heads, axis=0), jnp.zeros((), BF16))
    q_lat = _dot_nt(q_wide, wkn_ref[...]).astype(BF16)
    qr = qr_ref[...]
    c_p = cp_ref[...].astype(BF16)
    c_n = cn_ref[...].astype(BF16)
    s_p = _dot_nt(q_lat, c_p) + _dot(qr, krp_ref[...].astype(BF16))
    s_n = _dot_nt(q_lat, c_n) + _dot_nt(qr, krn_ref[...].astype(BF16))
    q_chunk_p = (past_len + lax.rem(lax.broadcasted_iota(jnp.int32, (rows, p), 0), t)) // CHUNK
    s_p = jnp.where((lax.broadcasted_iota(jnp.int32, (rows, p), 1) // CHUNK) <= q_chunk_p, s_p, NEG)
    q_chunk_n = (past_len + lax.rem(lax.broadcasted_iota(jnp.int32, (rows, t), 0), t)) // CHUNK
    s_n = jnp.where(((past_len + lax.broadcasted_iota(jnp.int32, (rows, t), 1)) // CHUNK) <= q_chunk_n, s_n, NEG)
    m = jnp.maximum(jnp.max(s_p, axis=-1, keepdims=True), jnp.max(s_n, axis=-1, keepdims=True))
    p_p = jnp.exp(s_p - m)
    p_n = jnp.exp(s_n - m)
    l = jnp.sum(p_p, axis=-1, keepdims=True) + jnp.sum(p_n, axis=-1, keepdims=True)
    o_lat = (_dot(p_p.astype(BF16), c_p) + _dot(p_n.astype(BF16), c_n)) / l
    o_wide = jnp.where(own, _dot(o_lat.astype(BF16), wv_ref[...]), 0.0)
    o = o_wide[:t]
    for h in range(1, heads):
        o = o + o_wide[h * t:(h + 1) * t]
    o_ref[...] = o.astype(BF16)


def _mla_attn_sample(qn, qr_rows, ckv_cache, krt_cache, layer, ckv_new, kr_new, wkn, wv):
    b, t, w = qn.shape
    p, c = ckv_cache.shape[2:]
    rows = qr_rows.shape[1]
    blk = lambda *s: pl.BlockSpec((None,) + s, lambda bi: (bi,) + (0,) * len(s))
    past = lambda *s: pl.BlockSpec((None, None) + s, lambda bi: (layer, bi) + (0,) * len(s))
    return pl.pallas_call(
        functools.partial(_mla_attn_sample_body, past_len=p),
        grid=(b,),
        in_specs=[blk(t, w), blk(rows, MLA_ROPE), past(p, c), past(MLA_ROPE, p), blk(t, c), blk(t, MLA_ROPE),
                  _full(wkn.shape), _full(wv.shape)],
        out_specs=blk(t, w),
        out_shape=jax.ShapeDtypeStruct((b, t, w), BF16),
        compiler_params=_params("parallel"),
        name="mla_attn_sample",
    )(qn, qr_rows, ckv_cache, krt_cache, ckv_new, kr_new, wkn, wv)


def _mla_in_body(x_ref, g_ref, w1, gq_ref, gkv_ref, w2, cos_ref, sin_ref, cosq_ref, sinq_ref,
                 qn_out, qr_out, ckv_out, kr_out, *, q_lora, kv_lora, scale):
    h = (_rms(x_ref[...]) * g_ref[...]).astype(BF16)
    a = _dot(h, w1[...])
    c_q = (_rms(a[:, :q_lora]) * gq_ref[...]).astype(BF16)
    ckv_out[...] = _rms(a[:, q_lora:q_lora + kv_lora]) * gkv_ref[...]
    o = q_lora + kv_lora
    krd = a[:, o:o + LANE] * cos_ref[...] + a[:, o + LANE:o + 2 * LANE] * sin_ref[...]
    kr_out[...] = krd[:, :MLA_ROPE]
    w = qn_out.shape[1]
    r = qr_out.shape[1]
    qn_out[...] = (_dot(c_q, w2[:, :w]) * scale).astype(BF16)
    qr = _dot(c_q, w2[:, w:w + r]) * cosq_ref[...] + _dot(c_q, w2[:, w + r:w + 2 * r]) * sinq_ref[...]
    qr_out[...] = (qr * scale).astype(BF16)


def _mla_in(x, g, w1, gq, gkv, w2, cos, sin, heads, scale):
    m, d = x.shape
    q_lora = gq.shape[1]
    kv_lora = gkv.shape[1]
    w = heads * HEAD_DIM
    r = heads * MLA_ROPE
    tm = _row_tile(m)
    t = cos.shape[0]
    assert tm % t == 0
    per_head = lambda tab: jnp.tile(tab[:, :MLA_ROPE], (tm // t, heads))
    cosq, sinq = per_head(cos), per_head(sin)
    cos, sin = jnp.tile(cos, (tm // t, 1)), jnp.tile(sin, (tm // t, 1))
    row = lambda n: pl.BlockSpec((tm, n), lambda i: (i, 0))
    return pl.pallas_call(
        functools.partial(_mla_in_body, q_lora=q_lora, kv_lora=kv_lora, scale=scale),
        grid=(m // tm,),
        in_specs=[row(d), _full((1, d)), _full(w1.shape), _full((1, q_lora)), _full((1, kv_lora)),
                  _full(w2.shape), _full((tm, LANE)), _full((tm, LANE)), _full((tm, r)), _full((tm, r))],
        out_specs=[row(w), row(r), row(kv_lora), row(MLA_ROPE)],
        out_shape=[jax.ShapeDtypeStruct((m, w), BF16), jax.ShapeDtypeStruct((m, r), BF16),
                   jax.ShapeDtypeStruct((m, kv_lora), F32), jax.ShapeDtypeStruct((m, MLA_ROPE), F32)],
        compiler_params=_params("parallel"),
        name="mla_in",
    )(x, g, w1, gq, gkv, w2, cos, sin, cosq, sinq)


def _mla_in_t_body(x_ref, g_ref, w1, gq_ref, gkv_ref, wq3t, wkn, wvt,
                   cos_ref, sin_ref, cost_ref, sint_ref,
                   qnt_out, qrt_out, kn_out, krd_out, vt_out, ckv_out, krt_out, *, q_lora, kv_lora, scale):
    h = (_rms(x_ref[...]) * g_ref[...]).astype(BF16)
    a = _dot(h, w1[...])
    c_q = (_rms(a[:, :q_lora]) * gq_ref[...]).astype(BF16)
    c_kv = _rms(a[:, q_lora:q_lora + kv_lora]) * gkv_ref[...]
    ckv_out[...] = c_kv
    o = q_lora + kv_lora
    krd = a[:, o:o + LANE] * cos_ref[...] + a[:, o + LANE:o + 2 * LANE] * sin_ref[...]
    krd_out[...] = krd.astype(BF16)
    krt_out[...] = krd.T[:MLA_ROPE]
    cost = cost_ref[...]
    sint = sint_ref[...]
    c_kv = c_kv.astype(BF16)
    kn_out[...] = _dot(c_kv, wkn[...]).astype(BF16)
    vt_out[...] = _dot_nt(wvt[...], c_kv).astype(BF16)
    w = qnt_out.shape[0]
    q3 = _dot_nt(wq3t[...], c_q)
    qnt_out[...] = (q3[:w] * scale).astype(BF16)
    for p in range(w // LANE):
        qa = q3[w + p * LANE:w + (p + 1) * LANE]
        qb = q3[2 * w + p * LANE:2 * w + (p + 1) * LANE]
        qrt_out[p * LANE:(p + 1) * LANE, :] = ((qa * cost + qb * sint) * scale).astype(BF16)


def _mla_in_t(x, g, gq, gkv, wts, tables, scale, layer, n_layers, prev):
    w1, wq3t, wkn, wvt = wts
    cos, sin, cost, sint = tables
    b, t, d = x.shape
    q_lora = gq.shape[1]
    kv_lora = gkv.shape[1]
    w = wkn.shape[1]
    tm = _row_tile(t)
    rows = lambda bi, i: (bi, i, 0)
    cols = lambda bi, i: (bi, 0, i)
    res, stacked = _stacked_call(
        functools.partial(_mla_in_t_body, q_lora=q_lora, kv_lora=kv_lora, scale=scale),
        grid=(b, t // tm),
        in_specs=[pl.BlockSpec((None, tm, d), rows), _full((1, d)), _full(w1.shape), _full((1, q_lora)),
                  _full((1, kv_lora)), _full(wq3t.shape), _full(wkn.shape), _full(wvt.shape),
                  pl.BlockSpec((tm, LANE), lambda bi, i: (i, 0)), pl.BlockSpec((tm, LANE), lambda bi, i: (i, 0)),
                  pl.BlockSpec((LANE, tm), lambda bi, i: (0, i)), pl.BlockSpec((LANE, tm), lambda bi, i: (0, i))],
        args=[x, g, w1, gq, gkv, wq3t, wkn, wvt, cos, sin, cost, sint],
        outs=[((b, w, t), BF16, (None, w, tm), cols, None),
              ((b, w, t), BF16, (None, w, tm), cols, None),
              ((b, t, w), BF16, (None, tm, w), rows, None),
              ((b, t, LANE), BF16, (None, tm, LANE), rows, None),
              ((b, w, t), BF16, (None, w, tm), cols, None),
              ((b, t, kv_lora), F32, (None, tm, kv_lora), rows, n_layers),
              ((b, MLA_ROPE, t), F32, (None, MLA_ROPE, tm), cols, n_layers)],
        layer=layer, prev=prev, sem=("parallel", "parallel"), name="mla_in_t")
    return res[:5], stacked


def _memory_kv_body(m_ref, g_ref, w_ref, k_out, v_out):
    h = (_rms(m_ref[...]) * g_ref[...]).astype(BF16)
    n = k_out.shape[-1]
    k_out[...] = _dot(h, w_ref[:, :n])
    v_out[...] = _dot(h, w_ref[:, n:])


def _memory_kv(mem, g_mem, w_kv):
    m, d = mem.shape
    depth, _, n2 = w_kv.shape
    n = n2 // 2
    tm = _row_tile(m)
    out = pl.BlockSpec((None, tm, n), lambda l, i: (l, i, 0))
    return pl.pallas_call(
        _memory_kv_body, grid=(depth, m // tm),
        in_specs=[pl.BlockSpec((tm, d), lambda l, i: (i, 0)),
                  pl.BlockSpec((None, 1, d), lambda l, i: (l, 0, 0)),
                  pl.BlockSpec((None, d, n2), lambda l, i: (l, 0, 0))],
        out_specs=[out, out],
        out_shape=[jax.ShapeDtypeStruct((depth, m, n), F32)] * 2,
        compiler_params=_params("parallel", "parallel"), name="memory_kv",
    )(mem, g_mem, w_kv)


FF_BLOCK = 256
SHORT_SEQ_ROWS = 128


def _layer_tail_body(o_ref, x_ref, wout, gc_ref, wq, mk_ref, mv_ref, wxo, gf_ref, wg, wu, wd, *rest, heads, final):
    if final:
        gl_ref, x_out, y_out, att = rest
    else:
        x_out, att = rest
    bb, tm, d = x_ref.shape
    x = x_ref[...].reshape(bb * tm, d) + _dot(o_ref[...].reshape(bb * tm, o_ref.shape[-1]), wout[...])
    h = (_rms(x) * gc_ref[...]).astype(BF16)
    dh = wq.shape[1] // heads
    q = (_dot(h, wq[...]) * dh ** -0.5).astype(BF16)
    for b in range(bb):
        for hd in range(heads):
            cols = slice(hd * dh, (hd + 1) * dh)
            q_h = q[b * tm:(b + 1) * tm, cols]
            if mk_ref.shape[-1] == wq.shape[1]:
                k_parts, v_parts = [mk_ref[b, :, cols]], [mv_ref[b, :, cols]]
            else:
                parts = dh // LANE
                pick = lambda ref, c: ref[b, pl.ds(c * heads + hd, ref.shape[1] // (heads * parts),
                                                   stride=heads * parts), :]
                k_parts = [pick(mk_ref, c) for c in range(parts)]
                v_parts = [pick(mv_ref, c) for c in range(parts)]
            width = dh // len(k_parts)
            s = sum(_dot_nt(q_h[:, c * width:(c + 1) * width], kp.astype(BF16)) for c, kp in enumerate(k_parts))
            p = jnp.exp(s - jnp.max(s, axis=-1, keepdims=True))
            pb = p.astype(BF16)
            o = jnp.concatenate([_dot(pb, vp.astype(BF16)) for vp in v_parts], axis=1)
            o = o / jnp.sum(p, axis=-1, keepdims=True)
            att[b * tm:(b + 1) * tm, cols] = o.astype(BF16)
    x = x + _dot(att[...], wxo[...])
    h = (_rms(x) * gf_ref[...]).astype(BF16)
    for c in range(wg.shape[1] // FF_BLOCK):
        cols = slice(c * FF_BLOCK, (c + 1) * FF_BLOCK)
        gate = _dot(h, wg[:, cols])
        up = _dot(h, wu[:, cols])
        x = x + _dot((gate * jax.nn.sigmoid(gate) * up).astype(BF16), wd[cols, :])
    x_out[...] = x.reshape(bb, tm, d)
    if final:
        y_out[...] = (_rms(x) * gl_ref[...]).reshape(bb, tm, d)


def _layer_tail(o, x, w_out, g_cross, wq, mk, mv, layer, wxo, g_ffn, wg, wu, wd, heads, g_final=None):
    b, t, d = x.shape
    w = o.shape[-1]
    xw = wq.shape[1]
    ff = wg.shape[1]
    assert ff % FF_BLOCK == 0
    tm = min(t, 512)
    bb = max(1, min(b, SHORT_SEQ_ROWS // tm))
    blk = lambda c: pl.BlockSpec((bb, tm, c), lambda bi, ti: (bi, ti, 0))
    mblk = pl.BlockSpec((None, bb) + mk.shape[2:], lambda bi, ti: (layer, bi, 0, 0),
                        pipeline_mode=pl.Buffered(1) if bb > 1 else None)
    final = g_final is not None
    ins = [blk(w), blk(d), _full((w, d)), _full((1, d)), _full((d, xw)), mblk, mblk, _full((xw, d)),
           _full((1, d)), _full((d, ff)), _full((d, ff)), _full((ff, d))]
    args = [o, x, w_out, g_cross, wq, mk, mv, wxo, g_ffn, wg, wu, wd]
    if final:
        ins.append(_full((1, d)))
        args.append(g_final)
    out = jax.ShapeDtypeStruct((b, t, d), F32)
    return pl.pallas_call(
        functools.partial(_layer_tail_body, heads=heads, final=final),
        grid=(b // bb, t // tm),
        in_specs=ins,
        out_specs=[blk(d), blk(d)] if final else blk(d),
        out_shape=[out, out] if final else out,
        scratch_shapes=[pltpu.VMEM((bb * tm, xw), BF16)],
        compiler_params=_params("parallel", "parallel"), name="layer_tail",
    )(*args)


def _prep_fox(w_in, b_f, w_out, heads):
    width = w_out.shape[0]
    w_in = w_in.astype(BF16)
    wq, wk, wv, wf, wg = jnp.split(w_in, [width, 2 * width, 3 * width, 3 * width + heads], axis=1)
    rows = (wq, wk, wv, wg, jnp.pad(wf, ((0, 0), (0, LANE - heads))),
            jnp.pad(b_f, (0, LANE - heads)).reshape(1, LANE))
    cols = (w_in[:, :3 * width].T, wg, wf.T, b_f.reshape(heads, 1))
    return {"rows": rows, "cols": cols, "out": w_out.astype(BF16)}


def _prep_mla(w_a, w_qb, w_kvb, w_out, q_lora, kv_lora, heads):
    half = MLA_ROPE // 2
    w_a = w_a.astype(BF16)
    d = w_a.shape[0]
    x1 = w_a[:, q_lora + kv_lora:q_lora + kv_lora + half]
    x2 = w_a[:, q_lora + kv_lora + half:]
    zeros = jnp.zeros((d, LANE - 2 * MLA_ROPE), BF16)
    w1 = jnp.concatenate([w_a[:, :q_lora + kv_lora], x1, x2, x1, x2, zeros, x2, x1, x2, x1, zeros], axis=1)
    qb = w_qb.astype(BF16).reshape(q_lora, heads, HEAD_DIM + MLA_ROPE)
    wqn = qb[:, :, :HEAD_DIM].reshape(q_lora, heads * HEAD_DIM)
    rope = qb[:, :, HEAD_DIM:]
    swapped = jnp.concatenate([rope[..., half:], rope[..., :half]], axis=-1)

    def pack(r):
        r = r.reshape(q_lora, heads // PAIR, PAIR * MLA_ROPE)
        return jnp.pad(r, ((0, 0), (0, 0), (0, LANE - PAIR * MLA_ROPE))).reshape(q_lora, -1)

    wqa, wqb = pack(rope), pack(swapped)
    kvb = w_kvb.astype(BF16).reshape(kv_lora, heads, 2 * HEAD_DIM)
    wkn = kvb[:, :, :HEAD_DIM].reshape(kv_lora, heads * HEAD_DIM)
    wv = kvb[:, :, HEAD_DIM:].reshape(kv_lora, heads * HEAD_DIM)
    flat = lambda r: r.reshape(q_lora, heads * MLA_ROPE)
    rows = (w1, jnp.concatenate([wqn, flat(rope), flat(swapped)], axis=1), wkn, wv)
    cols = (w1, jnp.concatenate([wqn, wqa, wqb], axis=1).T, wkn, wv.T)
    return {"rows": rows, "cols": cols, "out": w_out.astype(BF16)}


def _rope_tables(pos):
    half = MLA_ROPE // 2
    inv = ROPE_THETA ** (-jnp.arange(half, dtype=F32) / half)
    ang = pos.astype(F32)[:, None] * inv[None, :]
    cos, sin = jnp.cos(ang), jnp.sin(ang)
    z = jnp.zeros((pos.shape[0], LANE - 2 * MLA_ROPE), F32)
    return (jnp.concatenate([cos, cos, cos, cos, z], axis=1),
            jnp.concatenate([-sin, sin, -sin, sin, z], axis=1))


def _after_mixer(x, o, w_out, i, mem_k, mem_v, wts):
    wxq, wxo = wts["cross"][i]
    wgt, wup, wdn = wts["ffn"][i]
    last = i == len(wts["ffn"]) - 1
    res = _layer_tail(o, x, w_out, wts["g_cross"][i], wxq, mem_k, mem_v, i, wxo, wts["g_ffn"][i], wgt, wup, wdn,
                      wts["x_heads"], wts["g_final"] if last else None)
    return res if last else (res, None)


def _trunk_prompt(x, mem_k, mem_v, wts):
    b, t, d = x.shape
    m = b * t
    depth = len(wts["ffn"])
    n_fox, n_mla = len(wts["fox"]), len(wts["mla"])
    cos, sin = _rope_tables(jnp.arange(t, dtype=jnp.int32))
    tables = (cos, sin, cos.T, sin.T)
    fox_state = mla_state = None
    for i in range(depth):
        j = i // 2
        if i % 2 == 0:
            wf = wts["fox"][j]
            qt, k, gate, fox_state = _fox_in_t(x, wts["g_mix"][i], wf["cols"], j, n_fox, fox_state)
            cum_t = _cumsum_lanes(fox_state[2], j)
            o = _fox_attn_prompt(qt, k, fox_state[1], j, cum_t, gate)
        else:
            wf = wts["mla"][j]
            (qnt, qrt, kn, krd, vt), mla_state = _mla_in_t(
                x, wts["g_mix"][i], wts["g_mla_q"][j], wts["g_mla_kv"][j], wf["cols"], tables,
                wts["mla_scale"] * LOG2E, j, n_mla, mla_state)
            o = _mla_attn_prompt(qnt, qrt, kn, krd, vt)
        x, y = _after_mixer(x, o, wf["out"], i, mem_k, mem_v, wts)
    kt, vt, lft = fox_state
    ckv, krt = mla_state
    heads = lft.shape[2]
    unfold = lambda a: jnp.transpose(a.reshape(n_fox, b, heads, a.shape[2] // heads, t), (0, 1, 4, 2, 3))
    return (y.reshape(b, t, d), unfold(kt), unfold(vt), jnp.swapaxes(lft, 2, 3), ckv, jnp.swapaxes(krt, 2, 3))


def _trunk_sample(x, pos, fox_past, mla_past, mem_k, mem_v, wts):
    b, t, d = x.shape
    m = b * t
    depth = len(wts["ffn"])
    cos, sin = _rope_tables(pos)
    heads = wts["fox_heads"]
    mla_heads = wts["mla_heads"]
    fox_k, fox_v, fox_lf, mla_c, mla_r = [], [], [], [], []
    r3 = lambda a: a.reshape(b, t, a.shape[-1])
    n_fox, _, p, _, hd = fox_past[0].shape
    kt_cache = jnp.transpose(fox_past[0], (0, 1, 3, 4, 2)).reshape(n_fox, b, heads * hd, p)
    vt_cache = jnp.transpose(fox_past[1], (0, 1, 3, 4, 2)).reshape(n_fox, b, heads * hd, p)
    lft_cache = jnp.swapaxes(fox_past[2], 2, 3)
    krt_cache = jnp.swapaxes(mla_past[1], 2, 3)
    total = p + t
    pad = jnp.zeros((b, heads, -(-total // CUM_BLOCK) * CUM_BLOCK - total), F32)
    for i in range(depth):
        j = i // 2
        x2 = x.reshape(m, d)
        if i % 2 == 0:
            wf = wts["fox"][j]
            q, k, v, gate, lf = _fox_in(x2, wts["g_mix"][i], *wf["rows"], heads)
            lf_all = jnp.concatenate([lft_cache[j], jnp.swapaxes(r3(lf), 1, 2), pad], axis=2)
            cum_t = _cumsum_lanes(lf_all)[:, :, :total]
            o = _fox_attn_sample(r3(q), kt_cache, vt_cache, j, r3(k), r3(v), cum_t, r3(gate))
            fox_k.append(k.reshape(b, t, heads, hd))
            fox_v.append(v.reshape(b, t, heads, hd))
            fox_lf.append(lf.reshape(b, t, heads))
        else:
            wf = wts["mla"][j]
            w1, w2, wkn, wv = wf["rows"]
            qn, qr, ckv, kr = _mla_in(x2, wts["g_mix"][i], w1, wts["g_mla_q"][j], wts["g_mla_kv"][j],
                                      w2, cos, sin, mla_heads, wts["mla_scale"])
            qr_rows = jnp.swapaxes(qr.reshape(b, t, mla_heads, MLA_ROPE), 1, 2).reshape(b, mla_heads * t, MLA_ROPE)
            o = _mla_attn_sample(r3(qn), qr_rows, mla_past[0], krt_cache, j, r3(ckv), r3(kr), wkn, wv)
            mla_c.append(r3(ckv))
            mla_r.append(r3(kr))
        x, y = _after_mixer(x, o, wf["out"], i, mem_k, mem_v, wts)
    return (y, jnp.stack(fox_k), jnp.stack(fox_v), jnp.stack(fox_lf),
            jnp.stack(mla_c), jnp.stack(mla_r))


def kernel(x_prompt, x_sample, mem_prompt, cache_fox_k, cache_fox_v, cache_fox_logf, cache_mla_ckv, cache_mla_krope, cache_mem_k, cache_mem_v, g_mix, g_cross, g_mem, g_ffn, g_final, w_fox_in, b_fox_f, w_fox_out, w_mla_a, g_mla_q, g_mla_kv, w_mla_qb, w_mla_kvb, w_mla_out, w_x_q, w_x_kv, w_x_o, w_ffn_gu, w_ffn_down):
    depth, d = g_mix.shape
    fox_heads = b_fox_f.shape[1]
    x_heads = cache_mem_k.shape[3]
    q_lora = g_mla_q.shape[1]
    kv_lora = g_mla_kv.shape[1]
    mla_heads = w_mla_out.shape[1] // HEAD_DIM
    ff = w_ffn_down.shape[1]
    row = lambda g: [g[i].reshape(1, -1) for i in range(g.shape[0])]
    gu = w_ffn_gu.astype(BF16)
    wts = {
        "g_mix": row(g_mix), "g_cross": row(g_cross), "g_ffn": row(g_ffn), "g_final": g_final.reshape(1, d),
        "g_mla_q": row(g_mla_q), "g_mla_kv": row(g_mla_kv),
        "fox": [_prep_fox(w_fox_in[j], b_fox_f[j], w_fox_out[j], fox_heads) for j in range(w_fox_in.shape[0])],
        "mla": [_prep_mla(w_mla_a[j], w_mla_qb[j], w_mla_kvb[j], w_mla_out[j], q_lora, kv_lora, mla_heads)
                for j in range(w_mla_a.shape[0])],
        "cross": [(w_x_q[i].astype(BF16), w_x_o[i].astype(BF16)) for i in range(depth)],
        "ffn": [(gu[i, :, :ff], gu[i, :, ff:], w_ffn_down[i].astype(BF16)) for i in range(depth)],
        "fox_heads": fox_heads, "x_heads": x_heads, "mla_heads": mla_heads,
        "mla_scale": (HEAD_DIM + MLA_ROPE) ** -0.5,
    }
    bp, n_mem, _ = mem_prompt.shape
    mk, mv = _memory_kv(mem_prompt.reshape(bp * n_mem, d), g_mem.reshape(depth, 1, d), w_x_kv.astype(BF16))
    xw = mk.shape[-1]
    mk = mk.reshape(depth, bp, n_mem, xw)
    mv = mv.reshape(depth, bp, n_mem, xw)
    y_p, fk_p, fv_p, fl_p, mc_p, mr_p = _trunk_prompt(x_prompt, mk, mv, wts)
    past_len = cache_fox_k.shape[2]
    pos_s = past_len + jnp.arange(x_sample.shape[1], dtype=jnp.int32)
    bs = x_sample.shape[0]
    dh = xw // x_heads

    def in_memory_order(c):
        c = c.reshape(depth, bs, n_mem, x_heads, dh // LANE, LANE)
        return jnp.transpose(c, (0, 1, 2, 4, 3, 5)).reshape(depth, bs, n_mem * xw // LANE, LANE)

    y_s, fk_s, fv_s, fl_s, mc_s, mr_s = _trunk_sample(
        x_sample, pos_s, (cache_fox_k, cache_fox_v, cache_fox_logf), (cache_mla_ckv, cache_mla_krope),
        in_memory_order(cache_mem_k), in_memory_order(cache_mem_v), wts)
    return (y_p, y_s, fk_p, fv_p, fl_p, mc_p, mr_p,
            mk.reshape(depth, bp, n_mem, x_heads, dh), mv.reshape(depth, bp, n_mem, x_heads, dh),
            fk_s, fv_s, fl_s, mc_s, mr_s)
```
